```python
import math
import jax, jax.numpy as jnp
from jax import lax
import numpy as np

D_MODEL = 4096
BATCH = 2
SEQ = 8192
DEPTH = 2
DEC_BATCH = 2
DEC_SEQ = 4096
PAST_LEN = 128

HEAD_DIM = 128
BLOCK = 128
WINDOW = 128
A_HEADS = 16
A_KV_HEADS = 4
A_GROUP = A_HEADS // A_KV_HEADS
B_HEADS = 16
B_DK = 128
B_DV = 128
CONV_WIDTH = 5
CHUNK = 64
C_HEADS = 16
C_QK_DIM = 128
C_V_DIM = 2 * C_QK_DIM
N_BUCKETS = 32
MAX_DISTANCE = 128
BIAS_HEADS = 16
MEM_TOKENS = 256
X_HEADS = 4
X_HEAD_DIM = 128
N_EXPERTS = 64
TOP_K = 8
N_GROUPS = 8
TOPK_GROUPS = 4
EXPERT_FF = 128
SHARED_FF = 512
ROUTED_SCALE = 2.5
EPS = 1e-6

A_Q = A_HEADS * HEAD_DIM
A_KV = A_KV_HEADS * HEAD_DIM
B_QK = B_HEADS * B_DK
B_V = B_HEADS * B_DV
B_CONV = 2 * B_QK + B_V
EVEN_IN = A_Q + 2 * A_KV + B_CONV + B_V + 4 * B_HEADS
EVEN_MIX = A_Q + B_V
C_QK = 2 * C_HEADS * C_QK_DIM
ODD_IN = 2 * C_QK + C_HEADS * C_V_DIM
ODD_MIX = C_HEADS * C_V_DIM
X_WIDTH = X_HEADS * X_HEAD_DIM
N_EVEN = (DEPTH + 1) // 2
N_ODD = DEPTH // 2

kernel_name = 'hybrid_bidir_encoder_two_groups'


def rmsnorm(x, g):
    xf = x.astype(jnp.float32)
    y = xf * lax.rsqrt(jnp.mean(xf * xf, axis=-1, keepdims=True) + EPS)
    return (y * g.astype(jnp.float32)).astype(x.dtype)


def l2norm(x):
    xf = x.astype(jnp.float32)
    return xf * lax.rsqrt(jnp.sum(xf * xf, axis=-1, keepdims=True) + EPS)


def t5_bucket(rel):
    nb = N_BUCKETS // 2
    max_exact = nb // 2
    ret = jnp.where(rel > 0, nb, 0)
    n = jnp.abs(rel)
    n_f = jnp.maximum(n, max_exact).astype(jnp.float32)
    large = max_exact + (jnp.log(n_f / max_exact) / math.log(MAX_DISTANCE / max_exact) * (nb - max_exact)).astype(jnp.int32)
    large = jnp.minimum(large, nb - 1)
    return ret + jnp.where(n < max_exact, n, large)


def t5_bias(table, rel):
    return jnp.moveaxis(table.astype(jnp.float32)[t5_bucket(rel)], -1, 0)


def window_attention(q, k, v, sinks, rel_bias):
    Bn, S = q.shape[:2]
    nb = S // BLOCK
    qb = q.reshape(Bn, nb, BLOCK, A_KV_HEADS, A_GROUP, HEAD_DIM)

    def bands(t):
        tp = jnp.pad(t, ((0, 0), (BLOCK, BLOCK), (0, 0), (0, 0))).reshape(Bn, nb + 2, BLOCK, A_KV_HEADS, HEAD_DIM)
        return jnp.concatenate([tp[:, :-2], tp[:, 1:-1], tp[:, 2:]], axis=2)

    kb, vb = bands(k), bands(v)
    s = jnp.einsum('bnqhgd,bnkhd->bnhgqk', qb, kb).astype(jnp.float32) * HEAD_DIM ** -0.5
    q_loc = jnp.arange(BLOCK)[:, None]
    k_loc = jnp.arange(3 * BLOCK)[None, :] - BLOCK
    rel = k_loc - q_loc
    bias = t5_bias(rel_bias, rel).reshape(A_KV_HEADS, A_GROUP, BLOCK, 3 * BLOCK)
    k_pos = jnp.arange(nb)[:, None] * BLOCK + k_loc
    valid = (jnp.abs(rel) <= WINDOW)[None] & ((k_pos >= 0) & (k_pos < S))[:, None, :]
    s = jnp.where(valid[None, :, None, None], s + bias, -jnp.inf)
    sink = sinks.astype(jnp.float32).reshape(A_KV_HEADS, A_GROUP)[None, None, :, :, None, None]
    m = jnp.maximum(jnp.max(s, axis=-1, keepdims=True), sink)
    p = jnp.exp(s - m)
    p = (p / (jnp.sum(p, axis=-1, keepdims=True) + jnp.exp(sink - m))).astype(v.dtype)
    o = jnp.einsum('bnhgqk,bnkhd->bnqhgd', p, vb)
    return o.reshape(Bn, S, A_Q)


def short_conv(x, w):
    C = x.shape[-1]
    y = lax.conv_general_dilated(x, w[:, None, :].astype(x.dtype), window_strides=(1,),
                                 padding=[(CONV_WIDTH // 2, CONV_WIDTH // 2)],
                                 dimension_numbers=('NWC', 'WIO', 'NWC'), feature_group_count=C)
    return jax.nn.silu(y)


def gated_delta_chunked(q, k, v, g, beta):
    Bn, H, S, dk = k.shape
    dv = v.shape[-1]
    n = S // CHUNK
    q = q.reshape(Bn, H, n, CHUNK, dk)
    k = k.reshape(Bn, H, n, CHUNK, dk)
    v = v.reshape(Bn, H, n, CHUNK, dv)
    beta = beta.reshape(Bn, H, n, CHUNK, 1)
    gc = jnp.cumsum(g.reshape(Bn, H, n, CHUNK), axis=-1)
    causal = jnp.tril(jnp.ones((CHUNK, CHUNK), bool))
    strict = jnp.tril(jnp.ones((CHUNK, CHUNK), bool), -1)
    decay = jnp.exp(jnp.where(causal, gc[..., :, None] - gc[..., None, :], -jnp.inf))
    k_beta = k * beta
    m = jnp.where(strict, jnp.einsum('bhncd,bhnjd->bhncj', k_beta, k) * decay, 0.0)
    lhs = m + jnp.eye(CHUNK, dtype=m.dtype)
    rhs = jnp.concatenate([v * beta, k_beta * jnp.exp(gc)[..., None]], axis=-1)
    sol = lax.linalg.triangular_solve(lhs, rhs, left_side=True, lower=True, unit_diagonal=True)
    u, w = sol[..., :dv], sol[..., dv:]
    a_intra = jnp.einsum('bhncd,bhnjd->bhncj', q, k) * decay
    q_dec = q * jnp.exp(gc)[..., None]
    k_tail = k * jnp.exp(gc[..., -1:] - gc)[..., None]
    c_dec = jnp.exp(gc[..., -1])

    def step(state, xs):
        qd, kt, uc, wc, ac, cd = xs
        v_new = uc - jnp.einsum('bhcd,bhde->bhce', wc, state)
        o = jnp.einsum('bhcd,bhde->bhce', qd, state) + jnp.einsum('bhcj,bhje->bhce', ac, v_new)
        state = state * cd[..., None, None] + jnp.einsum('bhcd,bhce->bhde', kt, v_new)
        return state, o

    xs = tuple(jnp.moveaxis(t, 2, 0) for t in (q_dec, k_tail, u, w, a_intra, c_dec))
    state0 = jnp.zeros((Bn, H, dk, dv), jnp.float32)
    _, o = lax.scan(step, state0, xs)
    return jnp.moveaxis(o, 0, 2).reshape(Bn, H, S, dv)


def even_mixer(h, w_in, sinks, conv_w, a_log, dt_bias, o_norm, w_out, rel_bias):
    Bn, S, _ = h.shape
    z = jnp.dot(h, w_in)
    o0 = A_Q
    o1 = o0 + A_KV
    o2 = o1 + A_KV
    o3 = o2 + B_CONV
    o4 = o3 + B_V
    qa = z[..., :o0].reshape(Bn, S, A_HEADS, HEAD_DIM)
    ka = z[..., o0:o1].reshape(Bn, S, A_KV_HEADS, HEAD_DIM)
    va = z[..., o1:o2].reshape(Bn, S, A_KV_HEADS, HEAD_DIM)
    qkv = short_conv(z[..., o2:o3], conv_w)
    zb = z[..., o3:o4].reshape(Bn, S, B_HEADS, B_DV)
    gl = z[..., o4:].astype(jnp.float32).reshape(Bn, S, 4, B_HEADS)
    out_a = window_attention(qa, ka, va, sinks, rel_bias)
    qb = l2norm(qkv[..., :B_QK].reshape(Bn, S, B_HEADS, B_DK)) * B_DK ** -0.5
    kb = l2norm(qkv[..., B_QK:2 * B_QK].reshape(Bn, S, B_HEADS, B_DK))
    vb = qkv[..., 2 * B_QK:].reshape(Bn, S, B_HEADS, B_DV).astype(jnp.float32)
    beta = jax.nn.sigmoid(gl[:, :, 0:2])
    g = -jnp.exp(a_log.astype(jnp.float32)) * jax.nn.softplus(gl[:, :, 2:4] + dt_bias.astype(jnp.float32))
    q_, k_, v_ = jnp.swapaxes(qb, 1, 2), jnp.swapaxes(kb, 1, 2), jnp.swapaxes(vb, 1, 2)
    g_ = jnp.transpose(g, (2, 0, 3, 1))
    beta_ = jnp.transpose(beta, (2, 0, 3, 1))
    fwd = gated_delta_chunked(q_, k_, v_, g_[0], beta_[0])
    bwd = jnp.flip(gated_delta_chunked(jnp.flip(q_, 2), jnp.flip(k_, 2), jnp.flip(v_, 2),
                                       jnp.flip(g_[1], -1), jnp.flip(beta_[1], -1)), 2)
    ob = jnp.swapaxes(fwd + bwd, 1, 2)
    ob = rmsnorm(ob, o_norm) * jax.nn.silu(zb.astype(jnp.float32))
    out_b = ob.reshape(Bn, S, B_V).astype(h.dtype)
    return jnp.dot(jnp.concatenate([out_a, out_b], axis=-1), w_out)


def diff_attention(q, k, v, lam, rel_bias):
    Bn, S = q.shape[:2]
    nb = S // BLOCK
    q = q * (C_QK_DIM ** -0.5)
    q_blocks = jnp.moveaxis(q.reshape(Bn, nb, BLOCK, C_HEADS, 2, C_QK_DIM), 1, 0)
    k_pos = jnp.arange(S)

    def one_block(args):
        qblk, i = args
        rel = k_pos[None, :] - (i * BLOCK + jnp.arange(BLOCK))[:, None]
        s = jnp.einsum('bqhtd,bkhtd->bthqk', qblk, k).astype(jnp.float32) + t5_bias(rel_bias, rel)
        e = jnp.exp(s - jnp.max(s, axis=-1, keepdims=True))
        r = 1.0 / jnp.sum(e, axis=-1, keepdims=True)
        w = (e[:, 0] * r[:, 0] - e[:, 1] * (lam * r[:, 1])).astype(v.dtype)
        return jnp.einsum('bhqk,bkhe->bqhe', w, v)

    o = lax.map(one_block, (q_blocks, jnp.arange(nb)))
    return jnp.moveaxis(o, 0, 1).reshape(Bn, S, C_HEADS, C_V_DIM)


def odd_mixer(h, w_in, lam_p, subln, w_out, rel_bias, lam_init):
    Bn, S, _ = h.shape
    z = jnp.dot(h, w_in)
    q = z[..., :C_QK].reshape(Bn, S, C_HEADS, 2, C_QK_DIM)
    k = z[..., C_QK:2 * C_QK].reshape(Bn, S, C_HEADS, 2, C_QK_DIM)
    v = z[..., 2 * C_QK:].reshape(Bn, S, C_HEADS, C_V_DIM)
    lp = lam_p.astype(jnp.float32)
    lam = jnp.exp(jnp.sum(lp[0] * lp[1])) - jnp.exp(jnp.sum(lp[2] * lp[3])) + lam_init
    o = diff_attention(q, k, v, lam, rel_bias)
    o = rmsnorm(o, subln) * (1.0 - lam_init)
    return jnp.dot(o.reshape(Bn, S, ODD_MIX), w_out)


def memory_attention(x, mem, g_kv, wq, wkv, wo):
    Bn, S, _ = x.shape
    m = rmsnorm(mem, g_kv)
    q = jnp.dot(x, wq).reshape(Bn, S, X_HEADS, X_HEAD_DIM)
    kv = jnp.dot(m, wkv).reshape(Bn, m.shape[1], 2, X_HEADS, X_HEAD_DIM)
    s = jnp.einsum('bqhd,bkhd->bhqk', q, kv[:, :, 0]).astype(jnp.float32) * X_HEAD_DIM ** -0.5
    p = jax.nn.softmax(s, axis=-1).astype(x.dtype)
    o = jnp.einsum('bhqk,bkhd->bqhd', p, kv[:, :, 1])
    return jnp.dot(o.reshape(Bn, S, X_WIDTH), wo)


def routed_moe(x, w_router, r_bias, w_gate, w_up, w_down, s_gate, s_up, s_down):
    shp = x.shape
    t = x.reshape(-1, shp[-1])
    scores = jax.nn.sigmoid(jnp.dot(t, w_router).astype(jnp.float32))
    biased = scores + r_bias.astype(jnp.float32)
    per_group = N_EXPERTS // N_GROUPS
    group_score = jnp.sum(lax.top_k(biased.reshape(-1, N_GROUPS, per_group), 2)[0], axis=-1)
    _, g_idx = lax.top_k(group_score, TOPK_GROUPS)
    g_mask = jnp.sum(jax.nn.one_hot(g_idx, N_GROUPS, dtype=jnp.float32), axis=1) > 0
    biased = jnp.where(jnp.repeat(g_mask, per_group, axis=1), biased, -jnp.inf)
    _, e_idx = lax.top_k(biased, TOP_K)
    w = jnp.take_along_axis(scores, e_idx, axis=-1)
    w = w / jnp.sum(w, axis=-1, keepdims=True) * ROUTED_SCALE
    gates = jnp.sum(jax.nn.one_hot(e_idx, N_EXPERTS, dtype=jnp.float32) * w[..., None], axis=1)
    shared = jnp.dot(jax.nn.silu(jnp.dot(t, s_gate)) * jnp.dot(t, s_up), s_down)
    hg = jnp.einsum('td,edf->tef', t, w_gate)
    hu = jnp.einsum('td,edf->tef', t, w_up)
    hdn = jax.nn.silu(hg) * hu * gates[:, :, None].astype(hu.dtype)
    out = shared + jnp.einsum('tef,efd->td', hdn, w_down)
    return out.reshape(shp)


def setup_inputs(seed: int = 0) -> dict:
    key = jax.random.key(seed)
    ks = iter(jax.random.split(key, 48))
    f32 = jnp.float32

    def nrm(shape, scale):
        return jax.random.normal(next(ks), shape, f32) * scale

    def gain(shape):
        return 1.0 + nrm(shape, 0.01)

    dt = jnp.exp(jax.random.uniform(next(ks), (N_EVEN, 2, B_HEADS), f32, math.log(1e-3), math.log(1e-1)))
    a_log = jnp.log(jax.random.uniform(next(ks), (N_EVEN, 2, B_HEADS), f32, 1.0, 16.0))
    return {
        'x_prompt': nrm((BATCH, SEQ, D_MODEL), 1.0),
        'x_sample': nrm((DEC_BATCH, DEC_SEQ, D_MODEL), 1.0),
        'mem_prompt': nrm((BATCH, MEM_TOKENS, D_MODEL), 1.0),
        'mem_sample': nrm((DEC_BATCH, MEM_TOKENS, D_MODEL), 1.0),
        'rel_bias': nrm((N_BUCKETS, BIAS_HEADS), 0.5),
        'norm_mix': gain((DEPTH, D_MODEL)),
        'norm_mem': gain((DEPTH, D_MODEL)),
        'norm_memkv': gain((DEPTH, D_MODEL)),
        'norm_ffn': gain((DEPTH, D_MODEL)),
        'norm_final': gain((D_MODEL,)),
        'ev_w_in': nrm((N_EVEN, D_MODEL, EVEN_IN), D_MODEL ** -0.5),
        'ev_sinks': nrm((N_EVEN, A_HEADS), 0.5),
        'ev_conv': nrm((N_EVEN, CONV_WIDTH, B_CONV), CONV_WIDTH ** -0.5),
        'ev_a_log': a_log,
        'ev_dt_bias': dt + jnp.log(-jnp.expm1(-dt)),
        'ev_onorm': gain((N_EVEN, B_DV)),
        'ev_w_out': nrm((N_EVEN, EVEN_MIX, D_MODEL), EVEN_MIX ** -0.5),
        'od_w_in': nrm((N_ODD, D_MODEL, ODD_IN), D_MODEL ** -0.5),
        'od_lambda': nrm((N_ODD, 4, C_QK_DIM), 0.1),
        'od_subln': gain((N_ODD, C_V_DIM)),
        'od_w_out': nrm((N_ODD, ODD_MIX, D_MODEL), ODD_MIX ** -0.5),
        'mx_wq': nrm((DEPTH, D_MODEL, X_WIDTH), D_MODEL ** -0.5),
        'mx_wkv': nrm((DEPTH, D_MODEL, 2 * X_WIDTH), D_MODEL ** -0.5),
        'mx_wo': nrm((DEPTH, X_WIDTH, D_MODEL), X_WIDTH ** -0.5),
        'moe_router': nrm((DEPTH, D_MODEL, N_EXPERTS), D_MODEL ** -0.5),
        'moe_bias': nrm((DEPTH, N_EXPERTS), 0.01),
        'moe_w_gate': nrm((DEPTH, N_EXPERTS, D_MODEL, EXPERT_FF), D_MODEL ** -0.5),
        'moe_w_up': nrm((DEPTH, N_EXPERTS, D_MODEL, EXPERT_FF), D_MODEL ** -0.5),
        'moe_w_down': nrm((DEPTH, N_EXPERTS, EXPERT_FF, D_MODEL), (TOP_K * EXPERT_FF) ** -0.5),
        'sh_w_gate': nrm((DEPTH, D_MODEL, SHARED_FF), D_MODEL ** -0.5),
        'sh_w_up': nrm((DEPTH, D_MODEL, SHARED_FF), D_MODEL ** -0.5),
        'sh_w_down': nrm((DEPTH, SHARED_FF, D_MODEL), SHARED_FF ** -0.5),
    }


def reference(x_prompt, x_sample, mem_prompt, mem_sample, rel_bias, norm_mix, norm_mem, norm_memkv,
              norm_ffn, norm_final, ev_w_in, ev_sinks, ev_conv, ev_a_log, ev_dt_bias, ev_onorm, ev_w_out,
              od_w_in, od_lambda, od_subln, od_w_out, mx_wq, mx_wkv, mx_wo, moe_router, moe_bias,
              moe_w_gate, moe_w_up, moe_w_down, sh_w_gate, sh_w_up, sh_w_down):
    def run(x, mem):
        for l in range(DEPTH):
            h = rmsnorm(x, norm_mix[l])
            i = l // 2
            if l % 2 == 0:
                x = x + even_mixer(h, ev_w_in[i], ev_sinks[i], ev_conv[i], ev_a_log[i], ev_dt_bias[i],
                                   ev_onorm[i], ev_w_out[i], rel_bias)
            else:
                lam_init = 0.8 - 0.6 * math.exp(-0.3 * l)
                x = x + odd_mixer(h, od_w_in[i], od_lambda[i], od_subln[i], od_w_out[i], rel_bias, lam_init)
            x = x + memory_attention(rmsnorm(x, norm_mem[l]), mem, norm_memkv[l], mx_wq[l], mx_wkv[l], mx_wo[l])
            x = x + routed_moe(rmsnorm(x, norm_ffn[l]), moe_router[l], moe_bias[l], moe_w_gate[l],
                               moe_w_up[l], moe_w_down[l], sh_w_gate[l], sh_w_up[l], sh_w_down[l])
        return rmsnorm(x, norm_final)

    y_prompt = run(x_prompt, mem_prompt)
    y_sample = run(x_sample, mem_sample)
    return (y_prompt, y_sample)
```

```python
import functools
import math

import jax
import jax.numpy as jnp
from jax import lax
from jax.experimental import pallas as pl
from jax.experimental.pallas import tpu as pltpu

F32 = jnp.float32
BF16 = jnp.bfloat16

D_MODEL = 4096
DEPTH = 2
HEAD_DIM = 128
BLOCK = 128
WINDOW = 128
A_HEADS = 16
A_KV_HEADS = 4
A_GROUP = A_HEADS // A_KV_HEADS
B_HEADS = 16
B_DK = 128
B_DV = 128
CONV_WIDTH = 5
CHUNK = 64
C_HEADS = 16
C_QK_DIM = 128
C_V_DIM = 2 * C_QK_DIM
N_BUCKETS = 32
MAX_DISTANCE = 128
MEM_TOKENS = 256
X_HEADS = 4
X_HEAD_DIM = 128
N_EXPERTS = 64
TOP_K = 8
N_GROUPS = 8
TOPK_GROUPS = 4
EXPERT_FF = 128
SHARED_FF = 512
ROUTED_SCALE = 2.5
EPS = 1e-6

A_Q = A_HEADS * HEAD_DIM
A_KV = A_KV_HEADS * HEAD_DIM
B_QK = B_HEADS * B_DK
B_V = B_HEADS * B_DV
B_CONV = 2 * B_QK + B_V
EVEN_MAIN = A_Q + 2 * A_KV + B_CONV + B_V
N_GATE = 4 * B_HEADS
C_QK = 2 * C_HEADS * C_QK_DIM
X_WIDTH = X_HEADS * X_HEAD_DIM

LANE = 128
VMEM_LIMIT = 56 * 1024 * 1024
HI = lax.Precision.HIGHEST

DELTA_HB = 4
DELTA_HG = B_HEADS // DELTA_HB
N_PAIRS = (N_EXPERTS + SHARED_FF // EXPERT_FF) // 2


def _cp(*sem):
    return pltpu.CompilerParams(dimension_semantics=sem, vmem_limit_bytes=VMEM_LIMIT)


def _dot(a, b, prec=None):
    return lax.dot_general(a, b, (((1,), (0,)), ((), ())), precision=prec,
                           preferred_element_type=F32)


def _dot_nt(a, b, prec=None):
    return lax.dot_general(a, b, (((1,), (1,)), ((), ())), precision=prec,
                           preferred_element_type=F32)


def _sigmoid(x):
    return 1.0 / (1.0 + jnp.exp(-x))


def _silu(x):
    return x * _sigmoid(x)


def _softplus(x):
    return jnp.maximum(x, 0.0) + jnp.log(1.0 + jnp.exp(-jnp.abs(x)))


def _rms_rows(x, g):
    ms = jnp.mean(x * x, axis=-1, keepdims=True)
    return x * lax.rsqrt(ms + EPS) * g


def _norm_mm_kernel(x_ref, g_ref, w_ref, *rest, tm, has_tail):
    if has_tail:
        wt_ref, o_ref, ot_ref, h_ref = rest
    else:
        o_ref, h_ref = rest

    @pl.when(pl.program_id(1) == 0)
    def _():
        def body(r, c):
            sl = pl.ds(pl.multiple_of(r * LANE, LANE), LANE)
            h_ref[sl, :] = _rms_rows(x_ref[sl, :], g_ref[...]).astype(BF16)
            return c
        lax.fori_loop(0, tm // LANE, body, 0)
        if has_tail:
            ot_ref[...] = _dot(h_ref[...], wt_ref[...])

    o_ref[...] = _dot(h_ref[...], w_ref[...]).astype(o_ref.dtype)


def norm_mm(x, g, w, w_tail=None, *, tm=512, tn=512, out_dtype=BF16):
    T, D = x.shape
    N = w.shape[1]
    tm = min(tm, T)
    assert T % tm == 0 and N % tn == 0 and tm % LANE == 0
    has_tail = w_tail is not None
    in_specs = [
        pl.BlockSpec((tm, D), lambda i, j: (i, 0)),
        pl.BlockSpec((1, D), lambda i, j: (0, 0)),
        pl.BlockSpec((D, tn), lambda i, j: (0, j)),
    ]
    args = [x, g.reshape(1, D).astype(F32), w]
    out_shape = [jax.ShapeDtypeStruct((T, N), out_dtype)]
    out_specs = [pl.BlockSpec((tm, tn), lambda i, j: (i, j))]
    if has_tail:
        nt = w_tail.shape[1]
        in_specs.append(pl.BlockSpec((D, nt), lambda i, j: (0, 0)))
        args.append(w_tail)
        out_shape.append(jax.ShapeDtypeStruct((T, nt), F32))
        out_specs.append(pl.BlockSpec((tm, nt), lambda i, j: (i, 0)))
    outs = pl.pallas_call(
        functools.partial(_norm_mm_kernel, tm=tm, has_tail=has_tail),
        grid=(T // tm, N // tn),
        in_specs=in_specs, out_specs=out_specs, out_shape=out_shape,
        scratch_shapes=[pltpu.VMEM((tm, D), BF16)],
        compiler_params=_cp("parallel", "arbitrary"),
        name="norm_mm",
    )(*args)
    return outs if has_tail else outs[0]


def _mm_res_kernel(*refs, n_a):
    res_ref = refs[0]
    a_refs = refs[1:1 + n_a]
    w_refs = refs[1 + n_a:1 + 2 * n_a]
    o_ref = refs[-1]
    acc = res_ref[...]
    for a, w in zip(a_refs, w_refs):
        acc = acc + _dot(a[...], w[...])
    o_ref[...] = acc


def mm_res(res, a_list, w_list, *, tm=1024, tn=512):
    T, N = res.shape
    tm = min(tm, T)
    assert T % tm == 0 and N % tn == 0
    n_a = len(a_list)
    in_specs = [pl.BlockSpec((tm, tn), lambda i, j: (i, j))]
    for a in a_list:
        in_specs.append(pl.BlockSpec((tm, a.shape[1]), lambda i, j: (i, 0)))
    for w in w_list:
        in_specs.append(pl.BlockSpec((w.shape[0], tn), lambda i, j: (0, j)))
    return pl.pallas_call(
        functools.partial(_mm_res_kernel, n_a=n_a),
        grid=(T // tm, N // tn),
        in_specs=in_specs,
        out_specs=pl.BlockSpec((tm, tn), lambda i, j: (i, j)),
        out_shape=jax.ShapeDtypeStruct((T, N), F32),
        compiler_params=_cp("parallel", "arbitrary"),
        name="mm_res",
    )(res, *a_list, *w_list)


def _t5_bucket(rel):
    nb = N_BUCKETS // 2
    max_exact = nb // 2
    ret = jnp.where(rel > 0, nb, 0)
    n = jnp.abs(rel)
    n_f = jnp.maximum(n, max_exact).astype(F32)
    large = max_exact + (jnp.log(n_f / max_exact) / math.log(MAX_DISTANCE / max_exact)
                         * (nb - max_exact)).astype(jnp.int32)
    large = jnp.minimum(large, nb - 1)
    return ret + jnp.where(n < max_exact, n, large)


def _bias_by_rel(rel_bias):
    rel = jnp.arange(-MAX_DISTANCE, MAX_DISTANCE + 1, dtype=jnp.int32)
    return rel_bias.astype(F32)[_t5_bucket(rel)]


def _win_attn_kernel(q_ref, kp_ref, kc_ref, kn_ref, vp_ref, vc_ref, vn_ref, bias_ref, sink_ref, o_ref):
    i = pl.program_id(1)
    nb = pl.num_programs(1)
    col = lax.broadcasted_iota(jnp.int32, (1, 3 * BLOCK), 1)
    invalid = ((col < BLOCK) & (i == 0)) | ((col >= 2 * BLOCK) & (i == nb - 1))
    scale = HEAD_DIM ** -0.5
    for hk in range(A_KV_HEADS):
        ks = slice(hk * HEAD_DIM, (hk + 1) * HEAD_DIM)
        qh = jnp.concatenate(
            [q_ref[:, (hk * A_GROUP + g) * HEAD_DIM:(hk * A_GROUP + g + 1) * HEAD_DIM]
             for g in range(A_GROUP)], axis=0)
        kb = jnp.concatenate([kp_ref[:, ks], kc_ref[:, ks], kn_ref[:, ks]], axis=0)
        vb = jnp.concatenate([vp_ref[:, ks], vc_ref[:, ks], vn_ref[:, ks]], axis=0)
        s = _dot_nt(qh, kb) * scale + bias_ref[hk]
        s = jnp.where(invalid, -jnp.inf, s)
        sink = sink_ref[hk]
        m = jnp.maximum(jnp.max(s, axis=-1, keepdims=True), sink)
        p = jnp.exp(s - m)
        den = jnp.sum(p, axis=-1, keepdims=True) + jnp.exp(sink - m)
        p = (p / den).astype(BF16)
        o = _dot(p, vb)
        for g in range(A_GROUP):
            h = hk * A_GROUP + g
            o_ref[:, h * HEAD_DIM:(h + 1) * HEAD_DIM] = o[g * BLOCK:(g + 1) * BLOCK].astype(o_ref.dtype)


def window_attention(z, sinks, tab):
    B, S, _ = z.shape
    nb = S // BLOCK
    q_loc = jnp.arange(BLOCK)[:, None]
    k_loc = jnp.arange(3 * BLOCK)[None, :] - BLOCK
    rel = k_loc - q_loc
    bias = jnp.where((jnp.abs(rel) <= WINDOW)[..., None],
                     tab[jnp.clip(rel, -MAX_DISTANCE, MAX_DISTANCE) + MAX_DISTANCE], -jnp.inf)
    bias = jnp.moveaxis(bias, -1, 0).reshape(A_KV_HEADS, A_GROUP * BLOCK, 3 * BLOCK)
    sink = jnp.broadcast_to(sinks.astype(F32).reshape(A_KV_HEADS, A_GROUP, 1, 1),
                            (A_KV_HEADS, A_GROUP, BLOCK, 1)).reshape(A_KV_HEADS, A_GROUP * BLOCK, 1)
    kcol = A_Q // A_KV
    vcol = (A_Q + A_KV) // A_KV
    prev = lambda b, i: (b, jnp.maximum(i - 1, 0))
    nxt = lambda b, i: (b, jnp.minimum(i + 1, nb - 1))
    return pl.pallas_call(
        _win_attn_kernel,
        grid=(B, nb),
        in_specs=[
            pl.BlockSpec((None, BLOCK, A_Q), lambda b, i: (b, i, 0)),
            pl.BlockSpec((None, BLOCK, A_KV), lambda b, i: (*prev(b, i), kcol)),
            pl.BlockSpec((None, BLOCK, A_KV), lambda b, i: (b, i, kcol)),
            pl.BlockSpec((None, BLOCK, A_KV), lambda b, i: (*nxt(b, i), kcol)),
            pl.BlockSpec((None, BLOCK, A_KV), lambda b, i: (*prev(b, i), vcol)),
            pl.BlockSpec((None, BLOCK, A_KV), lambda b, i: (b, i, vcol)),
            pl.BlockSpec((None, BLOCK, A_KV), lambda b, i: (*nxt(b, i), vcol)),
            pl.BlockSpec((A_KV_HEADS, A_GROUP * BLOCK, 3 * BLOCK), lambda b, i: (0, 0, 0)),
            pl.BlockSpec((A_KV_HEADS, A_GROUP * BLOCK, 1), lambda b, i: (0, 0, 0)),
        ],
        out_specs=pl.BlockSpec((None, BLOCK, A_Q), lambda b, i: (b, i, 0)),
        out_shape=jax.ShapeDtypeStruct((B, S, A_Q), BF16),
        compiler_params=_cp("parallel", "arbitrary"),
        name="window_attn",
    )(z, z, z, z, z, z, z, bias, sink)


CONV_TS = 256
CONV_TC = 512
CONV_HALO = 16


def _conv_kernel(prev_ref, cur_ref, next_ref, w_ref, o_ref, ext_ref):
    i = pl.program_id(1)
    j = pl.program_id(2)
    ns = pl.num_programs(1)
    ts = cur_ref.shape[0]
    pv = jnp.where(i == 0, 0.0, prev_ref[...].astype(F32))
    nx = jnp.where(i == ns - 1, 0.0, next_ref[...].astype(F32))
    ext_ref[0:CONV_HALO, :] = pv
    ext_ref[CONV_HALO:CONV_HALO + ts, :] = cur_ref[...].astype(F32)
    ext_ref[CONV_HALO + ts:, :] = nx
    half = CONV_WIDTH // 2
    acc = None
    for t in range(CONV_WIDTH):
        term = w_ref[t:t + 1, :] * ext_ref[pl.ds(CONV_HALO - half + t, ts), :]
        acc = term if acc is None else acc + term
    y = _silu(acc)
    heads_per_step = CONV_TC // B_DK
    q_steps = B_QK // CONV_TC

    def l2(scale):
        for hh in range(heads_per_step):
            seg = y[:, hh * B_DK:(hh + 1) * B_DK]
            r = lax.rsqrt(jnp.sum(seg * seg, axis=-1, keepdims=True) + EPS)
            o_ref[:, hh * B_DK:(hh + 1) * B_DK] = seg * (r * scale)

    @pl.when(j < q_steps)
    def _():
        l2(B_DK ** -0.5)

    @pl.when((j >= q_steps) & (j < 2 * q_steps))
    def _():
        l2(1.0)

    @pl.when(j >= 2 * q_steps)
    def _():
        o_ref[...] = y


def conv_qkv(z, conv_w):
    B, S, _ = z.shape
    ts = min(CONV_TS, S)
    c0 = (A_Q + 2 * A_KV) // CONV_TC
    hb = ts // CONV_HALO
    nh = S // CONV_HALO
    return pl.pallas_call(
        _conv_kernel,
        grid=(B, S // ts, B_CONV // CONV_TC),
        in_specs=[
            pl.BlockSpec((None, CONV_HALO, CONV_TC), lambda b, i, j: (b, jnp.maximum(i * hb - 1, 0), c0 + j)),
            pl.BlockSpec((None, ts, CONV_TC), lambda b, i, j: (b, i, c0 + j)),
            pl.BlockSpec((None, CONV_HALO, CONV_TC), lambda b, i, j: (b, jnp.minimum((i + 1) * hb, nh - 1), c0 + j)),
            pl.BlockSpec((CONV_WIDTH, CONV_TC), lambda b, i, j: (0, j)),
        ],
        out_specs=pl.BlockSpec((None, ts, CONV_TC), lambda b, i, j: (b, i, j)),
        out_shape=jax.ShapeDtypeStruct((B, S, B_CONV), F32),
        scratch_shapes=[pltpu.VMEM((ts + 2 * CONV_HALO, CONV_TC), F32)],
        compiler_params=_cp("parallel", "parallel", "arbitrary"),
        name="conv_qkv",
    )(z, z, z, conv_w.astype(F32))


def _delta_chain(qh, kh, vh, beta_col, gc_col, gc_row, g_end, state, causal, strict, level_masks, eye):
    decay = jnp.exp(jnp.where(causal, gc_col - gc_row, -jnp.inf))
    kb = kh * beta_col
    m = jnp.where(strict, _dot_nt(kb, kh, HI) * decay, 0.0)
    inv = eye - jnp.where(level_masks[0], m, 0.0)
    for lm in level_masks[1:]:
        c = jnp.where(lm, m, 0.0)
        inv = inv - _dot(_dot(inv, c, HI), inv, HI)
    eg = jnp.exp(gc_col)
    rhs = jnp.concatenate([vh * beta_col, kb * eg], axis=1)
    sol = _dot(inv, rhs, HI)
    u = sol[:, :B_DV]
    w = sol[:, B_DV:]
    a = _dot_nt(qh, kh, HI) * decay
    q_dec = qh * eg
    k_tail = kh * jnp.exp(g_end - gc_col)
    v_new = u - _dot(w, state, HI)
    o = _dot(q_dec, state, HI) + _dot(a, v_new, HI)
    state = state * jnp.exp(g_end) + lax.dot_general(
        k_tail, v_new, (((0,), (0,)), ((), ())), precision=HI, preferred_element_type=F32)
    return o, state


def _delta_kernel(qf_ref, kf_ref, vf_ref, glf_ref, gtf_ref, qb_ref, kb_ref, vb_ref, glb_ref, gtb_ref,
                  arow_ref, drow_ref, acol_ref, dcol_ref, of_ref, ob_ref, st_ref):
    c = pl.program_id(2)

    @pl.when(c == 0)
    def _():
        st_ref[...] = jnp.zeros_like(st_ref)

    ri = lax.broadcasted_iota(jnp.int32, (CHUNK, CHUNK), 0)
    ci = lax.broadcasted_iota(jnp.int32, (CHUNK, CHUNK), 1)
    eye = (ri == ci).astype(F32)
    levels = int(math.log2(CHUNK))
    level_masks = [((ri >> (l + 1)) == (ci >> (l + 1))) & ((ri >> l) != (ci >> l)) for l in range(levels)]
    lower = (ri >= ci).astype(F32)
    upper = (ri <= ci).astype(F32)
    hb = DELTA_HB

    for d, (q_ref, k_ref, v_ref, gl_ref, gt_ref, o_ref) in enumerate(
            ((qf_ref, kf_ref, vf_ref, glf_ref, gtf_ref, of_ref),
             (qb_ref, kb_ref, vb_ref, glb_ref, gtb_ref, ob_ref))):
        fwd = d == 0
        gl = gl_ref[...]
        gt = gt_ref[...]
        beta_all = _sigmoid(gl)
        g_all = arow_ref[...] * _softplus(gl + drow_ref[...])
        gT_all = acol_ref[...] * _softplus(gt + dcol_ref[...])
        if fwd:
            gc_all = _dot(lower, g_all, HI)
            gcT_all = _dot(gT_all, upper, HI)
            causal, strict = ri >= ci, ri > ci
        else:
            gc_all = _dot(upper, g_all, HI)
            gcT_all = _dot(gT_all, lower, HI)
            causal, strict = ri <= ci, ri < ci
        end = CHUNK - 1 if fwd else 0
        for hh in range(hb):
            bl = d * hb + hh
            al = (2 + d) * hb + hh
            sl = slice(hh * B_DK, (hh + 1) * B_DK)
            gc_col = gc_all[:, al:al + 1]
            o, st = _delta_chain(
                q_ref[:, sl], k_ref[:, sl], v_ref[:, sl], beta_all[:, bl:bl + 1],
                gc_col, gcT_all[al:al + 1, :], gc_col[end:end + 1, :], st_ref[d * hb + hh],
                causal, strict, level_masks, eye)
            o_ref[:, sl] = o
            st_ref[d * hb + hh] = st


def delta_rule(qkv, gl, a_log, dt_bias):
    B, S, _ = qkv.shape
    n = S // CHUNK
    hb, hg = DELTA_HB, DELTA_HG
    gw = 4 * hb
    g4 = gl[..., :N_GATE].reshape(B, S, 4, hg, hb)
    g4 = jnp.transpose(g4, (0, 3, 1, 2, 4)).reshape(B, hg, S, gw)
    g_rows = jnp.pad(g4, ((0, 0), (0, 0), (0, 0), (0, LANE - gw)))
    g_cols = jnp.transpose(g4.reshape(B, hg, n, CHUNK, gw), (0, 1, 2, 4, 3))
    neg_a = -jnp.exp(a_log.astype(F32)).reshape(2, hg, hb)
    dtb = dt_bias.astype(F32).reshape(2, hg, hb)
    zeros = jnp.zeros((2, hg, hb), F32)
    a4 = jnp.transpose(jnp.concatenate([zeros, neg_a], 0), (1, 0, 2)).reshape(hg, gw)
    d4 = jnp.transpose(jnp.concatenate([zeros, dtb], 0), (1, 0, 2)).reshape(hg, gw)
    arow = jnp.pad(a4, ((0, 0), (0, LANE - gw))).reshape(hg, 1, LANE)
    drow = jnp.pad(d4, ((0, 0), (0, LANE - gw))).reshape(hg, 1, LANE)
    acol = a4.reshape(hg, gw, 1)
    dcol = d4.reshape(hg, gw, 1)
    cw = hb * B_DK
    kq, kk, kv = 0, B_QK // cw, 2 * B_QK // cw

    def seq_specs(cidx):
        return [
            pl.BlockSpec((None, CHUNK, cw), lambda b, g, c: (b, cidx(c), kq + g)),
            pl.BlockSpec((None, CHUNK, cw), lambda b, g, c: (b, cidx(c), kk + g)),
            pl.BlockSpec((None, CHUNK, cw), lambda b, g, c: (b, cidx(c), kv + g)),
            pl.BlockSpec((None, None, CHUNK, LANE), lambda b, g, c: (b, g, cidx(c), 0)),
            pl.BlockSpec((None, None, None, gw, CHUNK), lambda b, g, c: (b, g, cidx(c), 0, 0)),
        ]

    fwd_idx = lambda c: c
    bwd_idx = lambda c: n - 1 - c
    par_specs = [
        pl.BlockSpec((None, 1, LANE), lambda b, g, c: (g, 0, 0)),
        pl.BlockSpec((None, 1, LANE), lambda b, g, c: (g, 0, 0)),
        pl.BlockSpec((None, gw, 1), lambda b, g, c: (g, 0, 0)),
        pl.BlockSpec((None, gw, 1), lambda b, g, c: (g, 0, 0)),
    ]
    o_f, o_b = pl.pallas_call(
        _delta_kernel,
        grid=(B, hg, n),
        in_specs=seq_specs(fwd_idx) + seq_specs(bwd_idx) + par_specs,
        out_specs=[
            pl.BlockSpec((None, CHUNK, cw), lambda b, g, c: (b, fwd_idx(c), g)),
            pl.BlockSpec((None, CHUNK, cw), lambda b, g, c: (b, bwd_idx(c), g)),
        ],
        out_shape=[jax.ShapeDtypeStruct((B, S, B_V), F32)] * 2,
        scratch_shapes=[pltpu.VMEM((2 * hb, B_DK, B_DV), F32)],
        compiler_params=_cp("parallel", "parallel", "arbitrary"),
        name="delta_rule",
    )(qkv, qkv, qkv, g_rows, g_cols, qkv, qkv, qkv, g_rows, g_cols, arow, drow, acol, dcol)
    return o_f, o_b


GATE_TC = 1024


def _delta_out_kernel(of_ref, ob_ref, zb_ref, g_ref, o_ref):
    for hh in range(GATE_TC // B_DV):
        sl = slice(hh * B_DV, (hh + 1) * B_DV)
        ob = of_ref[:, sl] + ob_ref[:, sl]
        y = _rms_rows(ob, g_ref[...])
        o_ref[:, sl] = (y * _silu(zb_ref[:, sl].astype(F32))).astype(o_ref.dtype)


def delta_out(o_f, o_b, z2d, onorm, *, tm=512):
    T = o_f.shape[0]
    tm = min(tm, T)
    c0 = (A_Q + 2 * A_KV + B_CONV) // GATE_TC
    return pl.pallas_call(
        _delta_out_kernel,
        grid=(T // tm, B_V // GATE_TC),
        in_specs=[
            pl.BlockSpec((tm, GATE_TC), lambda i, j: (i, j)),
            pl.BlockSpec((tm, GATE_TC), lambda i, j: (i, j)),
            pl.BlockSpec((tm, GATE_TC), lambda i, j: (i, c0 + j)),
            pl.BlockSpec((1, B_DV), lambda i, j: (0, 0)),
        ],
        out_specs=pl.BlockSpec((tm, GATE_TC), lambda i, j: (i, j)),
        out_shape=jax.ShapeDtypeStruct((T, B_V), BF16),
        compiler_params=_cp("parallel", "arbitrary"),
        name="delta_out",
    )(o_f, o_b, z2d, onorm.reshape(1, B_DV).astype(F32))


DIFF_T = 512


def _diff_attn_kernel(lam_ref, q_ref, k_ref, v_ref, b_ref, g_ref, o_ref, m_ref, l_ref, acc_ref, *, out_scale):
    kj = pl.program_id(3)
    nk = pl.num_programs(3)

    @pl.when(kj == 0)
    def _():
        m_ref[...] = jnp.full_like(m_ref, -jnp.inf)
        l_ref[...] = jnp.zeros_like(l_ref)
        acc_ref[...] = jnp.zeros_like(acc_ref)

    bias = b_ref[...]
    v = v_ref[...]
    for t in range(2):
        sl = slice(t * C_QK_DIM, (t + 1) * C_QK_DIM)
        s = _dot_nt(q_ref[:, sl], k_ref[:, sl]) + bias
        m_old = m_ref[t]
        m_new = jnp.maximum(m_old, jnp.max(s, axis=-1, keepdims=True))
        alpha = jnp.exp(m_old - m_new)
        p = jnp.exp(s - m_new)
        l_ref[t] = alpha * l_ref[t] + jnp.sum(p, axis=-1, keepdims=True)
        acc_ref[t] = alpha * acc_ref[t] + _dot(p.astype(BF16), v)
        m_ref[t] = m_new

    @pl.when(kj == nk - 1)
    def _():
        lam = lam_ref[0, 0]
        o = acc_ref[0] * (1.0 / l_ref[0]) - acc_ref[1] * (lam * (1.0 / l_ref[1]))
        o_ref[...] = (_rms_rows(o, g_ref[...]) * out_scale).astype(o_ref.dtype)


def diff_attention(z, lam, tab, subln, lam_init):
    B, S, _ = z.shape
    t = min(DIFF_T, S)
    nq = S // t
    r = jnp.arange(t)[:, None]
    c = jnp.arange(t)[None, :]
    d = jnp.arange(-2, 3)[:, None, None]
    idx = jnp.clip(d * t + c - r, -MAX_DISTANCE, MAX_DISTANCE) + MAX_DISTANCE
    btile = jnp.moveaxis(tab[idx], -1, 0)
    hw = 2 * C_QK_DIM
    return pl.pallas_call(
        functools.partial(_diff_attn_kernel, out_scale=1.0 - lam_init),
        grid=(B, C_HEADS, nq, nq),
        in_specs=[
            pl.BlockSpec(memory_space=pltpu.SMEM),
            pl.BlockSpec((None, t, hw), lambda b, h, i, j: (b, i, h)),
            pl.BlockSpec((None, t, hw), lambda b, h, i, j: (b, j, C_HEADS + h)),
            pl.BlockSpec((None, t, hw), lambda b, h, i, j: (b, j, 2 * C_HEADS + h)),
            pl.BlockSpec((None, None, t, t), lambda b, h, i, j: (h, jnp.clip(j - i, -2, 2) + 2, 0, 0)),
            pl.BlockSpec((1, C_V_DIM), lambda b, h, i, j: (0, 0)),
        ],
        out_specs=pl.BlockSpec((None, t, C_V_DIM), lambda b, h, i, j: (b, i, h)),
        out_shape=jax.ShapeDtypeStruct((B, S, C_HEADS * C_V_DIM), BF16),
        scratch_shapes=[pltpu.VMEM((2, t, 1), F32), pltpu.VMEM((2, t, 1), F32),
                        pltpu.VMEM((2, t, C_V_DIM), F32)],
        compiler_params=_cp("parallel", "parallel", "parallel", "arbitrary"),
        name="diff_attn",
    )(lam.reshape(1, 1).astype(F32), z, z, z, btile, subln.reshape(1, C_V_DIM).astype(F32))


def _mem_attn_kernel(x_ref, g_ref, wq_ref, kv_ref, wo_ref, o_ref):
    x = x_ref[...]
    h = _rms_rows(x, g_ref[...]).astype(BF16)
    q = _dot(h, wq_ref[...]).astype(BF16)
    outs = []
    for hd in range(X_HEADS):
        sl = slice(hd * X_HEAD_DIM, (hd + 1) * X_HEAD_DIM)
        s = _dot_nt(q[:, sl], kv_ref[:, sl]) * (X_HEAD_DIM ** -0.5)
        m = jnp.max(s, axis=-1, keepdims=True)
        e = jnp.exp(s - m)
        p = (e / jnp.sum(e, axis=-1, keepdims=True)).astype(BF16)
        outs.append(_dot(p, kv_ref[:, X_WIDTH + hd * X_HEAD_DIM:X_WIDTH + (hd + 1) * X_HEAD_DIM]))
    o = jnp.concatenate(outs, axis=1).astype(BF16)
    o_ref[...] = x + _dot(o, wo_ref[...])


def memory_attention(x, kv, g, wq, wo, *, tm=256):
    B, S, D = x.shape
    tm = min(tm, S)
    return pl.pallas_call(
        _mem_attn_kernel,
        grid=(B, S // tm),
        in_specs=[
            pl.BlockSpec((None, tm, D), lambda b, i: (b, i, 0)),
            pl.BlockSpec((1, D), lambda b, i: (0, 0)),
            pl.BlockSpec((D, X_WIDTH), lambda b, i: (0, 0)),
            pl.BlockSpec((None, MEM_TOKENS, 2 * X_WIDTH), lambda b, i: (b, 0, 0)),
            pl.BlockSpec((X_WIDTH, D), lambda b, i: (0, 0)),
        ],
        out_specs=pl.BlockSpec((None, tm, D), lambda b, i: (b, i, 0)),
        out_shape=jax.ShapeDtypeStruct((B, S, D), F32),
        compiler_params=_cp("parallel", "arbitrary"),
        name="mem_attn",
    )(x, g.reshape(1, D).astype(F32), wq, kv, wo)


def _first_max_onehot(vals, rows):
    m = jnp.max(vals, axis=0, keepdims=True)
    idx = jnp.min(jnp.where(vals == m, rows, vals.shape[0]), axis=0, keepdims=True)
    return rows == idx


def _router_kernel(x_ref, g_ref, w_ref, b_ref, o_ref, h_ref):
    tm = x_ref.shape[0]

    def body(r, c):
        sl = pl.ds(pl.multiple_of(r * LANE, LANE), LANE)
        h_ref[sl, :] = _rms_rows(x_ref[sl, :], g_ref[...])
        return c
    lax.fori_loop(0, tm // LANE, body, 0)
    logits = _dot(h_ref[...], w_ref[...], HI)
    lt = jnp.transpose(logits)[:N_EXPERTS, :]
    scores = _sigmoid(lt)
    biased = scores + b_ref[...]
    per_group = N_EXPERTS // N_GROUPS
    rows8 = lax.broadcasted_iota(jnp.int32, (per_group, tm), 0)
    gscore = []
    for gi in range(N_GROUPS):
        blk = biased[gi * per_group:(gi + 1) * per_group]
        first = _first_max_onehot(blk, rows8)
        m1 = jnp.max(blk, axis=0, keepdims=True)
        m2 = jnp.max(jnp.where(first, -jnp.inf, blk), axis=0, keepdims=True)
        gscore.append(m1 + m2)
    gscore = jnp.concatenate(gscore, axis=0)
    growi = lax.broadcasted_iota(jnp.int32, (N_GROUPS, tm), 0)
    gsel = jnp.zeros((N_GROUPS, tm), jnp.bool_)
    work = gscore
    for _ in range(TOPK_GROUPS):
        oh = _first_max_onehot(work, growi)
        gsel = gsel | oh
        work = jnp.where(oh, -jnp.inf, work)
    gself = gsel.astype(F32)
    masked = jnp.concatenate(
        [jnp.where(gself[gi:gi + 1] > 0.0, biased[gi * per_group:(gi + 1) * per_group], -jnp.inf)
         for gi in range(N_GROUPS)], axis=0)
    erow = lax.broadcasted_iota(jnp.int32, (N_EXPERTS, tm), 0)
    esel = jnp.zeros((N_EXPERTS, tm), jnp.bool_)
    work = masked
    for _ in range(TOP_K):
        oh = _first_max_onehot(work, erow)
        esel = esel | oh
        work = jnp.where(oh, -jnp.inf, work)
    w = jnp.where(esel, scores, 0.0)
    gates = w / jnp.sum(w, axis=0, keepdims=True) * ROUTED_SCALE
    n_shared = SHARED_FF // EXPERT_FF
    full = jnp.concatenate([gates, jnp.ones((8, tm), F32), jnp.zeros((LANE - N_EXPERTS - 8, tm), F32)], axis=0)
    frow = lax.broadcasted_iota(jnp.int32, (LANE, tm), 0)
    full = jnp.where(frow < N_EXPERTS + n_shared, full, 0.0)
    o_ref[...] = jnp.transpose(full)


def route_tokens(x, g, w_router, r_bias, *, tm=256):
    T, D = x.shape
    tm = min(tm, T)
    w = jnp.pad(w_router.astype(F32), ((0, 0), (0, LANE - N_EXPERTS)))
    return pl.pallas_call(
        _router_kernel,
        grid=(T // tm,),
        in_specs=[
            pl.BlockSpec((tm, D), lambda i: (i, 0)),
            pl.BlockSpec((1, D), lambda i: (0, 0)),
            pl.BlockSpec((D, LANE), lambda i: (0, 0)),
            pl.BlockSpec((N_EXPERTS, 1), lambda i: (0, 0)),
        ],
        out_specs=pl.BlockSpec((tm, LANE), lambda i: (i, 0)),
        out_shape=jax.ShapeDtypeStruct((T, LANE), F32),
        scratch_shapes=[pltpu.VMEM((tm, D), F32)],
        compiler_params=_cp("parallel"),
        name="moe_router",
    )(x, g.reshape(1, D).astype(F32), w, r_bias.astype(F32).reshape(N_EXPERTS, 1))


def _moe_kernel(x_ref, g_ref, gate_ref, wgu_ref, wd_ref, gf_ref, o_ref, h_ref, *, tm, final_norm):
    p = pl.program_id(1)
    npair = pl.num_programs(1)

    @pl.when(p == 0)
    def _():
        def body(r, c):
            sl = pl.ds(pl.multiple_of(r * LANE, LANE), LANE)
            x = x_ref[sl, :]
            h_ref[sl, :] = _rms_rows(x, g_ref[...]).astype(BF16)
            o_ref[sl, :] = x
            return c
        lax.fori_loop(0, tm // LANE, body, 0)

    gu = _dot(h_ref[...], wgu_ref[...])
    gates = gate_ref[...]
    lane = lax.broadcasted_iota(jnp.int32, gates.shape, 1)
    F = EXPERT_FF
    hid = []
    for e in range(2):
        ge = jnp.sum(jnp.where(lane == 2 * p + e, gates, 0.0), axis=1, keepdims=True)
        hid.append(_silu(gu[:, e * F:(e + 1) * F]) * gu[:, (2 + e) * F:(3 + e) * F] * ge)
    hid = jnp.concatenate(hid, axis=1).astype(BF16)
    o_ref[...] += _dot(hid, wd_ref[...])

    if final_norm:
        @pl.when(p == npair - 1)
        def _():
            def body(r, c):
                sl = pl.ds(pl.multiple_of(r * LANE, LANE), LANE)
                o_ref[sl, :] = _rms_rows(o_ref[sl, :], gf_ref[...])
                return c
            lax.fori_loop(0, tm // LANE, body, 0)


def moe_experts(x, g, gates, w_gu, w_d, g_final, *, final_norm, tm=512):
    T, D = x.shape
    tm = min(tm, T)
    return pl.pallas_call(
        functools.partial(_moe_kernel, tm=tm, final_norm=final_norm),
        grid=(T // tm, N_PAIRS),
        in_specs=[
            pl.BlockSpec((tm, D), lambda i, p: (i, 0)),
            pl.BlockSpec((1, D), lambda i, p: (0, 0)),
            pl.BlockSpec((tm, LANE), lambda i, p: (i, 0)),
            pl.BlockSpec((None, D, 4 * EXPERT_FF), lambda i, p: (p, 0, 0)),
            pl.BlockSpec((None, 2 * EXPERT_FF, D), lambda i, p: (p, 0, 0)),
            pl.BlockSpec((1, D), lambda i, p: (0, 0)),
        ],
        out_specs=pl.BlockSpec((tm, D), lambda i, p: (i, 0)),
        out_shape=jax.ShapeDtypeStruct((T, D), F32),
        scratch_shapes=[pltpu.VMEM((tm, D), BF16)],
        compiler_params=_cp("parallel", "arbitrary"),
        name="moe_experts",
    )(x, g.reshape(1, D).astype(F32), gates, w_gu, w_d, g_final.reshape(1, D).astype(F32))


def _moe_weights(w_gate, w_up, w_down, s_gate, s_up, s_down):
    D = w_gate.shape[1]
    ns = SHARED_FF // EXPERT_FF
    sg = jnp.transpose(s_gate.reshape(D, ns, EXPERT_FF), (1, 0, 2))
    su = jnp.transpose(s_up.reshape(D, ns, EXPERT_FF), (1, 0, 2))
    wg = jnp.concatenate([w_gate, sg], axis=0).astype(BF16).reshape(N_PAIRS, 2, D, EXPERT_FF)
    wu = jnp.concatenate([w_up, su], axis=0).astype(BF16).reshape(N_PAIRS, 2, D, EXPERT_FF)
    w_gu = jnp.concatenate([wg[:, 0], wg[:, 1], wu[:, 0], wu[:, 1]], axis=-1)
    w_d = jnp.concatenate([w_down, s_down.reshape(ns, EXPERT_FF, D)], axis=0).astype(BF16)
    return w_gu, w_d.reshape(N_PAIRS, 2 * EXPERT_FF, D)


def kernel(x_prompt, x_sample, mem_prompt, mem_sample, rel_bias, norm_mix, norm_mem, norm_memkv, norm_ffn, norm_final, ev_w_in, ev_sinks, ev_conv, ev_a_log, ev_dt_bias, ev_onorm, ev_w_out, od_w_in, od_lambda, od_subln, od_w_out, mx_wq, mx_wkv, mx_wo, moe_router, moe_bias, moe_w_gate, moe_w_up, moe_w_down, sh_w_gate, sh_w_up, sh_w_down):
    tab = _bias_by_rel(rel_bias)

    layers = []
    for l in range(DEPTH):
        i = l // 2
        lw = {}
        if l % 2 == 0:
            w_in = ev_w_in[i]
            lw["w_in"] = w_in[:, :EVEN_MAIN].astype(BF16)
            lw["w_gate_tail"] = jnp.pad(w_in[:, EVEN_MAIN:], ((0, 0), (0, LANE - N_GATE))).astype(BF16)
            lw["w_out_a"] = ev_w_out[i][:A_Q].astype(BF16)
            lw["w_out_b"] = ev_w_out[i][A_Q:].astype(BF16)
        else:
            w_in = od_w_in[i]
            lw["w_in"] = jnp.concatenate([w_in[:, :C_QK] * (C_QK_DIM ** -0.5), w_in[:, C_QK:]], axis=1).astype(BF16)
            lw["w_out"] = od_w_out[i].astype(BF16)
            lp = od_lambda[i].astype(F32)
            lam_init = 0.8 - 0.6 * math.exp(-0.3 * l)
            lw["lam_init"] = lam_init
            lw["lam"] = jnp.exp(jnp.sum(lp[0] * lp[1])) - jnp.exp(jnp.sum(lp[2] * lp[3])) + lam_init
        lw["wq"] = mx_wq[l].astype(BF16)
        lw["wkv"] = mx_wkv[l].astype(BF16)
        lw["wo"] = mx_wo[l].astype(BF16)
        lw["w_gu"], lw["w_d"] = _moe_weights(moe_w_gate[l], moe_w_up[l], moe_w_down[l],
                                             sh_w_gate[l], sh_w_up[l], sh_w_down[l])
        layers.append(lw)

    def run(x3, mem):
        B, S, D = x3.shape
        T = B * S
        x = x3.reshape(T, D)
        mem2 = mem.reshape(B * MEM_TOKENS, D)
        for l in range(DEPTH):
            lw = layers[l]
            i = l // 2
            if l % 2 == 0:
                z, gl = norm_mm(x, norm_mix[l], lw["w_in"], lw["w_gate_tail"])
                z3 = z.reshape(B, S, EVEN_MAIN)
                out_a = window_attention(z3, ev_sinks[i], tab)
                qkv = conv_qkv(z3, ev_conv[i])
                o_f, o_b = delta_rule(qkv, gl.reshape(B, S, LANE), ev_a_log[i], ev_dt_bias[i])
                out_b = delta_out(o_f.reshape(T, B_V), o_b.reshape(T, B_V), z, ev_onorm[i])
                x = mm_res(x, [out_a.reshape(T, A_Q), out_b], [lw["w_out_a"], lw["w_out_b"]])
            else:
                z = norm_mm(x, norm_mix[l], lw["w_in"])
                o = diff_attention(z.reshape(B, S, 3 * C_QK), lw["lam"], tab, od_subln[i], lw["lam_init"])
                x = mm_res(x, [o.reshape(T, C_HEADS * C_V_DIM)], [lw["w_out"]])
            kv = norm_mm(mem2, norm_memkv[l], lw["wkv"])
            x = memory_attention(x.reshape(B, S, D), kv.reshape(B, MEM_TOKENS, 2 * X_WIDTH),
                                 norm_mem[l], lw["wq"], lw["wo"]).reshape(T, D)
            gates = route_tokens(x, norm_ffn[l], moe_router[l], moe_bias[l])
            x = moe_experts(x, norm_ffn[l], gates, lw["w_gu"], lw["w_d"], norm_final,
                            final_norm=(l == DEPTH - 1))
        return x.reshape(B, S, D)

    return (run(x_prompt, mem_prompt), run(x_sample, mem_sample))
```

```python
import functools
import math

import jax
import jax.numpy as jnp
from jax import lax
from jax.experimental import pallas as pl
from jax.experimental.pallas import tpu as pltpu

F32 = jnp.float32
BF16 = jnp.bfloat16

D_MODEL = 4096
DEPTH = 2
HEAD_DIM = 128
BLOCK = 128
WINDOW = 128
A_HEADS = 16
A_KV_HEADS = 4
A_GROUP = A_HEADS // A_KV_HEADS
B_HEADS = 16
B_DK = 128
B_DV = 128
CONV_WIDTH = 5
CHUNK = 64
C_HEADS = 16
C_QK_DIM = 128
C_V_DIM = 2 * C_QK_DIM
N_BUCKETS = 32
MAX_DISTANCE = 128
MEM_TOKENS = 256
X_HEADS = 4
X_HEAD_DIM = 128
N_EXPERTS = 64
TOP_K = 8
N_GROUPS = 8
TOPK_GROUPS = 4
EXPERT_FF = 128
SHARED_FF = 512
ROUTED_SCALE = 2.5
EPS = 1e-6

A_Q = A_HEADS * HEAD_DIM
A_KV = A_KV_HEADS * HEAD_DIM
B_QK = B_HEADS * B_DK
B_V = B_HEADS * B_DV
B_CONV = 2 * B_QK + B_V
EVEN_MAIN = A_Q + 2 * A_KV + B_CONV + B_V
N_GATE = 4 * B_HEADS
C_QK = 2 * C_HEADS * C_QK_DIM
X_WIDTH = X_HEADS * X_HEAD_DIM

LANE = 128
VMEM_LIMIT = 56 * 1024 * 1024
HI = lax.Precision.HIGHEST
LOG2E = math.log2(math.e)

DELTA_HB = 2
DELTA_HG = B_HEADS // DELTA_HB
DELTA_CH = 256
DELTA_MXU = BF16
DELTA_PREC = None
N_PAIRS = (N_EXPERTS + SHARED_FF // EXPERT_FF) // 2


def _cp(*sem):
    return pltpu.CompilerParams(dimension_semantics=sem, vmem_limit_bytes=VMEM_LIMIT)


def _dot(a, b, prec=None):
    return lax.dot_general(a, b, (((1,), (0,)), ((), ())), precision=prec,
                           preferred_element_type=F32)


def _dot_nt(a, b, prec=None):
    return lax.dot_general(a, b, (((1,), (1,)), ((), ())), precision=prec,
                           preferred_element_type=F32)


def _sigmoid(x):
    return 1.0 / (1.0 + jnp.exp(-x))


def _silu(x):
    return x * _sigmoid(x)


def _softplus(x):
    return jnp.maximum(x, 0.0) + jnp.log(1.0 + jnp.exp(-jnp.abs(x)))


def _rms_rows(x, g):
    ms = jnp.mean(x * x, axis=-1, keepdims=True)
    return x * lax.rsqrt(ms + EPS) * g


def _norm_mm_kernel(x_ref, g_ref, w_ref, *rest, tm, has_tail):
    if has_tail:
        wt_ref, o_ref, ot_ref, h_ref = rest
    else:
        o_ref, h_ref = rest

    @pl.when(pl.program_id(1) == 0)
    def _():
        def body(r, c):
            sl = pl.ds(pl.multiple_of(r * LANE, LANE), LANE)
            h_ref[sl, :] = _rms_rows(x_ref[sl, :], g_ref[...]).astype(BF16)
            return c
        lax.fori_loop(0, tm // LANE, body, 0)
        if has_tail:
            ot_ref[...] = _dot(h_ref[...], wt_ref[...])

    o_ref[...] = _dot(h_ref[...], w_ref[...]).astype(o_ref.dtype)


def norm_mm(x, g, w, w_tail=None, *, tm=512, tn=512, out_dtype=BF16):
    T, D = x.shape
    N = w.shape[1]
    tm = min(tm, T)
    assert T % tm == 0 and N % tn == 0 and tm % LANE == 0
    has_tail = w_tail is not None
    in_specs = [
        pl.BlockSpec((tm, D), lambda i, j: (i, 0)),
        pl.BlockSpec((1, D), lambda i, j: (0, 0)),
        pl.BlockSpec((D, tn), lambda i, j: (0, j)),
    ]
    args = [x, g.reshape(1, D).astype(F32), w]
    out_shape = [jax.ShapeDtypeStruct((T, N), out_dtype)]
    out_specs = [pl.BlockSpec((tm, tn), lambda i, j: (i, j))]
    if has_tail:
        nt = w_tail.shape[1]
        in_specs.append(pl.BlockSpec((D, nt), lambda i, j: (0, 0)))
        args.append(w_tail)
        out_shape.append(jax.ShapeDtypeStruct((T, nt), F32))
        out_specs.append(pl.BlockSpec((tm, nt), lambda i, j: (i, 0)))
    outs = pl.pallas_call(
        functools.partial(_norm_mm_kernel, tm=tm, has_tail=has_tail),
        grid=(T // tm, N // tn),
        in_specs=in_specs, out_specs=out_specs, out_shape=out_shape,
        scratch_shapes=[pltpu.VMEM((tm, D), BF16)],
        compiler_params=_cp("parallel", "arbitrary"),
        name="norm_mm",
    )(*args)
    return outs if has_tail else outs[0]


def _mm_res_kernel(*refs, n_a):
    res_ref = refs[0]
    a_refs = refs[1:1 + n_a]
    w_refs = refs[1 + n_a:1 + 2 * n_a]
    o_ref = refs[-1]
    acc = res_ref[...]
    for a, w in zip(a_refs, w_refs):
        acc = acc + _dot(a[...], w[...])
    o_ref[...] = acc


def mm_res(res, a_list, w_list, *, tm=1024, tn=512):
    T, N = res.shape
    tm = min(tm, T)
    assert T % tm == 0 and N % tn == 0
    n_a = len(a_list)
    in_specs = [pl.BlockSpec((tm, tn), lambda i, j: (i, j))]
    for a in a_list:
        in_specs.append(pl.BlockSpec((tm, a.shape[1]), lambda i, j: (i, 0)))
    for w in w_list:
        in_specs.append(pl.BlockSpec((w.shape[0], tn), lambda i, j: (0, j)))
    return pl.pallas_call(
        functools.partial(_mm_res_kernel, n_a=n_a),
        grid=(T // tm, N // tn),
        in_specs=in_specs,
        out_specs=pl.BlockSpec((tm, tn), lambda i, j: (i, j)),
        out_shape=jax.ShapeDtypeStruct((T, N), F32),
        compiler_params=_cp("parallel", "arbitrary"),
        name="mm_res",
    )(res, *a_list, *w_list)


def _t5_bucket(rel):
    nb = N_BUCKETS // 2
    max_exact = nb // 2
    ret = jnp.where(rel > 0, nb, 0)
    n = jnp.abs(rel)
    n_f = jnp.maximum(n, max_exact).astype(F32)
    large = max_exact + (jnp.log(n_f / max_exact) / math.log(MAX_DISTANCE / max_exact)
                         * (nb - max_exact)).astype(jnp.int32)
    large = jnp.minimum(large, nb - 1)
    return ret + jnp.where(n < max_exact, n, large)


def _bias_by_rel(rel_bias):
    rel = jnp.arange(-MAX_DISTANCE, MAX_DISTANCE + 1, dtype=jnp.int32)
    return rel_bias.astype(F32)[_t5_bucket(rel)]


def _win_attn_kernel(q_ref, kp_ref, kc_ref, kn_ref, vp_ref, vc_ref, vn_ref, bias_ref, sink_ref, o_ref):
    i = pl.program_id(1)
    nb = pl.num_programs(1)
    col = lax.broadcasted_iota(jnp.int32, (1, 3 * BLOCK), 1)
    invalid = ((col < BLOCK) & (i == 0)) | ((col >= 2 * BLOCK) & (i == nb - 1))
    scale = HEAD_DIM ** -0.5
    for hk in range(A_KV_HEADS):
        ks = slice(hk * HEAD_DIM, (hk + 1) * HEAD_DIM)
        qh = jnp.concatenate(
            [q_ref[:, (hk * A_GROUP + g) * HEAD_DIM:(hk * A_GROUP + g + 1) * HEAD_DIM]
             for g in range(A_GROUP)], axis=0)
        kb = jnp.concatenate([kp_ref[:, ks], kc_ref[:, ks], kn_ref[:, ks]], axis=0)
        vb = jnp.concatenate([vp_ref[:, ks], vc_ref[:, ks], vn_ref[:, ks]], axis=0)
        s = _dot_nt(qh, kb) * scale + bias_ref[hk]
        s = jnp.where(invalid, -jnp.inf, s)
        sink = sink_ref[hk]
        m = jnp.maximum(jnp.max(s, axis=-1, keepdims=True), sink)
        p = jnp.exp(s - m)
        den = jnp.sum(p, axis=-1, keepdims=True) + jnp.exp(sink - m)
        p = (p / den).astype(BF16)
        o = _dot(p, vb)
        for g in range(A_GROUP):
            h = hk * A_GROUP + g
            o_ref[:, h * HEAD_DIM:(h + 1) * HEAD_DIM] = o[g * BLOCK:(g + 1) * BLOCK].astype(o_ref.dtype)


def window_attention(z, sinks, tab):
    B, S, _ = z.shape
    nb = S // BLOCK
    q_loc = jnp.arange(BLOCK)[:, None]
    k_loc = jnp.arange(3 * BLOCK)[None, :] - BLOCK
    rel = k_loc - q_loc
    bias = jnp.where((jnp.abs(rel) <= WINDOW)[..., None],
                     tab[jnp.clip(rel, -MAX_DISTANCE, MAX_DISTANCE) + MAX_DISTANCE], -jnp.inf)
    bias = jnp.moveaxis(bias, -1, 0).reshape(A_KV_HEADS, A_GROUP * BLOCK, 3 * BLOCK)
    sink = jnp.broadcast_to(sinks.astype(F32).reshape(A_KV_HEADS, A_GROUP, 1, 1),
                            (A_KV_HEADS, A_GROUP, BLOCK, 1)).reshape(A_KV_HEADS, A_GROUP * BLOCK, 1)
    kcol = A_Q // A_KV
    vcol = (A_Q + A_KV) // A_KV
    prev = lambda b, i: (b, jnp.maximum(i - 1, 0))
    nxt = lambda b, i: (b, jnp.minimum(i + 1, nb - 1))
    return pl.pallas_call(
        _win_attn_kernel,
        grid=(B, nb),
        in_specs=[
            pl.BlockSpec((None, BLOCK, A_Q), lambda b, i: (b, i, 0)),
            pl.BlockSpec((None, BLOCK, A_KV), lambda b, i: (*prev(b, i), kcol)),
            pl.BlockSpec((None, BLOCK, A_KV), lambda b, i: (b, i, kcol)),
            pl.BlockSpec((None, BLOCK, A_KV), lambda b, i: (*nxt(b, i), kcol)),
            pl.BlockSpec((None, BLOCK, A_KV), lambda b, i: (*prev(b, i), vcol)),
            pl.BlockSpec((None, BLOCK, A_KV), lambda b, i: (b, i, vcol)),
            pl.BlockSpec((None, BLOCK, A_KV), lambda b, i: (*nxt(b, i), vcol)),
            pl.BlockSpec((A_KV_HEADS, A_GROUP * BLOCK, 3 * BLOCK), lambda b, i: (0, 0, 0)),
            pl.BlockSpec((A_KV_HEADS, A_GROUP * BLOCK, 1), lambda b, i: (0, 0, 0)),
        ],
        out_specs=pl.BlockSpec((None, BLOCK, A_Q), lambda b, i: (b, i, 0)),
        out_shape=jax.ShapeDtypeStruct((B, S, A_Q), BF16),
        compiler_params=_cp("parallel", "arbitrary"),
        name="window_attn",
    )(z, z, z, z, z, z, z, bias, sink)


CONV_TS = 256
CONV_TC = 512
CONV_HALO = 16


def _conv_kernel(prev_ref, cur_ref, next_ref, w_ref, o_ref, ext_ref):
    i = pl.program_id(1)
    j = pl.program_id(2)
    ns = pl.num_programs(1)
    ts = cur_ref.shape[0]
    pv = jnp.where(i == 0, 0.0, prev_ref[...].astype(F32))
    nx = jnp.where(i == ns - 1, 0.0, next_ref[...].astype(F32))
    ext_ref[0:CONV_HALO, :] = pv
    ext_ref[CONV_HALO:CONV_HALO + ts, :] = cur_ref[...].astype(F32)
    ext_ref[CONV_HALO + ts:, :] = nx
    half = CONV_WIDTH // 2
    acc = None
    for t in range(CONV_WIDTH):
        term = w_ref[t:t + 1, :] * ext_ref[pl.ds(CONV_HALO - half + t, ts), :]
        acc = term if acc is None else acc + term
    y = _silu(acc)
    heads_per_step = CONV_TC // B_DK
    q_steps = B_QK // CONV_TC

    def l2(scale):
        for hh in range(heads_per_step):
            seg = y[:, hh * B_DK:(hh + 1) * B_DK]
            r = lax.rsqrt(jnp.sum(seg * seg, axis=-1, keepdims=True) + EPS)
            o_ref[:, hh * B_DK:(hh + 1) * B_DK] = seg * (r * scale)

    @pl.when(j < q_steps)
    def _():
        l2(B_DK ** -0.5)

    @pl.when((j >= q_steps) & (j < 2 * q_steps))
    def _():
        l2(1.0)

    @pl.when(j >= 2 * q_steps)
    def _():
        o_ref[...] = y


def conv_qkv(z, conv_w):
    B, S, _ = z.shape
    ts = min(CONV_TS, S)
    c0 = (A_Q + 2 * A_KV) // CONV_TC
    hb = ts // CONV_HALO
    nh = S // CONV_HALO
    return pl.pallas_call(
        _conv_kernel,
        grid=(B, S // ts, B_CONV // CONV_TC),
        in_specs=[
            pl.BlockSpec((None, CONV_HALO, CONV_TC), lambda b, i, j: (b, jnp.maximum(i * hb - 1, 0), c0 + j)),
            pl.BlockSpec((None, ts, CONV_TC), lambda b, i, j: (b, i, c0 + j)),
            pl.BlockSpec((None, CONV_HALO, CONV_TC), lambda b, i, j: (b, jnp.minimum((i + 1) * hb, nh - 1), c0 + j)),
            pl.BlockSpec((CONV_WIDTH, CONV_TC), lambda b, i, j: (0, j)),
        ],
        out_specs=pl.BlockSpec((None, ts, CONV_TC), lambda b, i, j: (b, i, j)),
        out_shape=jax.ShapeDtypeStruct((B, S, B_CONV), F32),
        scratch_shapes=[pltpu.VMEM((ts + 2 * CONV_HALO, CONV_TC), F32)],
        compiler_params=_cp("parallel", "parallel", "arbitrary"),
        name="conv_qkv",
    )(z, z, z, conv_w.astype(F32))


def _bdot(a, b):
    return _dot(a.astype(DELTA_MXU), b.astype(DELTA_MXU), DELTA_PREC)


def _delta_chain(q, k, v, beta_col, gc_col, gc_row, g_end, state, causal, strict, eye, lmask_ref):
    n = q.shape[0]
    decay = jnp.exp(jnp.where(causal, gc_col - gc_row, -jnp.inf))
    kb = k * beta_col
    k16 = k.astype(DELTA_MXU)
    m = jnp.where(strict, _dot_nt(kb.astype(DELTA_MXU), k16, DELTA_PREC) * decay, 0.0)
    inv = eye - m * lmask_ref[0]
    for l in range(1, lmask_ref.shape[0]):
        i16 = inv.astype(DELTA_MXU)
        inv = inv - _bdot(_bdot(i16, m * lmask_ref[l]), i16)
    eg = jnp.exp(gc_col)
    sol = _bdot(inv, jnp.concatenate([v * beta_col, kb * eg], axis=1))
    u = sol[:, :B_DV]
    w = sol[:, B_DV:]
    a = _dot_nt(q.astype(DELTA_MXU), k16, DELTA_PREC) * decay
    s16 = state.astype(DELTA_MXU)
    ws = _bdot(jnp.concatenate([w, q * eg], axis=0), s16)
    v_new = u - ws[:n]
    v16 = v_new.astype(DELTA_MXU)
    o = ws[n:] + _bdot(a, v16)
    k_tail = (k * jnp.exp(g_end - gc_col)).astype(DELTA_MXU)
    state = state * jnp.exp(g_end) + lax.dot_general(
        k_tail, v16, (((0,), (0,)), ((), ())), precision=DELTA_PREC, preferred_element_type=F32)
    return o, state


def _delta_kernel(qf_ref, kf_ref, vf_ref, glf_ref, gtf_ref, qb_ref, kb_ref, vb_ref, glb_ref, gtb_ref,
                  arow_ref, drow_ref, acol_ref, dcol_ref, lmask_ref, of_ref, ob_ref, st_ref):
    c = pl.program_id(2)

    @pl.when(c == 0)
    def _():
        st_ref[...] = jnp.zeros_like(st_ref)

    CHUNK = DELTA_CH
    ri = lax.broadcasted_iota(jnp.int32, (CHUNK, CHUNK), 0)
    ci = lax.broadcasted_iota(jnp.int32, (CHUNK, CHUNK), 1)
    eye = (ri == ci).astype(F32)
    lower = (ri >= ci).astype(F32)
    upper = (ri <= ci).astype(F32)
    hb = DELTA_HB

    for d, (q_ref, k_ref, v_ref, gl_ref, gt_ref, o_ref) in enumerate(
            ((qf_ref, kf_ref, vf_ref, glf_ref, gtf_ref, of_ref),
             (qb_ref, kb_ref, vb_ref, glb_ref, gtb_ref, ob_ref))):
        fwd = d == 0
        gl = gl_ref[...]
        gt = gt_ref[...]
        beta_all = _sigmoid(gl)
        g_all = arow_ref[...] * _softplus(gl + drow_ref[...])
        gT_all = acol_ref[...] * _softplus(gt + dcol_ref[...])
        if fwd:
            gc_all = _dot(lower, g_all, HI)
            gcT_all = _dot(gT_all, upper, HI)
            causal, strict = ri >= ci, ri > ci
        else:
            gc_all = _dot(upper, g_all, HI)
            gcT_all = _dot(gT_all, lower, HI)
            causal, strict = ri <= ci, ri < ci
        end = CHUNK - 1 if fwd else 0
        for hh in range(hb):
            bl = d * hb + hh
            al = (2 + d) * hb + hh
            sl = slice(hh * B_DK, (hh + 1) * B_DK)
            gc_col = gc_all[:, al:al + 1]
            o, st = _delta_chain(
                q_ref[:, sl], k_ref[:, sl], v_ref[:, sl], beta_all[:, bl:bl + 1],
                gc_col, gcT_all[al:al + 1, :], gc_col[end:end + 1, :], st_ref[d * hb + hh],
                causal, strict, eye, lmask_ref)
            o_ref[:, sl] = o
            st_ref[d * hb + hh] = st


def delta_rule(qkv, gl, a_log, dt_bias):
    B, S, _ = qkv.shape
    CHUNK = DELTA_CH
    n = S // CHUNK
    hb, hg = DELTA_HB, DELTA_HG
    ri = jnp.arange(CHUNK)[:, None]
    ci = jnp.arange(CHUNK)[None, :]
    lmask = jnp.stack([((ri >> (l + 1)) == (ci >> (l + 1))) & ((ri >> l) != (ci >> l))
                       for l in range(int(math.log2(CHUNK)))]).astype(F32)
    gw = 4 * hb
    g4 = gl[..., :N_GATE].reshape(B, S, 4, hg, hb)
    g4 = jnp.transpose(g4, (0, 3, 1, 2, 4)).reshape(B, hg, S, gw)
    g_rows = jnp.pad(g4, ((0, 0), (0, 0), (0, 0), (0, LANE - gw)))
    g_cols = jnp.transpose(g4.reshape(B, hg, n, CHUNK, gw), (0, 1, 2, 4, 3))
    neg_a = -jnp.exp(a_log.astype(F32)).reshape(2, hg, hb)
    dtb = dt_bias.astype(F32).reshape(2, hg, hb)
    zeros = jnp.zeros((2, hg, hb), F32)
    a4 = jnp.transpose(jnp.concatenate([zeros, neg_a], 0), (1, 0, 2)).reshape(hg, gw)
    d4 = jnp.transpose(jnp.concatenate([zeros, dtb], 0), (1, 0, 2)).reshape(hg, gw)
    arow = jnp.pad(a4, ((0, 0), (0, LANE - gw))).reshape(hg, 1, LANE)
    drow = jnp.pad(d4, ((0, 0), (0, LANE - gw))).reshape(hg, 1, LANE)
    acol = a4.reshape(hg, gw, 1)
    dcol = d4.reshape(hg, gw, 1)
    cw = hb * B_DK
    kq, kk, kv = 0, B_QK // cw, 2 * B_QK // cw

    def seq_specs(cidx):
        return [
            pl.BlockSpec((None, CHUNK, cw), lambda b, g, c: (b, cidx(c), kq + g)),
            pl.BlockSpec((None, CHUNK, cw), lambda b, g, c: (b, cidx(c), kk + g)),
            pl.BlockSpec((None, CHUNK, cw), lambda b, g, c: (b, cidx(c), kv + g)),
            pl.BlockSpec((None, None, CHUNK, LANE), lambda b, g, c: (b, g, cidx(c), 0)),
            pl.BlockSpec((None, None, None, gw, CHUNK), lambda b, g, c: (b, g, cidx(c), 0, 0)),
        ]

    fwd_idx = lambda c: c
    bwd_idx = lambda c: n - 1 - c
    par_specs = [
        pl.BlockSpec((None, 1, LANE), lambda b, g, c: (g, 0, 0)),
        pl.BlockSpec((None, 1, LANE), lambda b, g, c: (g, 0, 0)),
        pl.BlockSpec((None, gw, 1), lambda b, g, c: (g, 0, 0)),
        pl.BlockSpec((None, gw, 1), lambda b, g, c: (g, 0, 0)),
        pl.BlockSpec(lmask.shape, lambda b, g, c: (0, 0, 0)),
    ]
    o_f, o_b = pl.pallas_call(
        _delta_kernel,
        grid=(B, hg, n),
        in_specs=seq_specs(fwd_idx) + seq_specs(bwd_idx) + par_specs,
        out_specs=[
            pl.BlockSpec((None, CHUNK, cw), lambda b, g, c: (b, fwd_idx(c), g)),
            pl.BlockSpec((None, CHUNK, cw), lambda b, g, c: (b, bwd_idx(c), g)),
        ],
        out_shape=[jax.ShapeDtypeStruct((B, S, B_V), F32)] * 2,
        scratch_shapes=[pltpu.VMEM((2 * hb, B_DK, B_DV), F32)],
        compiler_params=_cp("parallel", "parallel", "arbitrary"),
        name="delta_rule",
    )(qkv, qkv, qkv, g_rows, g_cols, qkv, qkv, qkv, g_rows, g_cols, arow, drow, acol, dcol, lmask)
    return o_f, o_b


GATE_TC = 1024


def _delta_out_kernel(of_ref, ob_ref, zb_ref, g_ref, o_ref):
    for hh in range(GATE_TC // B_DV):
        sl = slice(hh * B_DV, (hh + 1) * B_DV)
        ob = of_ref[:, sl] + ob_ref[:, sl]
        y = _rms_rows(ob, g_ref[...])
        o_ref[:, sl] = (y * _silu(zb_ref[:, sl].astype(F32))).astype(o_ref.dtype)


def delta_out(o_f, o_b, z2d, onorm, *, tm=512):
    T = o_f.shape[0]
    tm = min(tm, T)
    c0 = (A_Q + 2 * A_KV + B_CONV) // GATE_TC
    return pl.pallas_call(
        _delta_out_kernel,
        grid=(T // tm, B_V // GATE_TC),
        in_specs=[
            pl.BlockSpec((tm, GATE_TC), lambda i, j: (i, j)),
            pl.BlockSpec((tm, GATE_TC), lambda i, j: (i, j)),
            pl.BlockSpec((tm, GATE_TC), lambda i, j: (i, c0 + j)),
            pl.BlockSpec((1, B_DV), lambda i, j: (0, 0)),
        ],
        out_specs=pl.BlockSpec((tm, GATE_TC), lambda i, j: (i, j)),
        out_shape=jax.ShapeDtypeStruct((T, B_V), BF16),
        compiler_params=_cp("parallel", "arbitrary"),
        name="delta_out",
    )(o_f, o_b, z2d, onorm.reshape(1, B_DV).astype(F32))


DIFF_T = 512


def _diff_attn_kernel(lam_ref, far_ref, q_ref, k_ref, v_ref, b_ref, g_ref, o_ref, m_ref, l_ref, acc_ref, *, out_scale):
    h = pl.program_id(1)
    qi = pl.program_id(2)
    kj = pl.program_id(3)
    nk = pl.num_programs(3)
    tk = k_ref.shape[0]

    @pl.when(kj == 0)
    def _():
        m_ref[...] = jnp.full_like(m_ref, -jnp.inf)
        l_ref[...] = jnp.zeros_like(l_ref)
        acc_ref[...] = jnp.zeros_like(acc_ref)

    def step(bias, const):
        v = v_ref[...]
        for t in range(2):
            sl = slice(t * C_QK_DIM, (t + 1) * C_QK_DIM)
            s = _dot_nt(q_ref[:, sl], k_ref[:, sl])
            if bias is not None:
                s = s + bias
            m_loc = jnp.max(s, axis=-1, keepdims=True)
            if const is not None:
                m_loc = m_loc + const
            m_old = m_ref[t]
            m_new = jnp.maximum(m_old, m_loc)
            alpha = jnp.exp2(m_old - m_new)
            shift = m_new if const is None else m_new - const
            p = jnp.exp2(s - jnp.concatenate([shift] * (tk // LANE), axis=1))
            psum = p[:, :LANE]
            for c in range(1, tk // LANE):
                psum = psum + p[:, c * LANE:(c + 1) * LANE]
            l_ref[t] = alpha * l_ref[t] + psum
            acc_ref[t] = (jnp.concatenate([alpha] * (C_V_DIM // LANE), axis=1) * acc_ref[t]
                          + _dot(p.astype(BF16), v))
            m_ref[t] = m_new

    near = jnp.abs(kj - qi) <= 1

    @pl.when(near)
    def _():
        step(b_ref[...], None)

    @pl.when(jnp.logical_not(near))
    def _():
        step(None, jnp.where(kj < qi, far_ref[0, h], far_ref[1, h]))

    @pl.when(kj == nk - 1)
    def _():
        lam = lam_ref[0, 0]
        r0 = 1.0 / jnp.sum(l_ref[0], axis=-1, keepdims=True)
        r1 = 1.0 / jnp.sum(l_ref[1], axis=-1, keepdims=True)
        o = acc_ref[0] * r0 - acc_ref[1] * (lam * r1)
        o_ref[...] = (_rms_rows(o, g_ref[...]) * out_scale).astype(o_ref.dtype)


def diff_attention(z, lam, tab, subln, lam_init):
    B, S, _ = z.shape
    t = min(DIFF_T, S)
    assert t > MAX_DISTANCE
    nq = S // t
    tab2 = tab.T * LOG2E
    period = 2 * t + 1
    u = jnp.arange(period)[None, :]
    d = jnp.arange(-1, 2)[:, None]
    idx = jnp.clip(d * t + u - t, -MAX_DISTANCE, MAX_DISTANCE) + MAX_DISTANCE
    sig = tab2[:, idx]
    skew = jnp.tile(sig, (1, 1, t))[..., :t * (period - 1)].reshape(C_HEADS, 3, t, period - 1)
    btile = skew[..., t:2 * t]
    far = jnp.stack([tab2[:, 0], tab2[:, -1]])
    hw = 2 * C_QK_DIM
    return pl.pallas_call(
        functools.partial(_diff_attn_kernel, out_scale=1.0 - lam_init),
        grid=(B, C_HEADS, nq, nq),
        in_specs=[
            pl.BlockSpec(memory_space=pltpu.SMEM),
            pl.BlockSpec(memory_space=pltpu.SMEM),
            pl.BlockSpec((None, t, hw), lambda b, h, i, j: (b, i, h)),
            pl.BlockSpec((None, t, hw), lambda b, h, i, j: (b, j, C_HEADS + h)),
            pl.BlockSpec((None, t, hw), lambda b, h, i, j: (b, j, 2 * C_HEADS + h)),
            pl.BlockSpec((None, None, t, t), lambda b, h, i, j: (h, jnp.clip(j - i, -1, 1) + 1, 0, 0)),
            pl.BlockSpec((1, C_V_DIM), lambda b, h, i, j: (0, 0)),
        ],
        out_specs=pl.BlockSpec((None, t, C_V_DIM), lambda b, h, i, j: (b, i, h)),
        out_shape=jax.ShapeDtypeStruct((B, S, C_HEADS * C_V_DIM), BF16),
        scratch_shapes=[pltpu.VMEM((2, t, LANE), F32), pltpu.VMEM((2, t, LANE), F32),
                        pltpu.VMEM((2, t, C_V_DIM), F32)],
        compiler_params=_cp("parallel", "parallel", "parallel", "arbitrary"),
        name="diff_attn",
    )(lam.reshape(1, 1).astype(F32), far, z, z, z, btile, subln.reshape(1, C_V_DIM).astype(F32))


def _mem_attn_kernel(x_ref, g_ref, wq_ref, kv_ref, wo_ref, o_ref):
    x = x_ref[...]
    h = _rms_rows(x, g_ref[...]).astype(BF16)
    q = _dot(h, wq_ref[...]).astype(BF16)
    outs = []
    for hd in range(X_HEADS):
        sl = slice(hd * X_HEAD_DIM, (hd + 1) * X_HEAD_DIM)
        s = _dot_nt(q[:, sl], kv_ref[:, sl]) * (X_HEAD_DIM ** -0.5)
        m = jnp.max(s, axis=-1, keepdims=True)
        e = jnp.exp(s - m)
        p = (e / jnp.sum(e, axis=-1, keepdims=True)).astype(BF16)
        outs.append(_dot(p, kv_ref[:, X_WIDTH + hd * X_HEAD_DIM:X_WIDTH + (hd + 1) * X_HEAD_DIM]))
    o = jnp.concatenate(outs, axis=1).astype(BF16)
    o_ref[...] = x + _dot(o, wo_ref[...])


def memory_attention(x, kv, g, wq, wo, *, tm=256):
    B, S, D = x.shape
    tm = min(tm, S)
    return pl.pallas_call(
        _mem_attn_kernel,
        grid=(B, S // tm),
        in_specs=[
            pl.BlockSpec((None, tm, D), lambda b, i: (b, i, 0)),
            pl.BlockSpec((1, D), lambda b, i: (0, 0)),
            pl.BlockSpec((D, X_WIDTH), lambda b, i: (0, 0)),
            pl.BlockSpec((None, MEM_TOKENS, 2 * X_WIDTH), lambda b, i: (b, 0, 0)),
            pl.BlockSpec((X_WIDTH, D), lambda b, i: (0, 0)),
        ],
        out_specs=pl.BlockSpec((None, tm, D), lambda b, i: (b, i, 0)),
        out_shape=jax.ShapeDtypeStruct((B, S, D), F32),
        compiler_params=_cp("parallel", "arbitrary"),
        name="mem_attn",
    )(x, g.reshape(1, D).astype(F32), wq, kv, wo)


def _first_max_onehot(vals, rows):
    m = jnp.max(vals, axis=0, keepdims=True)
    idx = jnp.min(jnp.where(vals == m, rows, vals.shape[0]), axis=0, keepdims=True)
    return rows == idx


def _router_kernel(x_ref, g_ref, w_ref, b_ref, o_ref, h_ref):
    tm = x_ref.shape[0]

    def body(r, c):
        sl = pl.ds(pl.multiple_of(r * LANE, LANE), LANE)
        h_ref[sl, :] = _rms_rows(x_ref[sl, :], g_ref[...])
        return c
    lax.fori_loop(0, tm // LANE, body, 0)
    logits = _dot(h_ref[...], w_ref[...], HI)
    lt = jnp.transpose(logits)[:N_EXPERTS, :]
    scores = _sigmoid(lt)
    biased = scores + b_ref[...]
    per_group = N_EXPERTS // N_GROUPS
    rows8 = lax.broadcasted_iota(jnp.int32, (per_group, tm), 0)
    gscore = []
    for gi in range(N_GROUPS):
        blk = biased[gi * per_group:(gi + 1) * per_group]
        first = _first_max_onehot(blk, rows8)
        m1 = jnp.max(blk, axis=0, keepdims=True)
        m2 = jnp.max(jnp.where(first, -jnp.inf, blk), axis=0, keepdims=True)
        gscore.append(m1 + m2)
    gscore = jnp.concatenate(gscore, axis=0)
    growi = lax.broadcasted_iota(jnp.int32, (N_GROUPS, tm), 0)
    gsel = jnp.zeros((N_GROUPS, tm), jnp.bool_)
    work = gscore
    for _ in range(TOPK_GROUPS):
        oh = _first_max_onehot(work, growi)
        gsel = gsel | oh
        work = jnp.where(oh, -jnp.inf, work)
    gself = gsel.astype(F32)
    masked = jnp.concatenate(
        [jnp.where(gself[gi:gi + 1] > 0.0, biased[gi * per_group:(gi + 1) * per_group], -jnp.inf)
         for gi in range(N_GROUPS)], axis=0)
    erow = lax.broadcasted_iota(jnp.int32, (N_EXPERTS, tm), 0)
    esel = jnp.zeros((N_EXPERTS, tm), jnp.bool_)
    work = masked
    for _ in range(TOP_K):
        oh = _first_max_onehot(work, erow)
        esel = esel | oh
        work = jnp.where(oh, -jnp.inf, work)
    w = jnp.where(esel, scores, 0.0)
    gates = w / jnp.sum(w, axis=0, keepdims=True) * ROUTED_SCALE
    n_shared = SHARED_FF // EXPERT_FF
    full = jnp.concatenate([gates, jnp.ones((8, tm), F32), jnp.zeros((LANE - N_EXPERTS - 8, tm), F32)], axis=0)
    frow = lax.broadcasted_iota(jnp.int32, (LANE, tm), 0)
    full = jnp.where(frow < N_EXPERTS + n_shared, full, 0.0)
    o_ref[...] = jnp.transpose(full)


def route_tokens(x, g, w_router, r_bias, *, tm=256):
    T, D = x.shape
    tm = min(tm, T)
    w = jnp.pad(w_router.astype(F32), ((0, 0), (0, LANE - N_EXPERTS)))
    return pl.pallas_call(
        _router_kernel,
        grid=(T // tm,),
        in_specs=[
            pl.BlockSpec((tm, D), lambda i: (i, 0)),
            pl.BlockSpec((1, D), lambda i: (0, 0)),
            pl.BlockSpec((D, LANE), lambda i: (0, 0)),
            pl.BlockSpec((N_EXPERTS, 1), lambda i: (0, 0)),
        ],
        out_specs=pl.BlockSpec((tm, LANE), lambda i: (i, 0)),
        out_shape=jax.ShapeDtypeStruct((T, LANE), F32),
        scratch_shapes=[pltpu.VMEM((tm, D), F32)],
        compiler_params=_cp("parallel"),
        name="moe_router",
    )(x, g.reshape(1, D).astype(F32), w, r_bias.astype(F32).reshape(N_EXPERTS, 1))


def _moe_kernel(x_ref, g_ref, gate_ref, wgu_ref, wd_ref, gf_ref, o_ref, h_ref, *, tm, final_norm):
    p = pl.program_id(1)
    npair = pl.num_programs(1)

    @pl.when(p == 0)
    def _():
        def body(r, c):
            sl = pl.ds(pl.multiple_of(r * LANE, LANE), LANE)
            x = x_ref[sl, :]
            h_ref[sl, :] = _rms_rows(x, g_ref[...]).astype(BF16)
            o_ref[sl, :] = x
            return c
        lax.fori_loop(0, tm // LANE, body, 0)

    gu = _dot(h_ref[...], wgu_ref[...])
    gates = gate_ref[...]
    lane = lax.broadcasted_iota(jnp.int32, gates.shape, 1)
    F = EXPERT_FF
    hid = []
    for e in range(2):
        ge = jnp.sum(jnp.where(lane == 2 * p + e, gates, 0.0), axis=1, keepdims=True)
        hid.append(_silu(gu[:, e * F:(e + 1) * F]) * gu[:, (2 + e) * F:(3 + e) * F] * ge)
    hid = jnp.concatenate(hid, axis=1).astype(BF16)
    o_ref[...] += _dot(hid, wd_ref[...])

    if final_norm:
        @pl.when(p == npair - 1)
        def _():
            def body(r, c):
                sl = pl.ds(pl.multiple_of(r * LANE, LANE), LANE)
                o_ref[sl, :] = _rms_rows(o_ref[sl, :], gf_ref[...])
                return c
            lax.fori_loop(0, tm // LANE, body, 0)


def moe_experts(x, g, gates, w_gu, w_d, g_final, *, final_norm, tm=512):
    T, D = x.shape
    tm = min(tm, T)
    return pl.pallas_call(
        functools.partial(_moe_kernel, tm=tm, final_norm=final_norm),
        grid=(T // tm, N_PAIRS),
        in_specs=[
            pl.BlockSpec((tm, D), lambda i, p: (i, 0)),
            pl.BlockSpec((1, D), lambda i, p: (0, 0)),
            pl.BlockSpec((tm, LANE), lambda i, p: (i, 0)),
            pl.BlockSpec((None, D, 4 * EXPERT_FF), lambda i, p: (p, 0, 0)),
            pl.BlockSpec((None, 2 * EXPERT_FF, D), lambda i, p: (p, 0, 0)),
            pl.BlockSpec((1, D), lambda i, p: (0, 0)),
        ],
        out_specs=pl.BlockSpec((tm, D), lambda i, p: (i, 0)),
        out_shape=jax.ShapeDtypeStruct((T, D), F32),
        scratch_shapes=[pltpu.VMEM((tm, D), BF16)],
        compiler_params=_cp("parallel", "arbitrary"),
        name="moe_experts",
    )(x, g.reshape(1, D).astype(F32), gates, w_gu, w_d, g_final.reshape(1, D).astype(F32))


def _moe_weights(w_gate, w_up, w_down, s_gate, s_up, s_down):
    D = w_gate.shape[1]
    ns = SHARED_FF // EXPERT_FF
    sg = jnp.transpose(s_gate.reshape(D, ns, EXPERT_FF), (1, 0, 2))
    su = jnp.transpose(s_up.reshape(D, ns, EXPERT_FF), (1, 0, 2))
    wg = jnp.concatenate([w_gate, sg], axis=0).astype(BF16).reshape(N_PAIRS, 2, D, EXPERT_FF)
    wu = jnp.concatenate([w_up, su], axis=0).astype(BF16).reshape(N_PAIRS, 2, D, EXPERT_FF)
    w_gu = jnp.concatenate([wg[:, 0], wg[:, 1], wu[:, 0], wu[:, 1]], axis=-1)
    w_d = jnp.concatenate([w_down, s_down.reshape(ns, EXPERT_FF, D)], axis=0).astype(BF16)
    return w_gu, w_d.reshape(N_PAIRS, 2 * EXPERT_FF, D)


def kernel(x_prompt, x_sample, mem_prompt, mem_sample, rel_bias, norm_mix, norm_mem, norm_memkv, norm_ffn, norm_final, ev_w_in, ev_sinks, ev_conv, ev_a_log, ev_dt_bias, ev_onorm, ev_w_out, od_w_in, od_lambda, od_subln, od_w_out, mx_wq, mx_wkv, mx_wo, moe_router, moe_bias, moe_w_gate, moe_w_up, moe_w_down, sh_w_gate, sh_w_up, sh_w_down):
    tab = _bias_by_rel(rel_bias)

    layers = []
    for l in range(DEPTH):
        i = l // 2
        lw = {}
        if l % 2 == 0:
            w_in = ev_w_in[i]
            lw["w_in"] = w_in[:, :EVEN_MAIN].astype(BF16)
            lw["w_gate_tail"] = jnp.pad(w_in[:, EVEN_MAIN:], ((0, 0), (0, LANE - N_GATE))).astype(BF16)
            lw["w_out_a"] = ev_w_out[i][:A_Q].astype(BF16)
            lw["w_out_b"] = ev_w_out[i][A_Q:].astype(BF16)
        else:
            w_in = od_w_in[i]
            lw["w_in"] = jnp.concatenate([w_in[:, :C_QK] * (C_QK_DIM ** -0.5 * LOG2E), w_in[:, C_QK:]],
                                         axis=1).astype(BF16)
            lw["w_out"] = od_w_out[i].astype(BF16)
            lp = od_lambda[i].astype(F32)
            lam_init = 0.8 - 0.6 * math.exp(-0.3 * l)
            lw["lam_init"] = lam_init
            lw["lam"] = jnp.exp(jnp.sum(lp[0] * lp[1])) - jnp.exp(jnp.sum(lp[2] * lp[3])) + lam_init
        lw["wq"] = mx_wq[l].astype(BF16)
        lw["wkv"] = mx_wkv[l].astype(BF16)
        lw["wo"] = mx_wo[l].astype(BF16)
        lw["w_gu"], lw["w_d"] = _moe_weights(moe_w_gate[l], moe_w_up[l], moe_w_down[l],
                                             sh_w_gate[l], sh_w_up[l], sh_w_down[l])
        layers.append(lw)

    def run(x3, mem):
        B, S, D = x3.shape
        T = B * S
        x = x3.reshape(T, D)
        mem2 = mem.reshape(B * MEM_TOKENS, D)
        for l in range(DEPTH):
            lw = layers[l]
            i = l // 2
            if l % 2 == 0:
                z, gl = norm_mm(x, norm_mix[l], lw["w_in"], lw["w_gate_tail"])
                z3 = z.reshape(B, S, EVEN_MAIN)
                out_a = window_attention(z3, ev_sinks[i], tab)
                qkv = conv_qkv(z3, ev_conv[i])
                o_f, o_b = delta_rule(qkv, gl.reshape(B, S, LANE), ev_a_log[i], ev_dt_bias[i])
                out_b = delta_out(o_f.reshape(T, B_V), o_b.reshape(T, B_V), z, ev_onorm[i])
                x = mm_res(x, [out_a.reshape(T, A_Q), out_b], [lw["w_out_a"], lw["w_out_b"]])
            else:
                z = norm_mm(x, norm_mix[l], lw["w_in"])
                o = diff_attention(z.reshape(B, S, 3 * C_QK), lw["lam"], tab, od_subln[i], lw["lam_init"])
                x = mm_res(x, [o.reshape(T, C_HEADS * C_V_DIM)], [lw["w_out"]])
            kv = norm_mm(mem2, norm_memkv[l], lw["wkv"])
            x = memory_attention(x.reshape(B, S, D), kv.reshape(B, MEM_TOKENS, 2 * X_WIDTH),
                                 norm_mem[l], lw["wq"], lw["wo"]).reshape(T, D)
            gates = route_tokens(x, norm_ffn[l], moe_router[l], moe_bias[l])
            x = moe_experts(x, norm_ffn[l], gates, lw["w_gu"], lw["w_d"], norm_final,
                            final_norm=(l == DEPTH - 1))
        return x.reshape(B, S, D)

    return (run(x_prompt, mem_prompt), run(x_sample, mem_sample))
```

```python
import functools
import math

import jax
import jax.numpy as jnp
from jax import lax
from jax.experimental import pallas as pl
from jax.experimental.pallas import tpu as pltpu

F32 = jnp.float32
BF16 = jnp.bfloat16

D_MODEL = 4096
DEPTH = 2
HEAD_DIM = 128
BLOCK = 128
WINDOW = 128
A_HEADS = 16
A_KV_HEADS = 4
A_GROUP = A_HEADS // A_KV_HEADS
B_HEADS = 16
B_DK = 128
B_DV = 128
CONV_WIDTH = 5
CHUNK = 64
C_HEADS = 16
C_QK_DIM = 128
C_V_DIM = 2 * C_QK_DIM
N_BUCKETS = 32
MAX_DISTANCE = 128
MEM_TOKENS = 256
X_HEADS = 4
X_HEAD_DIM = 128
N_EXPERTS = 64
TOP_K = 8
N_GROUPS = 8
TOPK_GROUPS = 4
EXPERT_FF = 128
SHARED_FF = 512
ROUTED_SCALE = 2.5
EPS = 1e-6

A_Q = A_HEADS * HEAD_DIM
A_KV = A_KV_HEADS * HEAD_DIM
B_QK = B_HEADS * B_DK
B_V = B_HEADS * B_DV
B_CONV = 2 * B_QK + B_V
EVEN_MAIN = A_Q + 2 * A_KV + B_CONV + B_V
N_GATE = 4 * B_HEADS
C_QK = 2 * C_HEADS * C_QK_DIM
X_WIDTH = X_HEADS * X_HEAD_DIM

LANE = 128
VMEM_LIMIT = 56 * 1024 * 1024
HI = lax.Precision.HIGHEST
LOG2E = math.log2(math.e)

DELTA_HB = 4
DELTA_HG = B_HEADS // DELTA_HB
DELTA_CH = 256
DELTA_MXU = BF16
DELTA_PREC = None
N_PAIRS = (N_EXPERTS + SHARED_FF // EXPERT_FF) // 2


def _cp(*sem):
    return pltpu.CompilerParams(dimension_semantics=sem, vmem_limit_bytes=VMEM_LIMIT)


def _dot(a, b, prec=None):
    return lax.dot_general(a, b, (((1,), (0,)), ((), ())), precision=prec,
                           preferred_element_type=F32)


def _dot_nt(a, b, prec=None):
    return lax.dot_general(a, b, (((1,), (1,)), ((), ())), precision=prec,
                           preferred_element_type=F32)


def _sigmoid(x):
    return 1.0 / (1.0 + jnp.exp(-x))


def _silu(x):
    return x * _sigmoid(x)


def _softplus(x):
    return jnp.maximum(x, 0.0) + jnp.log(1.0 + jnp.exp(-jnp.abs(x)))


def _rms_rows(x, g):
    ms = jnp.mean(x * x, axis=-1, keepdims=True)
    return x * lax.rsqrt(ms + EPS) * g


def _norm_mm_kernel(x_ref, g_ref, w_ref, *rest, tm, has_tail):
    if has_tail:
        wt_ref, o_ref, ot_ref, h_ref = rest
    else:
        o_ref, h_ref = rest

    @pl.when(pl.program_id(1) == 0)
    def _():
        def body(r, c):
            sl = pl.ds(pl.multiple_of(r * LANE, LANE), LANE)
            h_ref[sl, :] = _rms_rows(x_ref[sl, :], g_ref[...]).astype(BF16)
            return c
        lax.fori_loop(0, tm // LANE, body, 0)
        if has_tail:
            ot_ref[...] = _dot(h_ref[...], wt_ref[...])

    o_ref[...] = _dot(h_ref[...], w_ref[...]).astype(o_ref.dtype)


def norm_mm(x, g, w, w_tail=None, *, tm=512, tn=512, out_dtype=BF16):
    T, D = x.shape
    N = w.shape[1]
    tm = min(tm, T)
    assert T % tm == 0 and N % tn == 0 and tm % LANE == 0
    has_tail = w_tail is not None
    in_specs = [
        pl.BlockSpec((tm, D), lambda i, j: (i, 0)),
        pl.BlockSpec((1, D), lambda i, j: (0, 0)),
        pl.BlockSpec((D, tn), lambda i, j: (0, j)),
    ]
    args = [x, g.reshape(1, D).astype(F32), w]
    out_shape = [jax.ShapeDtypeStruct((T, N), out_dtype)]
    out_specs = [pl.BlockSpec((tm, tn), lambda i, j: (i, j))]
    if has_tail:
        nt = w_tail.shape[1]
        in_specs.append(pl.BlockSpec((D, nt), lambda i, j: (0, 0)))
        args.append(w_tail)
        out_shape.append(jax.ShapeDtypeStruct((T, nt), F32))
        out_specs.append(pl.BlockSpec((tm, nt), lambda i, j: (i, 0)))
    outs = pl.pallas_call(
        functools.partial(_norm_mm_kernel, tm=tm, has_tail=has_tail),
        grid=(T // tm, N // tn),
        in_specs=in_specs, out_specs=out_specs, out_shape=out_shape,
        scratch_shapes=[pltpu.VMEM((tm, D), BF16)],
        compiler_params=_cp("parallel", "arbitrary"),
        name="norm_mm",
    )(*args)
    return outs if has_tail else outs[0]


def _mm_res_kernel(*refs, n_a):
    res_ref = refs[0]
    a_refs = refs[1:1 + n_a]
    w_refs = refs[1 + n_a:1 + 2 * n_a]
    o_ref = refs[-1]
    acc = res_ref[...]
    for a, w in zip(a_refs, w_refs):
        acc = acc + _dot(a[...], w[...])
    o_ref[...] = acc


def mm_res(res, a_list, w_list, *, tm=1024, tn=512):
    T, N = res.shape
    tm = min(tm, T)
    assert T % tm == 0 and N % tn == 0
    n_a = len(a_list)
    in_specs = [pl.BlockSpec((tm, tn), lambda i, j: (i, j))]
    for a in a_list:
        in_specs.append(pl.BlockSpec((tm, a.shape[1]), lambda i, j: (i, 0)))
    for w in w_list:
        in_specs.append(pl.BlockSpec((w.shape[0], tn), lambda i, j: (0, j)))
    return pl.pallas_call(
        functools.partial(_mm_res_kernel, n_a=n_a),
        grid=(T // tm, N // tn),
        in_specs=in_specs,
        out_specs=pl.BlockSpec((tm, tn), lambda i, j: (i, j)),
        out_shape=jax.ShapeDtypeStruct((T, N), F32),
        compiler_params=_cp("parallel", "arbitrary"),
        name="mm_res",
    )(res, *a_list, *w_list)


def _t5_bucket(rel):
    nb = N_BUCKETS // 2
    max_exact = nb // 2
    ret = jnp.where(rel > 0, nb, 0)
    n = jnp.abs(rel)
    n_f = jnp.maximum(n, max_exact).astype(F32)
    large = max_exact + (jnp.log(n_f / max_exact) / math.log(MAX_DISTANCE / max_exact)
                         * (nb - max_exact)).astype(jnp.int32)
    large = jnp.minimum(large, nb - 1)
    return ret + jnp.where(n < max_exact, n, large)


def _bias_by_rel(rel_bias):
    rel = jnp.arange(-MAX_DISTANCE, MAX_DISTANCE + 1, dtype=jnp.int32)
    return rel_bias.astype(F32)[_t5_bucket(rel)]


def _win_attn_kernel(q_ref, kp_ref, kc_ref, kn_ref, vp_ref, vc_ref, vn_ref, bias_ref, sink_ref, o_ref):
    i = pl.program_id(1)
    nb = pl.num_programs(1)
    col = lax.broadcasted_iota(jnp.int32, (1, 3 * BLOCK), 1)
    invalid = ((col < BLOCK) & (i == 0)) | ((col >= 2 * BLOCK) & (i == nb - 1))
    scale = HEAD_DIM ** -0.5
    for hk in range(A_KV_HEADS):
        ks = slice(hk * HEAD_DIM, (hk + 1) * HEAD_DIM)
        qh = jnp.concatenate(
            [q_ref[:, (hk * A_GROUP + g) * HEAD_DIM:(hk * A_GROUP + g + 1) * HEAD_DIM]
             for g in range(A_GROUP)], axis=0)
        kb = jnp.concatenate([kp_ref[:, ks], kc_ref[:, ks], kn_ref[:, ks]], axis=0)
        vb = jnp.concatenate([vp_ref[:, ks], vc_ref[:, ks], vn_ref[:, ks]], axis=0)
        s = _dot_nt(qh, kb) * scale + bias_ref[hk]
        s = jnp.where(invalid, -jnp.inf, s)
        sink = sink_ref[hk]
        m = jnp.maximum(jnp.max(s, axis=-1, keepdims=True), sink)
        p = jnp.exp(s - m)
        den = jnp.sum(p, axis=-1, keepdims=True) + jnp.exp(sink - m)
        p = (p / den).astype(BF16)
        o = _dot(p, vb)
        for g in range(A_GROUP):
            h = hk * A_GROUP + g
            o_ref[:, h * HEAD_DIM:(h + 1) * HEAD_DIM] = o[g * BLOCK:(g + 1) * BLOCK].astype(o_ref.dtype)


def window_attention(z, sinks, tab):
    B, S, _ = z.shape
    nb = S // BLOCK
    q_loc = jnp.arange(BLOCK)[:, None]
    k_loc = jnp.arange(3 * BLOCK)[None, :] - BLOCK
    rel = k_loc - q_loc
    bias = jnp.where((jnp.abs(rel) <= WINDOW)[..., None],
                     tab[jnp.clip(rel, -MAX_DISTANCE, MAX_DISTANCE) + MAX_DISTANCE], -jnp.inf)
    bias = jnp.moveaxis(bias, -1, 0).reshape(A_KV_HEADS, A_GROUP * BLOCK, 3 * BLOCK)
    sink = jnp.broadcast_to(sinks.astype(F32).reshape(A_KV_HEADS, A_GROUP, 1, 1),
                            (A_KV_HEADS, A_GROUP, BLOCK, 1)).reshape(A_KV_HEADS, A_GROUP * BLOCK, 1)
    kcol = A_Q // A_KV
    vcol = (A_Q + A_KV) // A_KV
    prev = lambda b, i: (b, jnp.maximum(i - 1, 0))
    nxt = lambda b, i: (b, jnp.minimum(i + 1, nb - 1))
    return pl.pallas_call(
        _win_attn_kernel,
        grid=(B, nb),
        in_specs=[
            pl.BlockSpec((None, BLOCK, A_Q), lambda b, i: (b, i, 0)),
            pl.BlockSpec((None, BLOCK, A_KV), lambda b, i: (*prev(b, i), kcol)),
            pl.BlockSpec((None, BLOCK, A_KV), lambda b, i: (b, i, kcol)),
            pl.BlockSpec((None, BLOCK, A_KV), lambda b, i: (*nxt(b, i), kcol)),
            pl.BlockSpec((None, BLOCK, A_KV), lambda b, i: (*prev(b, i), vcol)),
            pl.BlockSpec((None, BLOCK, A_KV), lambda b, i: (b, i, vcol)),
            pl.BlockSpec((None, BLOCK, A_KV), lambda b, i: (*nxt(b, i), vcol)),
            pl.BlockSpec((A_KV_HEADS, A_GROUP * BLOCK, 3 * BLOCK), lambda b, i: (0, 0, 0)),
            pl.BlockSpec((A_KV_HEADS, A_GROUP * BLOCK, 1), lambda b, i: (0, 0, 0)),
        ],
        out_specs=pl.BlockSpec((None, BLOCK, A_Q), lambda b, i: (b, i, 0)),
        out_shape=jax.ShapeDtypeStruct((B, S, A_Q), BF16),
        compiler_params=_cp("parallel", "arbitrary"),
        name="window_attn",
    )(z, z, z, z, z, z, z, bias, sink)


CONV_TS = 256
CONV_TC = 512
CONV_HALO = 16


def _conv_kernel(prev_ref, cur_ref, next_ref, w_ref, o_ref, ext_ref):
    i = pl.program_id(1)
    j = pl.program_id(2)
    ns = pl.num_programs(1)
    ts = cur_ref.shape[0]
    pv = jnp.where(i == 0, 0.0, prev_ref[...].astype(F32))
    nx = jnp.where(i == ns - 1, 0.0, next_ref[...].astype(F32))
    ext_ref[0:CONV_HALO, :] = pv
    ext_ref[CONV_HALO:CONV_HALO + ts, :] = cur_ref[...].astype(F32)
    ext_ref[CONV_HALO + ts:, :] = nx
    half = CONV_WIDTH // 2
    acc = None
    for t in range(CONV_WIDTH):
        term = w_ref[t:t + 1, :] * ext_ref[pl.ds(CONV_HALO - half + t, ts), :]
        acc = term if acc is None else acc + term
    y = _silu(acc)
    heads_per_step = CONV_TC // B_DK
    q_steps = B_QK // CONV_TC

    def l2(scale):
        for hh in range(heads_per_step):
            seg = y[:, hh * B_DK:(hh + 1) * B_DK]
            r = lax.rsqrt(jnp.sum(seg * seg, axis=-1, keepdims=True) + EPS)
            o_ref[:, hh * B_DK:(hh + 1) * B_DK] = seg * (r * scale)

    @pl.when(j < q_steps)
    def _():
        l2(B_DK ** -0.5)

    @pl.when((j >= q_steps) & (j < 2 * q_steps))
    def _():
        l2(1.0)

    @pl.when(j >= 2 * q_steps)
    def _():
        o_ref[...] = y


def conv_qkv(z, conv_w):
    B, S, _ = z.shape
    ts = min(CONV_TS, S)
    c0 = (A_Q + 2 * A_KV) // CONV_TC
    hb = ts // CONV_HALO
    nh = S // CONV_HALO
    return pl.pallas_call(
        _conv_kernel,
        grid=(B, S // ts, B_CONV // CONV_TC),
        in_specs=[
            pl.BlockSpec((None, CONV_HALO, CONV_TC), lambda b, i, j: (b, jnp.maximum(i * hb - 1, 0), c0 + j)),
            pl.BlockSpec((None, ts, CONV_TC), lambda b, i, j: (b, i, c0 + j)),
            pl.BlockSpec((None, CONV_HALO, CONV_TC), lambda b, i, j: (b, jnp.minimum((i + 1) * hb, nh - 1), c0 + j)),
            pl.BlockSpec((CONV_WIDTH, CONV_TC), lambda b, i, j: (0, j)),
        ],
        out_specs=pl.BlockSpec((None, ts, CONV_TC), lambda b, i, j: (b, i, j)),
        out_shape=jax.ShapeDtypeStruct((B, S, B_CONV), F32),
        scratch_shapes=[pltpu.VMEM((ts + 2 * CONV_HALO, CONV_TC), F32)],
        compiler_params=_cp("parallel", "parallel", "arbitrary"),
        name="conv_qkv",
    )(z, z, z, conv_w.astype(F32))


def _bdot(a, b):
    return _dot(a.astype(DELTA_MXU), b.astype(DELTA_MXU), DELTA_PREC)


def _delta_chains(ch, eye, lmask_ref):
    n = ch[0]["q"].shape[0]
    for c in ch:
        c["decay"] = jnp.exp(jnp.where(c["causal"], c["gc_col"] - c["gc_row"], -jnp.inf))
        c["kb"] = c["k"] * c["beta"]
        c["k16"] = c["k"].astype(DELTA_MXU)
    for c in ch:
        c["m"] = jnp.where(c["strict"], _dot_nt(c["kb"].astype(DELTA_MXU), c["k16"], DELTA_PREC) * c["decay"], 0.0)
    for c in ch:
        c["inv"] = eye - c["m"] * lmask_ref[0]
    for l in range(1, lmask_ref.shape[0]):
        for c in ch:
            c["i16"] = c["inv"].astype(DELTA_MXU)
            c["p"] = _bdot(c["i16"], c["m"] * lmask_ref[l])
        for c in ch:
            c["inv"] = c["inv"] - _bdot(c["p"], c["i16"])
    for c in ch:
        c["eg"] = jnp.exp(c["gc_col"])
        c["sol"] = _bdot(c["inv"], jnp.concatenate([c["v"] * c["beta"], c["kb"] * c["eg"]], axis=1))
        c["a"] = _dot_nt(c["q"].astype(DELTA_MXU), c["k16"], DELTA_PREC) * c["decay"]
    for c in ch:
        w = c["sol"][:, B_DV:]
        c["ws"] = _bdot(jnp.concatenate([w, c["q"] * c["eg"]], axis=0), c["state"])
    out = []
    for c in ch:
        v_new = c["sol"][:, :B_DV] - c["ws"][:n]
        v16 = v_new.astype(DELTA_MXU)
        o = c["ws"][n:] + _bdot(c["a"], v16)
        k_tail = (c["k"] * jnp.exp(c["g_end"] - c["gc_col"])).astype(DELTA_MXU)
        state = c["state"] * jnp.exp(c["g_end"]) + lax.dot_general(
            k_tail, v16, (((0,), (0,)), ((), ())), precision=DELTA_PREC, preferred_element_type=F32)
        out.append((o, state))
    return out


def _delta_kernel(qf_ref, kf_ref, vf_ref, glf_ref, gtf_ref, qb_ref, kb_ref, vb_ref, glb_ref, gtb_ref,
                  arow_ref, drow_ref, acol_ref, dcol_ref, lmask_ref, of_ref, ob_ref, st_ref):
    c = pl.program_id(2)

    @pl.when(c == 0)
    def _():
        st_ref[...] = jnp.zeros_like(st_ref)

    CHUNK = DELTA_CH
    ri = lax.broadcasted_iota(jnp.int32, (CHUNK, CHUNK), 0)
    ci = lax.broadcasted_iota(jnp.int32, (CHUNK, CHUNK), 1)
    eye = (ri == ci).astype(F32)
    lower = (ri >= ci).astype(F32)
    upper = (ri <= ci).astype(F32)
    hb = DELTA_HB

    chains, sinks = [], []
    for d, (q_ref, k_ref, v_ref, gl_ref, gt_ref, o_ref) in enumerate(
            ((qf_ref, kf_ref, vf_ref, glf_ref, gtf_ref, of_ref),
             (qb_ref, kb_ref, vb_ref, glb_ref, gtb_ref, ob_ref))):
        fwd = d == 0
        gl = gl_ref[...]
        gt = gt_ref[...]
        beta_all = _sigmoid(gl)
        g_all = arow_ref[...] * _softplus(gl + drow_ref[...])
        gT_all = acol_ref[...] * _softplus(gt + dcol_ref[...])
        if fwd:
            gc_all = _dot(lower, g_all, HI)
            gcT_all = _dot(gT_all, upper, HI)
            causal, strict = ri >= ci, ri > ci
        else:
            gc_all = _dot(upper, g_all, HI)
            gcT_all = _dot(gT_all, lower, HI)
            causal, strict = ri <= ci, ri < ci
        end = CHUNK - 1 if fwd else 0
        for hh in range(hb):
            bl = d * hb + hh
            al = (2 + d) * hb + hh
            sl = slice(hh * B_DK, (hh + 1) * B_DK)
            gc_col = gc_all[:, al:al + 1]
            chains.append(dict(
                q=q_ref[:, sl], k=k_ref[:, sl], v=v_ref[:, sl], beta=beta_all[:, bl:bl + 1],
                gc_col=gc_col, gc_row=gcT_all[al:al + 1, :], g_end=gc_col[end:end + 1, :],
                state=st_ref[d * hb + hh], causal=causal, strict=strict))
            sinks.append((o_ref, sl, d * hb + hh))
    for (o, st), (o_ref, sl, si) in zip(_delta_chains(chains, eye, lmask_ref), sinks):
        o_ref[:, sl] = o
        st_ref[si] = st


def delta_rule(qkv, gl, a_log, dt_bias):
    B, S, _ = qkv.shape
    CHUNK = DELTA_CH
    n = S // CHUNK
    hb, hg = DELTA_HB, DELTA_HG
    ri = jnp.arange(CHUNK)[:, None]
    ci = jnp.arange(CHUNK)[None, :]
    lmask = jnp.stack([((ri >> (l + 1)) == (ci >> (l + 1))) & ((ri >> l) != (ci >> l))
                       for l in range(int(math.log2(CHUNK)))]).astype(F32)
    gw = 4 * hb
    g4 = gl[..., :N_GATE].reshape(B, S, 4, hg, hb)
    g4 = jnp.transpose(g4, (0, 3, 1, 2, 4)).reshape(B, hg, S, gw)
    g_rows = jnp.pad(g4, ((0, 0), (0, 0), (0, 0), (0, LANE - gw)))
    g_cols = jnp.transpose(g4.reshape(B, hg, n, CHUNK, gw), (0, 1, 2, 4, 3))
    neg_a = -jnp.exp(a_log.astype(F32)).reshape(2, hg, hb)
    dtb = dt_bias.astype(F32).reshape(2, hg, hb)
    zeros = jnp.zeros((2, hg, hb), F32)
    a4 = jnp.transpose(jnp.concatenate([zeros, neg_a], 0), (1, 0, 2)).reshape(hg, gw)
    d4 = jnp.transpose(jnp.concatenate([zeros, dtb], 0), (1, 0, 2)).reshape(hg, gw)
    arow = jnp.pad(a4, ((0, 0), (0, LANE - gw))).reshape(hg, 1, LANE)
    drow = jnp.pad(d4, ((0, 0), (0, LANE - gw))).reshape(hg, 1, LANE)
    acol = a4.reshape(hg, gw, 1)
    dcol = d4.reshape(hg, gw, 1)
    cw = hb * B_DK
    kq, kk, kv = 0, B_QK // cw, 2 * B_QK // cw

    def seq_specs(cidx):
        return [
            pl.BlockSpec((None, CHUNK, cw), lambda b, g, c: (b, cidx(c), kq + g)),
            pl.BlockSpec((None, CHUNK, cw), lambda b, g, c: (b, cidx(c), kk + g)),
            pl.BlockSpec((None, CHUNK, cw), lambda b, g, c: (b, cidx(c), kv + g)),
            pl.BlockSpec((None, None, CHUNK, LANE), lambda b, g, c: (b, g, cidx(c), 0)),
            pl.BlockSpec((None, None, None, gw, CHUNK), lambda b, g, c: (b, g, cidx(c), 0, 0)),
        ]

    fwd_idx = lambda c: c
    bwd_idx = lambda c: n - 1 - c
    par_specs = [
        pl.BlockSpec((None, 1, LANE), lambda b, g, c: (g, 0, 0)),
        pl.BlockSpec((None, 1, LANE), lambda b, g, c: (g, 0, 0)),
        pl.BlockSpec((None, gw, 1), lambda b, g, c: (g, 0, 0)),
        pl.BlockSpec((None, gw, 1), lambda b, g, c: (g, 0, 0)),
        pl.BlockSpec(lmask.shape, lambda b, g, c: (0, 0, 0)),
    ]
    o_f, o_b = pl.pallas_call(
        _delta_kernel,
        grid=(B, hg, n),
        in_specs=seq_specs(fwd_idx) + seq_specs(bwd_idx) + par_specs,
        out_specs=[
            pl.BlockSpec((None, CHUNK, cw), lambda b, g, c: (b, fwd_idx(c), g)),
            pl.BlockSpec((None, CHUNK, cw), lambda b, g, c: (b, bwd_idx(c), g)),
        ],
        out_shape=[jax.ShapeDtypeStruct((B, S, B_V), F32)] * 2,
        scratch_shapes=[pltpu.VMEM((2 * hb, B_DK, B_DV), F32)],
        compiler_params=_cp("parallel", "parallel", "arbitrary"),
        name="delta_rule",
    )(qkv, qkv, qkv, g_rows, g_cols, qkv, qkv, qkv, g_rows, g_cols, arow, drow, acol, dcol, lmask)
    return o_f, o_b


GATE_TC = 1024


def _delta_out_kernel(of_ref, ob_ref, zb_ref, g_ref, o_ref):
    for hh in range(GATE_TC // B_DV):
        sl = slice(hh * B_DV, (hh + 1) * B_DV)
        ob = of_ref[:, sl] + ob_ref[:, sl]
        y = _rms_rows(ob, g_ref[...])
        o_ref[:, sl] = (y * _silu(zb_ref[:, sl].astype(F32))).astype(o_ref.dtype)


def delta_out(o_f, o_b, z2d, onorm, *, tm=512):
    T = o_f.shape[0]
    tm = min(tm, T)
    c0 = (A_Q + 2 * A_KV + B_CONV) // GATE_TC
    return pl.pallas_call(
        _delta_out_kernel,
        grid=(T // tm, B_V // GATE_TC),
        in_specs=[
            pl.BlockSpec((tm, GATE_TC), lambda i, j: (i, j)),
            pl.BlockSpec((tm, GATE_TC), lambda i, j: (i, j)),
            pl.BlockSpec((tm, GATE_TC), lambda i, j: (i, c0 + j)),
            pl.BlockSpec((1, B_DV), lambda i, j: (0, 0)),
        ],
        out_specs=pl.BlockSpec((tm, GATE_TC), lambda i, j: (i, j)),
        out_shape=jax.ShapeDtypeStruct((T, B_V), BF16),
        compiler_params=_cp("parallel", "arbitrary"),
        name="delta_out",
    )(o_f, o_b, z2d, onorm.reshape(1, B_DV).astype(F32))


DIFF_T = 512


def _diff_attn_kernel(lam_ref, far_ref, q_ref, k_ref, v_ref, b_ref, g_ref, o_ref, m_ref, l_ref, acc_ref, *, out_scale):
    h = pl.program_id(1)
    qi = pl.program_id(2)
    tk = q_ref.shape[0]
    nk = k_ref.shape[0] // tk

    m_ref[...] = jnp.full_like(m_ref, -jnp.inf)
    l_ref[...] = jnp.zeros_like(l_ref)
    acc_ref[...] = jnp.zeros_like(acc_ref)

    def step(kj, bias, const):
        rows = pl.ds(pl.multiple_of(kj * tk, tk), tk)
        v = v_ref[rows, :]
        for t in range(2):
            sl = slice(t * C_QK_DIM, (t + 1) * C_QK_DIM)
            s = _dot_nt(q_ref[:, sl], k_ref[rows, sl])
            if bias is not None:
                s = s + bias
            m_loc = jnp.max(s, axis=-1, keepdims=True)
            if const is not None:
                m_loc = m_loc + const
            m_old = m_ref[t]
            m_new = jnp.maximum(m_old, m_loc)
            alpha = jnp.exp2(m_old - m_new)
            shift = m_new if const is None else m_new - const
            p = jnp.exp2(s - jnp.concatenate([shift] * (tk // LANE), axis=1))
            psum = p[:, :LANE]
            for c in range(1, tk // LANE):
                psum = psum + p[:, c * LANE:(c + 1) * LANE]
            l_ref[t] = alpha * l_ref[t] + psum
            acc_ref[t] = (jnp.concatenate([alpha] * (C_V_DIM // LANE), axis=1) * acc_ref[t]
                          + _dot(p.astype(BF16), v))
            m_ref[t] = m_new

    def far_left(kj, carry):
        step(kj, None, far_ref[0, h])
        return carry

    def far_right(kj, carry):
        step(kj, None, far_ref[1, h])
        return carry

    lax.fori_loop(0, jnp.maximum(qi - 1, 0), far_left, 0)
    for dj in range(-1, 2):
        @pl.when((qi + dj >= 0) & (qi + dj < nk))
        def _():
            step(qi + dj, b_ref[dj + 1], None)
    lax.fori_loop(jnp.minimum(qi + 2, nk), nk, far_right, 0)

    lam = lam_ref[0, 0]
    r0 = 1.0 / jnp.sum(l_ref[0], axis=-1, keepdims=True)
    r1 = 1.0 / jnp.sum(l_ref[1], axis=-1, keepdims=True)
    o = acc_ref[0] * r0 - acc_ref[1] * (lam * r1)
    o_ref[...] = (_rms_rows(o, g_ref[...]) * out_scale).astype(o_ref.dtype)


def diff_attention(z, lam, tab, subln, lam_init):
    B, S, _ = z.shape
    t = min(DIFF_T, S)
    assert t > MAX_DISTANCE
    nq = S // t
    tab2 = tab.T * LOG2E
    period = 2 * t + 1
    u = jnp.arange(period)[None, :]
    d = jnp.arange(-1, 2)[:, None]
    idx = jnp.clip(d * t + u - t, -MAX_DISTANCE, MAX_DISTANCE) + MAX_DISTANCE
    sig = tab2[:, idx]
    skew = jnp.tile(sig, (1, 1, t))[..., :t * (period - 1)].reshape(C_HEADS, 3, t, period - 1)
    btile = skew[..., t:2 * t]
    far = jnp.stack([tab2[:, 0], tab2[:, -1]])
    hw = 2 * C_QK_DIM
    return pl.pallas_call(
        functools.partial(_diff_attn_kernel, out_scale=1.0 - lam_init),
        grid=(B, C_HEADS, nq),
        in_specs=[
            pl.BlockSpec(memory_space=pltpu.SMEM),
            pl.BlockSpec(memory_space=pltpu.SMEM),
            pl.BlockSpec((None, t, hw), lambda b, h, i: (b, i, h)),
            pl.BlockSpec((None, S, hw), lambda b, h, i: (b, 0, C_HEADS + h)),
            pl.BlockSpec((None, S, hw), lambda b, h, i: (b, 0, 2 * C_HEADS + h)),
            pl.BlockSpec((None, 3, t, t), lambda b, h, i: (h, 0, 0, 0)),
            pl.BlockSpec((1, C_V_DIM), lambda b, h, i: (0, 0)),
        ],
        out_specs=pl.BlockSpec((None, t, C_V_DIM), lambda b, h, i: (b, i, h)),
        out_shape=jax.ShapeDtypeStruct((B, S, C_HEADS * C_V_DIM), BF16),
        scratch_shapes=[pltpu.VMEM((2, t, LANE), F32), pltpu.VMEM((2, t, LANE), F32),
                        pltpu.VMEM((2, t, C_V_DIM), F32)],
        compiler_params=_cp("parallel", "parallel", "arbitrary"),
        name="diff_attn",
    )(lam.reshape(1, 1).astype(F32), far, z, z, z, btile, subln.reshape(1, C_V_DIM).astype(F32))


def _mem_attn_kernel(x_ref, g_ref, wq_ref, kv_ref, wo_ref, o_ref):
    x = x_ref[...]
    h = _rms_rows(x, g_ref[...]).astype(BF16)
    q = _dot(h, wq_ref[...]).astype(BF16)
    outs = []
    for hd in range(X_HEADS):
        sl = slice(hd * X_HEAD_DIM, (hd + 1) * X_HEAD_DIM)
        s = _dot_nt(q[:, sl], kv_ref[:, sl]) * (X_HEAD_DIM ** -0.5)
        m = jnp.max(s, axis=-1, keepdims=True)
        e = jnp.exp(s - m)
        p = (e / jnp.sum(e, axis=-1, keepdims=True)).astype(BF16)
        outs.append(_dot(p, kv_ref[:, X_WIDTH + hd * X_HEAD_DIM:X_WIDTH + (hd + 1) * X_HEAD_DIM]))
    o = jnp.concatenate(outs, axis=1).astype(BF16)
    o_ref[...] = x + _dot(o, wo_ref[...])


def memory_attention(x, kv, g, wq, wo, *, tm=256):
    B, S, D = x.shape
    tm = min(tm, S)
    return pl.pallas_call(
        _mem_attn_kernel,
        grid=(B, S // tm),
        in_specs=[
            pl.BlockSpec((None, tm, D), lambda b, i: (b, i, 0)),
            pl.BlockSpec((1, D), lambda b, i: (0, 0)),
            pl.BlockSpec((D, X_WIDTH), lambda b, i: (0, 0)),
            pl.BlockSpec((None, MEM_TOKENS, 2 * X_WIDTH), lambda b, i: (b, 0, 0)),
            pl.BlockSpec((X_WIDTH, D), lambda b, i: (0, 0)),
        ],
        out_specs=pl.BlockSpec((None, tm, D), lambda b, i: (b, i, 0)),
        out_shape=jax.ShapeDtypeStruct((B, S, D), F32),
        compiler_params=_cp("parallel", "arbitrary"),
        name="mem_attn",
    )(x, g.reshape(1, D).astype(F32), wq, kv, wo)


def _first_max_onehot(vals, rows):
    m = jnp.max(vals, axis=0, keepdims=True)
    idx = jnp.min(jnp.where(vals == m, rows, vals.shape[0]), axis=0, keepdims=True)
    return rows == idx


def _router_kernel(x_ref, g_ref, w_ref, b_ref, o_ref, h_ref):
    tm = x_ref.shape[0]

    def body(r, c):
        sl = pl.ds(pl.multiple_of(r * LANE, LANE), LANE)
        h_ref[sl, :] = _rms_rows(x_ref[sl, :], g_ref[...])
        return c
    lax.fori_loop(0, tm // LANE, body, 0)
    logits = _dot(h_ref[...], w_ref[...], HI)
    lt = jnp.transpose(logits)[:N_EXPERTS, :]
    scores = _sigmoid(lt)
    biased = scores + b_ref[...]
    per_group = N_EXPERTS // N_GROUPS
    rows8 = lax.broadcasted_iota(jnp.int32, (per_group, tm), 0)
    gscore = []
    for gi in range(N_GROUPS):
        blk = biased[gi * per_group:(gi + 1) * per_group]
        first = _first_max_onehot(blk, rows8)
        m1 = jnp.max(blk, axis=0, keepdims=True)
        m2 = jnp.max(jnp.where(first, -jnp.inf, blk), axis=0, keepdims=True)
        gscore.append(m1 + m2)
    gscore = jnp.concatenate(gscore, axis=0)
    growi = lax.broadcasted_iota(jnp.int32, (N_GROUPS, tm), 0)
    gsel = jnp.zeros((N_GROUPS, tm), jnp.bool_)
    work = gscore
    for _ in range(TOPK_GROUPS):
        oh = _first_max_onehot(work, growi)
        gsel = gsel | oh
        work = jnp.where(oh, -jnp.inf, work)
    gself = gsel.astype(F32)
    masked = jnp.concatenate(
        [jnp.where(gself[gi:gi + 1] > 0.0, biased[gi * per_group:(gi + 1) * per_group], -jnp.inf)
         for gi in range(N_GROUPS)], axis=0)
    erow = lax.broadcasted_iota(jnp.int32, (N_EXPERTS, tm), 0)
    esel = jnp.zeros((N_EXPERTS, tm), jnp.bool_)
    work = masked
    for _ in range(TOP_K):
        oh = _first_max_onehot(work, erow)
        esel = esel | oh
        work = jnp.where(oh, -jnp.inf, work)
    w = jnp.where(esel, scores, 0.0)
    gates = w / jnp.sum(w, axis=0, keepdims=True) * ROUTED_SCALE
    n_shared = SHARED_FF // EXPERT_FF
    full = jnp.concatenate([gates, jnp.ones((8, tm), F32), jnp.zeros((LANE - N_EXPERTS - 8, tm), F32)], axis=0)
    frow = lax.broadcasted_iota(jnp.int32, (LANE, tm), 0)
    full = jnp.where(frow < N_EXPERTS + n_shared, full, 0.0)
    o_ref[...] = jnp.transpose(full)


def route_tokens(x, g, w_router, r_bias, *, tm=256):
    T, D = x.shape
    tm = min(tm, T)
    w = jnp.pad(w_router.astype(F32), ((0, 0), (0, LANE - N_EXPERTS)))
    return pl.pallas_call(
        _router_kernel,
        grid=(T // tm,),
        in_specs=[
            pl.BlockSpec((tm, D), lambda i: (i, 0)),
            pl.BlockSpec((1, D), lambda i: (0, 0)),
            pl.BlockSpec((D, LANE), lambda i: (0, 0)),
            pl.BlockSpec((N_EXPERTS, 1), lambda i: (0, 0)),
        ],
        out_specs=pl.BlockSpec((tm, LANE), lambda i: (i, 0)),
        out_shape=jax.ShapeDtypeStruct((T, LANE), F32),
        scratch_shapes=[pltpu.VMEM((tm, D), F32)],
        compiler_params=_cp("parallel"),
        name="moe_router",
    )(x, g.reshape(1, D).astype(F32), w, r_bias.astype(F32).reshape(N_EXPERTS, 1))


def _moe_kernel(x_ref, g_ref, gate_ref, wgu_ref, wd_ref, gf_ref, o_ref, h_ref, *, tm, final_norm):
    p = pl.program_id(1)
    npair = pl.num_programs(1)

    @pl.when(p == 0)
    def _():
        def body(r, c):
            sl = pl.ds(pl.multiple_of(r * LANE, LANE), LANE)
            x = x_ref[sl, :]
            h_ref[sl, :] = _rms_rows(x, g_ref[...]).astype(BF16)
            o_ref[sl, :] = x
            return c
        lax.fori_loop(0, tm // LANE, body, 0)

    gu = _dot(h_ref[...], wgu_ref[...])
    gates = gate_ref[...]
    lane = lax.broadcasted_iota(jnp.int32, gates.shape, 1)
    F = EXPERT_FF
    hid = []
    for e in range(2):
        ge = jnp.sum(jnp.where(lane == 2 * p + e, gates, 0.0), axis=1, keepdims=True)
        hid.append(_silu(gu[:, e * F:(e + 1) * F]) * gu[:, (2 + e) * F:(3 + e) * F] * ge)
    hid = jnp.concatenate(hid, axis=1).astype(BF16)
    o_ref[...] += _dot(hid, wd_ref[...])

    if final_norm:
        @pl.when(p == npair - 1)
        def _():
            def body(r, c):
                sl = pl.ds(pl.multiple_of(r * LANE, LANE), LANE)
                o_ref[sl, :] = _rms_rows(o_ref[sl, :], gf_ref[...])
                return c
            lax.fori_loop(0, tm // LANE, body, 0)


def moe_experts(x, g, gates, w_gu, w_d, g_final, *, final_norm, tm=512):
    T, D = x.shape
    tm = min(tm, T)
    return pl.pallas_call(
        functools.partial(_moe_kernel, tm=tm, final_norm=final_norm),
        grid=(T // tm, N_PAIRS),
        in_specs=[
            pl.BlockSpec((tm, D), lambda i, p: (i, 0)),
            pl.BlockSpec((1, D), lambda i, p: (0, 0)),
            pl.BlockSpec((tm, LANE), lambda i, p: (i, 0)),
            pl.BlockSpec((None, D, 4 * EXPERT_FF), lambda i, p: (p, 0, 0)),
            pl.BlockSpec((None, 2 * EXPERT_FF, D), lambda i, p: (p, 0, 0)),
            pl.BlockSpec((1, D), lambda i, p: (0, 0)),
        ],
        out_specs=pl.BlockSpec((tm, D), lambda i, p: (i, 0)),
        out_shape=jax.ShapeDtypeStruct((T, D), F32),
        scratch_shapes=[pltpu.VMEM((tm, D), BF16)],
        compiler_params=_cp("parallel", "arbitrary"),
        name="moe_experts",
    )(x, g.reshape(1, D).astype(F32), gates, w_gu, w_d, g_final.reshape(1, D).astype(F32))


def _moe_weights(w_gate, w_up, w_down, s_gate, s_up, s_down):
    D = w_gate.shape[1]
    ns = SHARED_FF // EXPERT_FF
    sg = jnp.transpose(s_gate.reshape(D, ns, EXPERT_FF), (1, 0, 2))
    su = jnp.transpose(s_up.reshape(D, ns, EXPERT_FF), (1, 0, 2))
    wg = jnp.concatenate([w_gate, sg], axis=0).astype(BF16).reshape(N_PAIRS, 2, D, EXPERT_FF)
    wu = jnp.concatenate([w_up, su], axis=0).astype(BF16).reshape(N_PAIRS, 2, D, EXPERT_FF)
    w_gu = jnp.concatenate([wg[:, 0], wg[:, 1], wu[:, 0], wu[:, 1]], axis=-1)
    w_d = jnp.concatenate([w_down, s_down.reshape(ns, EXPERT_FF, D)], axis=0).astype(BF16)
    return w_gu, w_d.reshape(N_PAIRS, 2 * EXPERT_FF, D)


def kernel(x_prompt, x_sample, mem_prompt, mem_sample, rel_bias, norm_mix, norm_mem, norm_memkv, norm_ffn, norm_final, ev_w_in, ev_sinks, ev_conv, ev_a_log, ev_dt_bias, ev_onorm, ev_w_out, od_w_in, od_lambda, od_subln, od_w_out, mx_wq, mx_wkv, mx_wo, moe_router, moe_bias, moe_w_gate, moe_w_up, moe_w_down, sh_w_gate, sh_w_up, sh_w_down):
    tab = _bias_by_rel(rel_bias)

    layers = []
    for l in range(DEPTH):
        i = l // 2
        lw = {}
        if l % 2 == 0:
            w_in = ev_w_in[i]
            lw["w_in"] = w_in[:, :EVEN_MAIN].astype(BF16)
            lw["w_gate_tail"] = jnp.pad(w_in[:, EVEN_MAIN:], ((0, 0), (0, LANE - N_GATE))).astype(BF16)
            lw["w_out_a"] = ev_w_out[i][:A_Q].astype(BF16)
            lw["w_out_b"] = ev_w_out[i][A_Q:].astype(BF16)
        else:
            w_in = od_w_in[i]
            lw["w_in"] = jnp.concatenate([w_in[:, :C_QK] * (C_QK_DIM ** -0.5 * LOG2E), w_in[:, C_QK:]],
                                         axis=1).astype(BF16)
            lw["w_out"] = od_w_out[i].astype(BF16)
            lp = od_lambda[i].astype(F32)
            lam_init = 0.8 - 0.6 * math.exp(-0.3 * l)
            lw["lam_init"] = lam_init
            lw["lam"] = jnp.exp(jnp.sum(lp[0] * lp[1])) - jnp.exp(jnp.sum(lp[2] * lp[3])) + lam_init
        lw["wq"] = mx_wq[l].astype(BF16)
        lw["wkv"] = mx_wkv[l].astype(BF16)
        lw["wo"] = mx_wo[l].astype(BF16)
        lw["w_gu"], lw["w_d"] = _moe_weights(moe_w_gate[l], moe_w_up[l], moe_w_down[l],
                                             sh_w_gate[l], sh_w_up[l], sh_w_down[l])
        layers.append(lw)

    def run(x3, mem):
        B, S, D = x3.shape
        T = B * S
        x = x3.reshape(T, D)
        mem2 = mem.reshape(B * MEM_TOKENS, D)
        for l in range(DEPTH):
            lw = layers[l]
            i = l // 2
            if l % 2 == 0:
                z, gl = norm_mm(x, norm_mix[l], lw["w_in"], lw["w_gate_tail"])
                z3 = z.reshape(B, S, EVEN_MAIN)
                out_a = window_attention(z3, ev_sinks[i], tab)
                qkv = conv_qkv(z3, ev_conv[i])
                o_f, o_b = delta_rule(qkv, gl.reshape(B, S, LANE), ev_a_log[i], ev_dt_bias[i])
                out_b = delta_out(o_f.reshape(T, B_V), o_b.reshape(T, B_V), z, ev_onorm[i])
                x = mm_res(x, [out_a.reshape(T, A_Q), out_b], [lw["w_out_a"], lw["w_out_b"]])
            else:
                z = norm_mm(x, norm_mix[l], lw["w_in"])
                o = diff_attention(z.reshape(B, S, 3 * C_QK), lw["lam"], tab, od_subln[i], lw["lam_init"])
                x = mm_res(x, [o.reshape(T, C_HEADS * C_V_DIM)], [lw["w_out"]])
            kv = norm_mm(mem2, norm_memkv[l], lw["wkv"])
            x = memory_attention(x.reshape(B, S, D), kv.reshape(B, MEM_TOKENS, 2 * X_WIDTH),
                                 norm_mem[l], lw["wq"], lw["wo"]).reshape(T, D)
            gates = route_tokens(x, norm_ffn[l], moe_router[l], moe_bias[l])
            x = moe_experts(x, norm_ffn[l], gates, lw["w_gu"], lw["w_d"], norm_final,
                            final_norm=(l == DEPTH - 1))
        return x.reshape(B, S, D)

    return (run(x_prompt, mem_prompt), run(x_sample, mem_sample))
```

```python
import functools
import math

import jax
import jax.numpy as jnp
from jax import lax
from jax.experimental import pallas as pl
from jax.experimental.pallas import tpu as pltpu

F32 = jnp.float32
BF16 = jnp.bfloat16

D_MODEL = 4096
DEPTH = 2
HEAD_DIM = 128
BLOCK = 128
WINDOW = 128
A_HEADS = 16
A_KV_HEADS = 4
A_GROUP = A_HEADS // A_KV_HEADS
B_HEADS = 16
B_DK = 128
B_DV = 128
CONV_WIDTH = 5
CHUNK = 64
C_HEADS = 16
C_QK_DIM = 128
C_V_DIM = 2 * C_QK_DIM
N_BUCKETS = 32
MAX_DISTANCE = 128
MEM_TOKENS = 256
X_HEADS = 4
X_HEAD_DIM = 128
N_EXPERTS = 64
TOP_K = 8
N_GROUPS = 8
TOPK_GROUPS = 4
EXPERT_FF = 128
SHARED_FF = 512
ROUTED_SCALE = 2.5
EPS = 1e-6

A_Q = A_HEADS * HEAD_DIM
A_KV = A_KV_HEADS * HEAD_DIM
B_QK = B_HEADS * B_DK
B_V = B_HEADS * B_DV
B_CONV = 2 * B_QK + B_V
EVEN_MAIN = A_Q + 2 * A_KV + B_CONV + B_V
N_GATE = 4 * B_HEADS
C_QK = 2 * C_HEADS * C_QK_DIM
X_WIDTH = X_HEADS * X_HEAD_DIM

LANE = 128
VMEM_LIMIT = 56 * 1024 * 1024
HI = lax.Precision.HIGHEST
LOG2E = math.log2(math.e)

DELTA_HB = 4
DELTA_HG = B_HEADS // DELTA_HB
DELTA_CH = 256
DELTA_MXU = BF16
DELTA_PREC = None
N_PAIRS = (N_EXPERTS + SHARED_FF // EXPERT_FF) // 2


def _cp(*sem):
    return pltpu.CompilerParams(dimension_semantics=sem, vmem_limit_bytes=VMEM_LIMIT)


def _dot(a, b, prec=None):
    return lax.dot_general(a, b, (((1,), (0,)), ((), ())), precision=prec,
                           preferred_element_type=F32)


def _dot_nt(a, b, prec=None):
    return lax.dot_general(a, b, (((1,), (1,)), ((), ())), precision=prec,
                           preferred_element_type=F32)


def _sigmoid(x):
    return 1.0 / (1.0 + jnp.exp(-x))


def _silu(x):
    return x * _sigmoid(x)


def _softplus(x):
    return jnp.maximum(x, 0.0) + jnp.log(1.0 + jnp.exp(-jnp.abs(x)))


def _rms_rows(x, g):
    ms = jnp.mean(x * x, axis=-1, keepdims=True)
    return x * lax.rsqrt(ms + EPS) * g


def _norm_mm_kernel(x_ref, g_ref, w_ref, *rest, tm, has_tail):
    if has_tail:
        wt_ref, o_ref, ot_ref, h_ref = rest
    else:
        o_ref, h_ref = rest

    @pl.when(pl.program_id(1) == 0)
    def _():
        def body(r, c):
            sl = pl.ds(pl.multiple_of(r * LANE, LANE), LANE)
            h_ref[sl, :] = _rms_rows(x_ref[sl, :], g_ref[...]).astype(BF16)
            return c
        lax.fori_loop(0, tm // LANE, body, 0)
        if has_tail:
            ot_ref[...] = _dot(h_ref[...], wt_ref[...])

    o_ref[...] = _dot(h_ref[...], w_ref[...]).astype(o_ref.dtype)


def norm_mm(x, g, w, w_tail=None, *, tm=1024, tn=512, out_dtype=BF16):
    T, D = x.shape
    N = w.shape[1]
    tm = min(tm, T)
    assert T % tm == 0 and N % tn == 0 and tm % LANE == 0
    has_tail = w_tail is not None
    in_specs = [
        pl.BlockSpec((tm, D), lambda i, j: (i, 0), pipeline_mode=pl.Buffered(1)),
        pl.BlockSpec((1, D), lambda i, j: (0, 0)),
        pl.BlockSpec((D, tn), lambda i, j: (0, j)),
    ]
    args = [x, g.reshape(1, D).astype(F32), w]
    out_shape = [jax.ShapeDtypeStruct((T, N), out_dtype)]
    out_specs = [pl.BlockSpec((tm, tn), lambda i, j: (i, j))]
    if has_tail:
        nt = w_tail.shape[1]
        in_specs.append(pl.BlockSpec((D, nt), lambda i, j: (0, 0)))
        args.append(w_tail)
        out_shape.append(jax.ShapeDtypeStruct((T, nt), F32))
        out_specs.append(pl.BlockSpec((tm, nt), lambda i, j: (i, 0)))
    outs = pl.pallas_call(
        functools.partial(_norm_mm_kernel, tm=tm, has_tail=has_tail),
        grid=(T // tm, N // tn),
        in_specs=in_specs, out_specs=out_specs, out_shape=out_shape,
        scratch_shapes=[pltpu.VMEM((tm, D), BF16)],
        compiler_params=_cp("parallel", "arbitrary"),
        name="norm_mm",
    )(*args)
    return outs if has_tail else outs[0]


def _mm_res_kernel(*refs, n_a):
    res_ref = refs[0]
    a_refs = refs[1:1 + n_a]
    w_refs = refs[1 + n_a:1 + 2 * n_a]
    o_ref = refs[-1]
    acc = res_ref[...]
    for a, w in zip(a_refs, w_refs):
        acc = acc + _dot(a[...], w[...])
    o_ref[...] = acc


def mm_res(res, a_list, w_list, *, tm=1024, tn=512):
    T, N = res.shape
    tm = min(tm, T)
    assert T % tm == 0 and N % tn == 0
    n_a = len(a_list)
    in_specs = [pl.BlockSpec((tm, tn), lambda i, j: (i, j))]
    for a in a_list:
        in_specs.append(pl.BlockSpec((tm, a.shape[1]), lambda i, j: (i, 0)))
    for w in w_list:
        in_specs.append(pl.BlockSpec((w.shape[0], tn), lambda i, j: (0, j)))
    return pl.pallas_call(
        functools.partial(_mm_res_kernel, n_a=n_a),
        grid=(T // tm, N // tn),
        in_specs=in_specs,
        out_specs=pl.BlockSpec((tm, tn), lambda i, j: (i, j)),
        out_shape=jax.ShapeDtypeStruct((T, N), F32),
        compiler_params=_cp("parallel", "arbitrary"),
        name="mm_res",
    )(res, *a_list, *w_list)


def _t5_bucket(rel):
    nb = N_BUCKETS // 2
    max_exact = nb // 2
    ret = jnp.where(rel > 0, nb, 0)
    n = jnp.abs(rel)
    n_f = jnp.maximum(n, max_exact).astype(F32)
    large = max_exact + (jnp.log(n_f / max_exact) / math.log(MAX_DISTANCE / max_exact)
                         * (nb - max_exact)).astype(jnp.int32)
    large = jnp.minimum(large, nb - 1)
    return ret + jnp.where(n < max_exact, n, large)


def _bias_by_rel(rel_bias):
    rel = jnp.arange(-MAX_DISTANCE, MAX_DISTANCE + 1, dtype=jnp.int32)
    return rel_bias.astype(F32)[_t5_bucket(rel)]


def _win_attn_kernel(q_ref, kp_ref, kc_ref, kn_ref, vp_ref, vc_ref, vn_ref, bias_ref, sink_ref, o_ref):
    i = pl.program_id(1)
    nb = pl.num_programs(1)
    col = lax.broadcasted_iota(jnp.int32, (1, 3 * BLOCK), 1)
    invalid = ((col < BLOCK) & (i == 0)) | ((col >= 2 * BLOCK) & (i == nb - 1))
    scale = HEAD_DIM ** -0.5
    for hk in range(A_KV_HEADS):
        ks = slice(hk * HEAD_DIM, (hk + 1) * HEAD_DIM)
        qh = jnp.concatenate(
            [q_ref[:, (hk * A_GROUP + g) * HEAD_DIM:(hk * A_GROUP + g + 1) * HEAD_DIM]
             for g in range(A_GROUP)], axis=0)
        kb = jnp.concatenate([kp_ref[:, ks], kc_ref[:, ks], kn_ref[:, ks]], axis=0)
        vb = jnp.concatenate([vp_ref[:, ks], vc_ref[:, ks], vn_ref[:, ks]], axis=0)
        s = _dot_nt(qh, kb) * scale + bias_ref[hk]
        s = jnp.where(invalid, -jnp.inf, s)
        sink = sink_ref[hk]
        m = jnp.maximum(jnp.max(s, axis=-1, keepdims=True), sink)
        p = jnp.exp(s - m)
        den = jnp.sum(p, axis=-1, keepdims=True) + jnp.exp(sink - m)
        p = (p / den).astype(BF16)
        o = _dot(p, vb)
        for g in range(A_GROUP):
            h = hk * A_GROUP + g
            o_ref[:, h * HEAD_DIM:(h + 1) * HEAD_DIM] = o[g * BLOCK:(g + 1) * BLOCK].astype(o_ref.dtype)


def window_attention(z, sinks, tab):
    B, S, _ = z.shape
    nb = S // BLOCK
    q_loc = jnp.arange(BLOCK)[:, None]
    k_loc = jnp.arange(3 * BLOCK)[None, :] - BLOCK
    rel = k_loc - q_loc
    bias = jnp.where((jnp.abs(rel) <= WINDOW)[..., None],
                     tab[jnp.clip(rel, -MAX_DISTANCE, MAX_DISTANCE) + MAX_DISTANCE], -jnp.inf)
    bias = jnp.moveaxis(bias, -1, 0).reshape(A_KV_HEADS, A_GROUP * BLOCK, 3 * BLOCK)
    sink = jnp.broadcast_to(sinks.astype(F32).reshape(A_KV_HEADS, A_GROUP, 1, 1),
                            (A_KV_HEADS, A_GROUP, BLOCK, 1)).reshape(A_KV_HEADS, A_GROUP * BLOCK, 1)
    kcol = A_Q // A_KV
    vcol = (A_Q + A_KV) // A_KV
    prev = lambda b, i: (b, jnp.maximum(i - 1, 0))
    nxt = lambda b, i: (b, jnp.minimum(i + 1, nb - 1))
    return pl.pallas_call(
        _win_attn_kernel,
        grid=(B, nb),
        in_specs=[
            pl.BlockSpec((None, BLOCK, A_Q), lambda b, i: (b, i, 0)),
            pl.BlockSpec((None, BLOCK, A_KV), lambda b, i: (*prev(b, i), kcol)),
            pl.BlockSpec((None, BLOCK, A_KV), lambda b, i: (b, i, kcol)),
            pl.BlockSpec((None, BLOCK, A_KV), lambda b, i: (*nxt(b, i), kcol)),
            pl.BlockSpec((None, BLOCK, A_KV), lambda b, i: (*prev(b, i), vcol)),
            pl.BlockSpec((None, BLOCK, A_KV), lambda b, i: (b, i, vcol)),
            pl.BlockSpec((None, BLOCK, A_KV), lambda b, i: (*nxt(b, i), vcol)),
            pl.BlockSpec((A_KV_HEADS, A_GROUP * BLOCK, 3 * BLOCK), lambda b, i: (0, 0, 0)),
            pl.BlockSpec((A_KV_HEADS, A_GROUP * BLOCK, 1), lambda b, i: (0, 0, 0)),
        ],
        out_specs=pl.BlockSpec((None, BLOCK, A_Q), lambda b, i: (b, i, 0)),
        out_shape=jax.ShapeDtypeStruct((B, S, A_Q), BF16),
        compiler_params=_cp("parallel", "arbitrary"),
        name="window_attn",
    )(z, z, z, z, z, z, z, bias, sink)


CONV_TS = 256
CONV_TC = 512
CONV_HALO = 16


def _conv_kernel(prev_ref, cur_ref, next_ref, w_ref, o_ref, ext_ref):
    i = pl.program_id(1)
    j = pl.program_id(2)
    ns = pl.num_programs(1)
    ts = cur_ref.shape[0]
    pv = jnp.where(i == 0, 0.0, prev_ref[...].astype(F32))
    nx = jnp.where(i == ns - 1, 0.0, next_ref[...].astype(F32))
    ext_ref[0:CONV_HALO, :] = pv
    ext_ref[CONV_HALO:CONV_HALO + ts, :] = cur_ref[...].astype(F32)
    ext_ref[CONV_HALO + ts:, :] = nx
    half = CONV_WIDTH // 2
    acc = None
    for t in range(CONV_WIDTH):
        term = w_ref[t:t + 1, :] * ext_ref[pl.ds(CONV_HALO - half + t, ts), :]
        acc = term if acc is None else acc + term
    y = _silu(acc)
    heads_per_step = CONV_TC // B_DK
    q_steps = B_QK // CONV_TC

    def l2(scale):
        for hh in range(heads_per_step):
            seg = y[:, hh * B_DK:(hh + 1) * B_DK]
            r = lax.rsqrt(jnp.sum(seg * seg, axis=-1, keepdims=True) + EPS)
            o_ref[:, hh * B_DK:(hh + 1) * B_DK] = seg * (r * scale)

    @pl.when(j < q_steps)
    def _():
        l2(B_DK ** -0.5)

    @pl.when((j >= q_steps) & (j < 2 * q_steps))
    def _():
        l2(1.0)

    @pl.when(j >= 2 * q_steps)
    def _():
        o_ref[...] = y


def conv_qkv(z, conv_w):
    B, S, _ = z.shape
    ts = min(CONV_TS, S)
    c0 = (A_Q + 2 * A_KV) // CONV_TC
    hb = ts // CONV_HALO
    nh = S // CONV_HALO
    return pl.pallas_call(
        _conv_kernel,
        grid=(B, S // ts, B_CONV // CONV_TC),
        in_specs=[
            pl.BlockSpec((None, CONV_HALO, CONV_TC), lambda b, i, j: (b, jnp.maximum(i * hb - 1, 0), c0 + j)),
            pl.BlockSpec((None, ts, CONV_TC), lambda b, i, j: (b, i, c0 + j)),
            pl.BlockSpec((None, CONV_HALO, CONV_TC), lambda b, i, j: (b, jnp.minimum((i + 1) * hb, nh - 1), c0 + j)),
            pl.BlockSpec((CONV_WIDTH, CONV_TC), lambda b, i, j: (0, j)),
        ],
        out_specs=pl.BlockSpec((None, ts, CONV_TC), lambda b, i, j: (b, i, j)),
        out_shape=jax.ShapeDtypeStruct((B, S, B_CONV), F32),
        scratch_shapes=[pltpu.VMEM((ts + 2 * CONV_HALO, CONV_TC), F32)],
        compiler_params=_cp("parallel", "parallel", "arbitrary"),
        name="conv_qkv",
    )(z, z, z, conv_w.astype(F32))


def _bdot(a, b):
    return _dot(a.astype(DELTA_MXU), b.astype(DELTA_MXU), DELTA_PREC)


def _delta_chains(ch, eye, lmask_ref):
    n = ch[0]["q"].shape[0]
    for c in ch:
        c["decay"] = jnp.exp(jnp.where(c["causal"], c["gc_col"] - c["gc_row"], -jnp.inf))
        c["kb"] = c["k"] * c["beta"]
        c["k16"] = c["k"].astype(DELTA_MXU)
    for c in ch:
        c["m"] = jnp.where(c["strict"], _dot_nt(c["kb"].astype(DELTA_MXU), c["k16"], DELTA_PREC) * c["decay"], 0.0)
    for c in ch:
        c["inv"] = eye - c["m"] * lmask_ref[0]
    for l in range(1, lmask_ref.shape[0]):
        for c in ch:
            c["i16"] = c["inv"].astype(DELTA_MXU)
            c["p"] = _bdot(c["i16"], c["m"] * lmask_ref[l])
        for c in ch:
            c["inv"] = c["inv"] - _bdot(c["p"], c["i16"])
    for c in ch:
        c["eg"] = jnp.exp(c["gc_col"])
        c["sol"] = _bdot(c["inv"], jnp.concatenate([c["v"] * c["beta"], c["kb"] * c["eg"]], axis=1))
        c["a"] = _dot_nt(c["q"].astype(DELTA_MXU), c["k16"], DELTA_PREC) * c["decay"]
    for c in ch:
        w = c["sol"][:, B_DV:]
        c["ws"] = _bdot(jnp.concatenate([w, c["q"] * c["eg"]], axis=0), c["state"])
    out = []
    for c in ch:
        v_new = c["sol"][:, :B_DV] - c["ws"][:n]
        v16 = v_new.astype(DELTA_MXU)
        o = c["ws"][n:] + _bdot(c["a"], v16)
        k_tail = (c["k"] * jnp.exp(c["g_end"] - c["gc_col"])).astype(DELTA_MXU)
        state = c["state"] * jnp.exp(c["g_end"]) + lax.dot_general(
            k_tail, v16, (((0,), (0,)), ((), ())), precision=DELTA_PREC, preferred_element_type=F32)
        out.append((o, state))
    return out


def _delta_kernel(qf_ref, kf_ref, vf_ref, glf_ref, gtf_ref, qb_ref, kb_ref, vb_ref, glb_ref, gtb_ref,
                  arow_ref, drow_ref, acol_ref, dcol_ref, lmask_ref, of_ref, ob_ref, st_ref):
    c = pl.program_id(2)

    @pl.when(c == 0)
    def _():
        st_ref[...] = jnp.zeros_like(st_ref)

    CHUNK = DELTA_CH
    ri = lax.broadcasted_iota(jnp.int32, (CHUNK, CHUNK), 0)
    ci = lax.broadcasted_iota(jnp.int32, (CHUNK, CHUNK), 1)
    eye = (ri == ci).astype(F32)
    lower = (ri >= ci).astype(F32)
    upper = (ri <= ci).astype(F32)
    hb = DELTA_HB

    chains, sinks = [], []
    for d, (q_ref, k_ref, v_ref, gl_ref, gt_ref, o_ref) in enumerate(
            ((qf_ref, kf_ref, vf_ref, glf_ref, gtf_ref, of_ref),
             (qb_ref, kb_ref, vb_ref, glb_ref, gtb_ref, ob_ref))):
        fwd = d == 0
        gl = gl_ref[...]
        gt = gt_ref[...]
        beta_all = _sigmoid(gl)
        g_all = arow_ref[...] * _softplus(gl + drow_ref[...])
        gT_all = acol_ref[...] * _softplus(gt + dcol_ref[...])
        if fwd:
            gc_all = _dot(lower, g_all, HI)
            gcT_all = _dot(gT_all, upper, HI)
            causal, strict = ri >= ci, ri > ci
        else:
            gc_all = _dot(upper, g_all, HI)
            gcT_all = _dot(gT_all, lower, HI)
            causal, strict = ri <= ci, ri < ci
        end = CHUNK - 1 if fwd else 0
        for hh in range(hb):
            bl = d * hb + hh
            al = (2 + d) * hb + hh
            sl = slice(hh * B_DK, (hh + 1) * B_DK)
            gc_col = gc_all[:, al:al + 1]
            chains.append(dict(
                q=q_ref[:, sl], k=k_ref[:, sl], v=v_ref[:, sl], beta=beta_all[:, bl:bl + 1],
                gc_col=gc_col, gc_row=gcT_all[al:al + 1, :], g_end=gc_col[end:end + 1, :],
                state=st_ref[d * hb + hh], causal=causal, strict=strict))
            sinks.append((o_ref, sl, d * hb + hh))
    for (o, st), (o_ref, sl, si) in zip(_delta_chains(chains, eye, lmask_ref), sinks):
        o_ref[:, sl] = o
        st_ref[si] = st


def delta_rule(qkv, gl, a_log, dt_bias):
    B, S, _ = qkv.shape
    CHUNK = DELTA_CH
    n = S // CHUNK
    hb, hg = DELTA_HB, DELTA_HG
    ri = jnp.arange(CHUNK)[:, None]
    ci = jnp.arange(CHUNK)[None, :]
    lmask = jnp.stack([((ri >> (l + 1)) == (ci >> (l + 1))) & ((ri >> l) != (ci >> l))
                       for l in range(int(math.log2(CHUNK)))]).astype(F32)
    gw = 4 * hb
    g4 = gl[..., :N_GATE].reshape(B, S, 4, hg, hb)
    g4 = jnp.transpose(g4, (0, 3, 1, 2, 4)).reshape(B, hg, S, gw)
    g_rows = jnp.pad(g4, ((0, 0), (0, 0), (0, 0), (0, LANE - gw)))
    g_cols = jnp.transpose(g4.reshape(B, hg, n, CHUNK, gw), (0, 1, 2, 4, 3))
    neg_a = -jnp.exp(a_log.astype(F32)).reshape(2, hg, hb)
    dtb = dt_bias.astype(F32).reshape(2, hg, hb)
    zeros = jnp.zeros((2, hg, hb), F32)
    a4 = jnp.transpose(jnp.concatenate([zeros, neg_a], 0), (1, 0, 2)).reshape(hg, gw)
    d4 = jnp.transpose(jnp.concatenate([zeros, dtb], 0), (1, 0, 2)).reshape(hg, gw)
    arow = jnp.pad(a4, ((0, 0), (0, LANE - gw))).reshape(hg, 1, LANE)
    drow = jnp.pad(d4, ((0, 0), (0, LANE - gw))).reshape(hg, 1, LANE)
    acol = a4.reshape(hg, gw, 1)
    dcol = d4.reshape(hg, gw, 1)
    cw = hb * B_DK
    kq, kk, kv = 0, B_QK // cw, 2 * B_QK // cw

    def seq_specs(cidx):
        return [
            pl.BlockSpec((None, CHUNK, cw), lambda b, g, c: (b, cidx(c), kq + g)),
            pl.BlockSpec((None, CHUNK, cw), lambda b, g, c: (b, cidx(c), kk + g)),
            pl.BlockSpec((None, CHUNK, cw), lambda b, g, c: (b, cidx(c), kv + g)),
            pl.BlockSpec((None, None, CHUNK, LANE), lambda b, g, c: (b, g, cidx(c), 0)),
            pl.BlockSpec((None, None, None, gw, CHUNK), lambda b, g, c: (b, g, cidx(c), 0, 0)),
        ]

    fwd_idx = lambda c: c
    bwd_idx = lambda c: n - 1 - c
    par_specs = [
        pl.BlockSpec((None, 1, LANE), lambda b, g, c: (g, 0, 0)),
        pl.BlockSpec((None, 1, LANE), lambda b, g, c: (g, 0, 0)),
        pl.BlockSpec((None, gw, 1), lambda b, g, c: (g, 0, 0)),
        pl.BlockSpec((None, gw, 1), lambda b, g, c: (g, 0, 0)),
        pl.BlockSpec(lmask.shape, lambda b, g, c: (0, 0, 0)),
    ]
    o_f, o_b = pl.pallas_call(
        _delta_kernel,
        grid=(B, hg, n),
        in_specs=seq_specs(fwd_idx) + seq_specs(bwd_idx) + par_specs,
        out_specs=[
            pl.BlockSpec((None, CHUNK, cw), lambda b, g, c: (b, fwd_idx(c), g)),
            pl.BlockSpec((None, CHUNK, cw), lambda b, g, c: (b, bwd_idx(c), g)),
        ],
        out_shape=[jax.ShapeDtypeStruct((B, S, B_V), F32)] * 2,
        scratch_shapes=[pltpu.VMEM((2 * hb, B_DK, B_DV), F32)],
        compiler_params=_cp("parallel", "parallel", "arbitrary"),
        name="delta_rule",
    )(qkv, qkv, qkv, g_rows, g_cols, qkv, qkv, qkv, g_rows, g_cols, arow, drow, acol, dcol, lmask)
    return o_f, o_b


GATE_TC = 1024


def _delta_out_kernel(of_ref, ob_ref, zb_ref, g_ref, o_ref):
    for hh in range(GATE_TC // B_DV):
        sl = slice(hh * B_DV, (hh + 1) * B_DV)
        ob = of_ref[:, sl] + ob_ref[:, sl]
        y = _rms_rows(ob, g_ref[...])
        o_ref[:, sl] = (y * _silu(zb_ref[:, sl].astype(F32))).astype(o_ref.dtype)


def delta_out(o_f, o_b, z2d, onorm, *, tm=512):
    T = o_f.shape[0]
    tm = min(tm, T)
    c0 = (A_Q + 2 * A_KV + B_CONV) // GATE_TC
    return pl.pallas_call(
        _delta_out_kernel,
        grid=(T // tm, B_V // GATE_TC),
        in_specs=[
            pl.BlockSpec((tm, GATE_TC), lambda i, j: (i, j)),
            pl.BlockSpec((tm, GATE_TC), lambda i, j: (i, j)),
            pl.BlockSpec((tm, GATE_TC), lambda i, j: (i, c0 + j)),
            pl.BlockSpec((1, B_DV), lambda i, j: (0, 0)),
        ],
        out_specs=pl.BlockSpec((tm, GATE_TC), lambda i, j: (i, j)),
        out_shape=jax.ShapeDtypeStruct((T, B_V), BF16),
        compiler_params=_cp("parallel", "arbitrary"),
        name="delta_out",
    )(o_f, o_b, z2d, onorm.reshape(1, B_DV).astype(F32))


DIFF_TQ = 1024
DIFF_TK = 512


def _diff_attn_kernel(lam_ref, far_ref, q_ref, k_ref, v_ref, b_ref, g_ref, o_ref, m_ref, l_ref, acc_ref, *, out_scale, tk):
    h = pl.program_id(1)
    qi = pl.program_id(2)
    nk = k_ref.shape[0] // tk
    r = q_ref.shape[0] // tk
    base = qi * r

    m_ref[...] = jnp.full_like(m_ref, -jnp.inf)
    l_ref[...] = jnp.zeros_like(l_ref)
    acc_ref[...] = jnp.zeros_like(acc_ref)

    def step(kj, bias_idx, const):
        rows = pl.ds(pl.multiple_of(kj * tk, tk), tk)
        v = v_ref[rows, :]
        units = [(c, t) for c in range(r) for t in range(2)]

        def scores(c, t):
            sl = slice(t * C_QK_DIM, (t + 1) * C_QK_DIM)
            return _dot_nt(q_ref[c * tk:(c + 1) * tk, sl], k_ref[rows, sl])

        s_next = scores(*units[0])
        for i, (c, t) in enumerate(units):
            s = s_next
            if i + 1 < len(units):
                s_next = scores(*units[i + 1])
            qr = slice(c * tk, (c + 1) * tk)
            if bias_idx is not None:
                s = s + b_ref[bias_idx, qr, :]
            m_loc = jnp.max(s, axis=-1, keepdims=True)
            if const is not None:
                m_loc = m_loc + const
            m_old = m_ref[t, qr, :]
            m_new = jnp.maximum(m_old, m_loc)
            alpha = jnp.exp2(m_old - m_new)
            shift = m_new if const is None else m_new - const
            p = jnp.exp2(s - jnp.concatenate([shift] * (tk // LANE), axis=1))
            psum = p[:, :LANE]
            for cc in range(1, tk // LANE):
                psum = psum + p[:, cc * LANE:(cc + 1) * LANE]
            l_ref[t, qr, :] = alpha * l_ref[t, qr, :] + psum
            acc_ref[t, qr, :] = (jnp.concatenate([alpha] * (C_V_DIM // LANE), axis=1) * acc_ref[t, qr, :]
                                 + _dot(p.astype(BF16), v))
            m_ref[t, qr, :] = m_new

    def far_left(kj, carry):
        step(kj, None, far_ref[0, h])
        return carry

    def far_right(kj, carry):
        step(kj, None, far_ref[1, h])
        return carry

    lax.fori_loop(0, jnp.maximum(base - 1, 0), far_left, 0)
    for dj in range(-1, r + 1):
        @pl.when((base + dj >= 0) & (base + dj < nk))
        def _():
            step(base + dj, dj + 1, None)
    lax.fori_loop(jnp.minimum(base + r + 1, nk), nk, far_right, 0)

    lam = lam_ref[0, 0]
    r0 = 1.0 / jnp.sum(l_ref[0], axis=-1, keepdims=True)
    r1 = 1.0 / jnp.sum(l_ref[1], axis=-1, keepdims=True)
    o = acc_ref[0] * r0 - acc_ref[1] * (lam * r1)
    o_ref[...] = (_rms_rows(o, g_ref[...]) * out_scale).astype(o_ref.dtype)


def diff_attention(z, lam, tab, subln, lam_init):
    B, S, _ = z.shape
    tq, tk = min(DIFF_TQ, S), min(DIFF_TK, S)
    assert tk > MAX_DISTANCE and tq % tk == 0
    nq = S // tq
    r = tq // tk
    tab2 = tab.T * LOG2E
    period = tq + tk + 1
    u = jnp.arange(period)[None, :]
    d = jnp.arange(-1, r + 1)[:, None]
    idx = jnp.clip(d * tk + u - tq, -MAX_DISTANCE, MAX_DISTANCE) + MAX_DISTANCE
    sig = tab2[:, idx]
    skew = jnp.tile(sig, (1, 1, tq))[..., :tq * (period - 1)].reshape(C_HEADS, r + 2, tq, period - 1)
    btile = skew[..., tq:tq + tk]
    far = jnp.stack([tab2[:, 0], tab2[:, -1]])
    hw = 2 * C_QK_DIM
    return pl.pallas_call(
        functools.partial(_diff_attn_kernel, out_scale=1.0 - lam_init, tk=tk),
        grid=(B, C_HEADS, nq),
        in_specs=[
            pl.BlockSpec(memory_space=pltpu.SMEM),
            pl.BlockSpec(memory_space=pltpu.SMEM),
            pl.BlockSpec((None, tq, hw), lambda b, h, i: (b, i, h)),
            pl.BlockSpec((None, S, hw), lambda b, h, i: (b, 0, C_HEADS + h)),
            pl.BlockSpec((None, S, hw), lambda b, h, i: (b, 0, 2 * C_HEADS + h)),
            pl.BlockSpec((None, r + 2, tq, tk), lambda b, h, i: (h, 0, 0, 0)),
            pl.BlockSpec((1, C_V_DIM), lambda b, h, i: (0, 0)),
        ],
        out_specs=pl.BlockSpec((None, tq, C_V_DIM), lambda b, h, i: (b, i, h)),
        out_shape=jax.ShapeDtypeStruct((B, S, C_HEADS * C_V_DIM), BF16),
        scratch_shapes=[pltpu.VMEM((2, tq, LANE), F32), pltpu.VMEM((2, tq, LANE), F32),
                        pltpu.VMEM((2, tq, C_V_DIM), F32)],
        compiler_params=_cp("parallel", "parallel", "arbitrary"),
        name="diff_attn",
    )(lam.reshape(1, 1).astype(F32), far, z, z, z, btile, subln.reshape(1, C_V_DIM).astype(F32))


def _mem_attn_kernel(x_ref, g_ref, wq_ref, kv_ref, wo_ref, o_ref):
    x = x_ref[...]
    h = _rms_rows(x, g_ref[...]).astype(BF16)
    q = _dot(h, wq_ref[...]).astype(BF16)
    outs = []
    for hd in range(X_HEADS):
        sl = slice(hd * X_HEAD_DIM, (hd + 1) * X_HEAD_DIM)
        s = _dot_nt(q[:, sl], kv_ref[:, sl]) * (X_HEAD_DIM ** -0.5)
        m = jnp.max(s, axis=-1, keepdims=True)
        e = jnp.exp(s - m)
        p = (e / jnp.sum(e, axis=-1, keepdims=True)).astype(BF16)
        outs.append(_dot(p, kv_ref[:, X_WIDTH + hd * X_HEAD_DIM:X_WIDTH + (hd + 1) * X_HEAD_DIM]))
    o = jnp.concatenate(outs, axis=1).astype(BF16)
    o_ref[...] = x + _dot(o, wo_ref[...])


def memory_attention(x, kv, g, wq, wo, *, tm=256):
    B, S, D = x.shape
    tm = min(tm, S)
    return pl.pallas_call(
        _mem_attn_kernel,
        grid=(B, S // tm),
        in_specs=[
            pl.BlockSpec((None, tm, D), lambda b, i: (b, i, 0)),
            pl.BlockSpec((1, D), lambda b, i: (0, 0)),
            pl.BlockSpec((D, X_WIDTH), lambda b, i: (0, 0)),
            pl.BlockSpec((None, MEM_TOKENS, 2 * X_WIDTH), lambda b, i: (b, 0, 0)),
            pl.BlockSpec((X_WIDTH, D), lambda b, i: (0, 0)),
        ],
        out_specs=pl.BlockSpec((None, tm, D), lambda b, i: (b, i, 0)),
        out_shape=jax.ShapeDtypeStruct((B, S, D), F32),
        compiler_params=_cp("parallel", "arbitrary"),
        name="mem_attn",
    )(x, g.reshape(1, D).astype(F32), wq, kv, wo)


def _first_max_onehot(vals, rows):
    m = jnp.max(vals, axis=0, keepdims=True)
    idx = jnp.min(jnp.where(vals == m, rows, vals.shape[0]), axis=0, keepdims=True)
    return rows == idx


def _router_kernel(x_ref, g_ref, w_ref, b_ref, o_ref, h_ref):
    tm = x_ref.shape[0]

    def body(r, c):
        sl = pl.ds(pl.multiple_of(r * LANE, LANE), LANE)
        h_ref[sl, :] = _rms_rows(x_ref[sl, :], g_ref[...])
        return c
    lax.fori_loop(0, tm // LANE, body, 0)
    logits = _dot(h_ref[...], w_ref[...], HI)
    lt = jnp.transpose(logits)[:N_EXPERTS, :]
    scores = _sigmoid(lt)
    biased = scores + b_ref[...]
    per_group = N_EXPERTS // N_GROUPS
    rows8 = lax.broadcasted_iota(jnp.int32, (per_group, tm), 0)
    gscore = []
    for gi in range(N_GROUPS):
        blk = biased[gi * per_group:(gi + 1) * per_group]
        first = _first_max_onehot(blk, rows8)
        m1 = jnp.max(blk, axis=0, keepdims=True)
        m2 = jnp.max(jnp.where(first, -jnp.inf, blk), axis=0, keepdims=True)
        gscore.append(m1 + m2)
    gscore = jnp.concatenate(gscore, axis=0)
    growi = lax.broadcasted_iota(jnp.int32, (N_GROUPS, tm), 0)
    gsel = jnp.zeros((N_GROUPS, tm), jnp.bool_)
    work = gscore
    for _ in range(TOPK_GROUPS):
        oh = _first_max_onehot(work, growi)
        gsel = gsel | oh
        work = jnp.where(oh, -jnp.inf, work)
    gself = gsel.astype(F32)
    masked = jnp.concatenate(
        [jnp.where(gself[gi:gi + 1] > 0.0, biased[gi * per_group:(gi + 1) * per_group], -jnp.inf)
         for gi in range(N_GROUPS)], axis=0)
    erow = lax.broadcasted_iota(jnp.int32, (N_EXPERTS, tm), 0)
    esel = jnp.zeros((N_EXPERTS, tm), jnp.bool_)
    work = masked
    for _ in range(TOP_K):
        oh = _first_max_onehot(work, erow)
        esel = esel | oh
        work = jnp.where(oh, -jnp.inf, work)
    w = jnp.where(esel, scores, 0.0)
    gates = w / jnp.sum(w, axis=0, keepdims=True) * ROUTED_SCALE
    n_shared = SHARED_FF // EXPERT_FF
    full = jnp.concatenate([gates, jnp.ones((8, tm), F32), jnp.zeros((LANE - N_EXPERTS - 8, tm), F32)], axis=0)
    frow = lax.broadcasted_iota(jnp.int32, (LANE, tm), 0)
    full = jnp.where(frow < N_EXPERTS + n_shared, full, 0.0)
    o_ref[...] = jnp.transpose(full)


def route_tokens(x, g, w_router, r_bias, *, tm=256):
    T, D = x.shape
    tm = min(tm, T)
    w = jnp.pad(w_router.astype(F32), ((0, 0), (0, LANE - N_EXPERTS)))
    return pl.pallas_call(
        _router_kernel,
        grid=(T // tm,),
        in_specs=[
            pl.BlockSpec((tm, D), lambda i: (i, 0)),
            pl.BlockSpec((1, D), lambda i: (0, 0)),
            pl.BlockSpec((D, LANE), lambda i: (0, 0)),
            pl.BlockSpec((N_EXPERTS, 1), lambda i: (0, 0)),
        ],
        out_specs=pl.BlockSpec((tm, LANE), lambda i: (i, 0)),
        out_shape=jax.ShapeDtypeStruct((T, LANE), F32),
        scratch_shapes=[pltpu.VMEM((tm, D), F32)],
        compiler_params=_cp("parallel"),
        name="moe_router",
    )(x, g.reshape(1, D).astype(F32), w, r_bias.astype(F32).reshape(N_EXPERTS, 1))


def _moe_kernel(x_ref, g_ref, gate_ref, wgu_ref, wd_ref, gf_ref, o_ref, h_ref, *, tm, final_norm):
    p = pl.program_id(1)
    npair = pl.num_programs(1)

    @pl.when(p == 0)
    def _():
        def body(r, c):
            sl = pl.ds(pl.multiple_of(r * LANE, LANE), LANE)
            x = x_ref[sl, :]
            h_ref[sl, :] = _rms_rows(x, g_ref[...]).astype(BF16)
            o_ref[sl, :] = x
            return c
        lax.fori_loop(0, tm // LANE, body, 0)

    gu = _dot(h_ref[...], wgu_ref[...])
    gates = gate_ref[...]
    lane = lax.broadcasted_iota(jnp.int32, gates.shape, 1)
    F = EXPERT_FF
    hid = []
    for e in range(2):
        ge = jnp.sum(jnp.where(lane == 2 * p + e, gates, 0.0), axis=1, keepdims=True)
        hid.append(_silu(gu[:, e * F:(e + 1) * F]) * gu[:, (2 + e) * F:(3 + e) * F] * ge)
    hid = jnp.concatenate(hid, axis=1).astype(BF16)
    o_ref[...] += _dot(hid, wd_ref[...])

    if final_norm:
        @pl.when(p == npair - 1)
        def _():
            def body(r, c):
                sl = pl.ds(pl.multiple_of(r * LANE, LANE), LANE)
                o_ref[sl, :] = _rms_rows(o_ref[sl, :], gf_ref[...])
                return c
            lax.fori_loop(0, tm // LANE, body, 0)


def moe_experts(x, g, gates, w_gu, w_d, g_final, *, final_norm, tm=512):
    T, D = x.shape
    tm = min(tm, T)
    return pl.pallas_call(
        functools.partial(_moe_kernel, tm=tm, final_norm=final_norm),
        grid=(T // tm, N_PAIRS),
        in_specs=[
            pl.BlockSpec((tm, D), lambda i, p: (i, 0)),
            pl.BlockSpec((1, D), lambda i, p: (0, 0)),
            pl.BlockSpec((tm, LANE), lambda i, p: (i, 0)),
            pl.BlockSpec((None, D, 4 * EXPERT_FF), lambda i, p: (p, 0, 0)),
            pl.BlockSpec((None, 2 * EXPERT_FF, D), lambda i, p: (p, 0, 0)),
            pl.BlockSpec((1, D), lambda i, p: (0, 0)),
        ],
        out_specs=pl.BlockSpec((tm, D), lambda i, p: (i, 0)),
        out_shape=jax.ShapeDtypeStruct((T, D), F32),
        scratch_shapes=[pltpu.VMEM((tm, D), BF16)],
        compiler_params=_cp("parallel", "arbitrary"),
        name="moe_experts",
    )(x, g.reshape(1, D).astype(F32), gates, w_gu, w_d, g_final.reshape(1, D).astype(F32))


def _moe_weights(w_gate, w_up, w_down, s_gate, s_up, s_down):
    D = w_gate.shape[1]
    ns = SHARED_FF // EXPERT_FF
    sg = jnp.transpose(s_gate.reshape(D, ns, EXPERT_FF), (1, 0, 2))
    su = jnp.transpose(s_up.reshape(D, ns, EXPERT_FF), (1, 0, 2))
    wg = jnp.concatenate([w_gate, sg], axis=0).astype(BF16).reshape(N_PAIRS, 2, D, EXPERT_FF)
    wu = jnp.concatenate([w_up, su], axis=0).astype(BF16).reshape(N_PAIRS, 2, D, EXPERT_FF)
    w_gu = jnp.concatenate([wg[:, 0], wg[:, 1], wu[:, 0], wu[:, 1]], axis=-1)
    w_d = jnp.concatenate([w_down, s_down.reshape(ns, EXPERT_FF, D)], axis=0).astype(BF16)
    return w_gu, w_d.reshape(N_PAIRS, 2 * EXPERT_FF, D)


def kernel(x_prompt, x_sample, mem_prompt, mem_sample, rel_bias, norm_mix, norm_mem, norm_memkv, norm_ffn, norm_final, ev_w_in, ev_sinks, ev_conv, ev_a_log, ev_dt_bias, ev_onorm, ev_w_out, od_w_in, od_lambda, od_subln, od_w_out, mx_wq, mx_wkv, mx_wo, moe_router, moe_bias, moe_w_gate, moe_w_up, moe_w_down, sh_w_gate, sh_w_up, sh_w_down):
    tab = _bias_by_rel(rel_bias)

    layers = []
    for l in range(DEPTH):
        i = l // 2
        lw = {}
        if l % 2 == 0:
            w_in = ev_w_in[i]
            lw["w_in"] = w_in[:, :EVEN_MAIN].astype(BF16)
            lw["w_gate_tail"] = jnp.pad(w_in[:, EVEN_MAIN:], ((0, 0), (0, LANE - N_GATE))).astype(BF16)
            lw["w_out_a"] = ev_w_out[i][:A_Q].astype(BF16)
            lw["w_out_b"] = ev_w_out[i][A_Q:].astype(BF16)
        else:
            w_in = od_w_in[i]
            lw["w_in"] = jnp.concatenate([w_in[:, :C_QK] * (C_QK_DIM ** -0.5 * LOG2E), w_in[:, C_QK:]],
                                         axis=1).astype(BF16)
            lw["w_out"] = od_w_out[i].astype(BF16)
            lp = od_lambda[i].astype(F32)
            lam_init = 0.8 - 0.6 * math.exp(-0.3 * l)
            lw["lam_init"] = lam_init
            lw["lam"] = jnp.exp(jnp.sum(lp[0] * lp[1])) - jnp.exp(jnp.sum(lp[2] * lp[3])) + lam_init
        lw["wq"] = mx_wq[l].astype(BF16)
        lw["wkv"] = mx_wkv[l].astype(BF16)
        lw["wo"] = mx_wo[l].astype(BF16)
        lw["w_gu"], lw["w_d"] = _moe_weights(moe_w_gate[l], moe_w_up[l], moe_w_down[l],
                                             sh_w_gate[l], sh_w_up[l], sh_w_down[l])
        layers.append(lw)

    def run(x3, mem):
        B, S, D = x3.shape
        T = B * S
        x = x3.reshape(T, D)
        mem2 = mem.reshape(B * MEM_TOKENS, D)
        for l in range(DEPTH):
            lw = layers[l]
            i = l // 2
            if l % 2 == 0:
                z, gl = norm_mm(x, norm_mix[l], lw["w_in"], lw["w_gate_tail"])
                z3 = z.reshape(B, S, EVEN_MAIN)
                out_a = window_attention(z3, ev_sinks[i], tab)
                qkv = conv_qkv(z3, ev_conv[i])
                o_f, o_b = delta_rule(qkv, gl.reshape(B, S, LANE), ev_a_log[i], ev_dt_bias[i])
                out_b = delta_out(o_f.reshape(T, B_V), o_b.reshape(T, B_V), z, ev_onorm[i])
                x = mm_res(x, [out_a.reshape(T, A_Q), out_b], [lw["w_out_a"], lw["w_out_b"]])
            else:
                z = norm_mm(x, norm_mix[l], lw["w_in"])
                o = diff_attention(z.reshape(B, S, 3 * C_QK), lw["lam"], tab, od_subln[i], lw["lam_init"])
                x = mm_res(x, [o.reshape(T, C_HEADS * C_V_DIM)], [lw["w_out"]])
            kv = norm_mm(mem2, norm_memkv[l], lw["wkv"])
            x = memory_attention(x.reshape(B, S, D), kv.reshape(B, MEM_TOKENS, 2 * X_WIDTH),
                                 norm_mem[l], lw["wq"], lw["wo"]).reshape(T, D)
            gates = route_tokens(x, norm_ffn[l], moe_router[l], moe_bias[l])
            x = moe_experts(x, norm_ffn[l], gates, lw["w_gu"], lw["w_d"], norm_final,
                            final_norm=(l == DEPTH - 1))
        return x.reshape(B, S, D)

    return (run(x_prompt, mem_prompt), run(x_sample, mem_sample))
```

```python
import functools
import math

import jax
import jax.numpy as jnp
from jax import lax
from jax.experimental import pallas as pl
from jax.experimental.pallas import tpu as pltpu

F32 = jnp.float32
BF16 = jnp.bfloat16

D_MODEL = 4096
DEPTH = 2
HEAD_DIM = 128
BLOCK = 128
WINDOW = 128
A_HEADS = 16
A_KV_HEADS = 4
A_GROUP = A_HEADS // A_KV_HEADS
B_HEADS = 16
B_DK = 128
B_DV = 128
CONV_WIDTH = 5
CHUNK = 64
C_HEADS = 16
C_QK_DIM = 128
C_V_DIM = 2 * C_QK_DIM
N_BUCKETS = 32
MAX_DISTANCE = 128
MEM_TOKENS = 256
X_HEADS = 4
X_HEAD_DIM = 128
N_EXPERTS = 64
TOP_K = 8
N_GROUPS = 8
TOPK_GROUPS = 4
EXPERT_FF = 128
SHARED_FF = 512
ROUTED_SCALE = 2.5
EPS = 1e-6

A_Q = A_HEADS * HEAD_DIM
A_KV = A_KV_HEADS * HEAD_DIM
B_QK = B_HEADS * B_DK
B_V = B_HEADS * B_DV
B_CONV = 2 * B_QK + B_V
EVEN_MAIN = A_Q + 2 * A_KV + B_CONV + B_V
N_GATE = 4 * B_HEADS
C_QK = 2 * C_HEADS * C_QK_DIM
X_WIDTH = X_HEADS * X_HEAD_DIM

LANE = 128
VMEM_LIMIT = 56 * 1024 * 1024
HI = lax.Precision.HIGHEST
LOG2E = math.log2(math.e)

DELTA_HB = 4
DELTA_HG = B_HEADS // DELTA_HB
DELTA_CH = 256
DELTA_MXU = BF16
DELTA_PREC = None
MOE_EPS = 4
N_PAIRS = (N_EXPERTS + SHARED_FF // EXPERT_FF) // MOE_EPS


def _cp(*sem):
    return pltpu.CompilerParams(dimension_semantics=sem, vmem_limit_bytes=VMEM_LIMIT)


def _dot(a, b, prec=None):
    return lax.dot_general(a, b, (((1,), (0,)), ((), ())), precision=prec,
                           preferred_element_type=F32)


def _dot_nt(a, b, prec=None):
    return lax.dot_general(a, b, (((1,), (1,)), ((), ())), precision=prec,
                           preferred_element_type=F32)


def _sigmoid(x):
    return 1.0 / (1.0 + jnp.exp(-x))


def _silu(x):
    return x * _sigmoid(x)


def _softplus(x):
    return jnp.maximum(x, 0.0) + jnp.log(1.0 + jnp.exp(-jnp.abs(x)))


def _rms_rows(x, g):
    ms = jnp.mean(x * x, axis=-1, keepdims=True)
    return x * lax.rsqrt(ms + EPS) * g


def _norm_mm_kernel(x_ref, g_ref, w_ref, *rest, tm, has_tail):
    if has_tail:
        wt_ref, o_ref, ot_ref, h_ref = rest
    else:
        o_ref, h_ref = rest

    @pl.when(pl.program_id(1) == 0)
    def _():
        def body(r, c):
            sl = pl.ds(pl.multiple_of(r * LANE, LANE), LANE)
            h_ref[sl, :] = _rms_rows(x_ref[sl, :], g_ref[...]).astype(BF16)
            return c
        lax.fori_loop(0, tm // LANE, body, 0)
        if has_tail:
            ot_ref[...] = _dot(h_ref[...], wt_ref[...])

    o_ref[...] = _dot(h_ref[...], w_ref[...]).astype(o_ref.dtype)


def norm_mm(x, g, w, w_tail=None, *, tm=1024, tn=512, out_dtype=BF16):
    T, D = x.shape
    N = w.shape[1]
    tm = min(tm, T)
    assert T % tm == 0 and N % tn == 0 and tm % LANE == 0
    has_tail = w_tail is not None
    in_specs = [
        pl.BlockSpec((tm, D), lambda i, j: (i, 0), pipeline_mode=pl.Buffered(1)),
        pl.BlockSpec((1, D), lambda i, j: (0, 0)),
        pl.BlockSpec((D, tn), lambda i, j: (0, j)),
    ]
    args = [x, g.reshape(1, D).astype(F32), w]
    out_shape = [jax.ShapeDtypeStruct((T, N), out_dtype)]
    out_specs = [pl.BlockSpec((tm, tn), lambda i, j: (i, j))]
    if has_tail:
        nt = w_tail.shape[1]
        in_specs.append(pl.BlockSpec((D, nt), lambda i, j: (0, 0)))
        args.append(w_tail)
        out_shape.append(jax.ShapeDtypeStruct((T, nt), F32))
        out_specs.append(pl.BlockSpec((tm, nt), lambda i, j: (i, 0)))
    outs = pl.pallas_call(
        functools.partial(_norm_mm_kernel, tm=tm, has_tail=has_tail),
        grid=(T // tm, N // tn),
        in_specs=in_specs, out_specs=out_specs, out_shape=out_shape,
        scratch_shapes=[pltpu.VMEM((tm, D), BF16)],
        compiler_params=_cp("parallel", "arbitrary"),
        name="norm_mm",
    )(*args)
    return outs if has_tail else outs[0]


def _mm_res_kernel(*refs, n_a):
    res_ref = refs[0]
    a_refs = refs[1:1 + n_a]
    w_refs = refs[1 + n_a:1 + 2 * n_a]
    o_ref = refs[-1]
    acc = res_ref[...]
    for a, w in zip(a_refs, w_refs):
        acc = acc + _dot(a[...], w[...])
    o_ref[...] = acc


def mm_res(res, a_list, w_list, *, tm=1024, tn=512):
    T, N = res.shape
    tm = min(tm, T)
    assert T % tm == 0 and N % tn == 0
    n_a = len(a_list)
    in_specs = [pl.BlockSpec((tm, tn), lambda i, j: (i, j))]
    for a in a_list:
        in_specs.append(pl.BlockSpec((tm, a.shape[1]), lambda i, j: (i, 0)))
    for w in w_list:
        in_specs.append(pl.BlockSpec((w.shape[0], tn), lambda i, j: (0, j)))
    return pl.pallas_call(
        functools.partial(_mm_res_kernel, n_a=n_a),
        grid=(T // tm, N // tn),
        in_specs=in_specs,
        out_specs=pl.BlockSpec((tm, tn), lambda i, j: (i, j)),
        out_shape=jax.ShapeDtypeStruct((T, N), F32),
        compiler_params=_cp("parallel", "arbitrary"),
        name="mm_res",
    )(res, *a_list, *w_list)


def _t5_bucket(rel):
    nb = N_BUCKETS // 2
    max_exact = nb // 2
    ret = jnp.where(rel > 0, nb, 0)
    n = jnp.abs(rel)
    n_f = jnp.maximum(n, max_exact).astype(F32)
    large = max_exact + (jnp.log(n_f / max_exact) / math.log(MAX_DISTANCE / max_exact)
                         * (nb - max_exact)).astype(jnp.int32)
    large = jnp.minimum(large, nb - 1)
    return ret + jnp.where(n < max_exact, n, large)


def _bias_by_rel(rel_bias):
    rel = jnp.arange(-MAX_DISTANCE, MAX_DISTANCE + 1, dtype=jnp.int32)
    return rel_bias.astype(F32)[_t5_bucket(rel)]


def _win_attn_kernel(q_ref, kp_ref, kc_ref, kn_ref, vp_ref, vc_ref, vn_ref, bias_ref, sink_ref, o_ref):
    i = pl.program_id(1)
    nb = pl.num_programs(1)
    col = lax.broadcasted_iota(jnp.int32, (1, 3 * BLOCK), 1)
    invalid = ((col < BLOCK) & (i == 0)) | ((col >= 2 * BLOCK) & (i == nb - 1))
    scale = HEAD_DIM ** -0.5
    for hk in range(A_KV_HEADS):
        ks = slice(hk * HEAD_DIM, (hk + 1) * HEAD_DIM)
        qh = jnp.concatenate(
            [q_ref[:, (hk * A_GROUP + g) * HEAD_DIM:(hk * A_GROUP + g + 1) * HEAD_DIM]
             for g in range(A_GROUP)], axis=0)
        kb = jnp.concatenate([kp_ref[:, ks], kc_ref[:, ks], kn_ref[:, ks]], axis=0)
        vb = jnp.concatenate([vp_ref[:, ks], vc_ref[:, ks], vn_ref[:, ks]], axis=0)
        s = _dot_nt(qh, kb) * scale + bias_ref[hk]
        s = jnp.where(invalid, -jnp.inf, s)
        sink = sink_ref[hk]
        m = jnp.maximum(jnp.max(s, axis=-1, keepdims=True), sink)
        p = jnp.exp(s - m)
        den = jnp.sum(p, axis=-1, keepdims=True) + jnp.exp(sink - m)
        p = (p / den).astype(BF16)
        o = _dot(p, vb)
        for g in range(A_GROUP):
            h = hk * A_GROUP + g
            o_ref[:, h * HEAD_DIM:(h + 1) * HEAD_DIM] = o[g * BLOCK:(g + 1) * BLOCK].astype(o_ref.dtype)


def window_attention(z, sinks, tab):
    B, S, _ = z.shape
    nb = S // BLOCK
    q_loc = jnp.arange(BLOCK)[:, None]
    k_loc = jnp.arange(3 * BLOCK)[None, :] - BLOCK
    rel = k_loc - q_loc
    bias = jnp.where((jnp.abs(rel) <= WINDOW)[..., None],
                     tab[jnp.clip(rel, -MAX_DISTANCE, MAX_DISTANCE) + MAX_DISTANCE], -jnp.inf)
    bias = jnp.moveaxis(bias, -1, 0).reshape(A_KV_HEADS, A_GROUP * BLOCK, 3 * BLOCK)
    sink = jnp.broadcast_to(sinks.astype(F32).reshape(A_KV_HEADS, A_GROUP, 1, 1),
                            (A_KV_HEADS, A_GROUP, BLOCK, 1)).reshape(A_KV_HEADS, A_GROUP * BLOCK, 1)
    kcol = A_Q // A_KV
    vcol = (A_Q + A_KV) // A_KV
    prev = lambda b, i: (b, jnp.maximum(i - 1, 0))
    nxt = lambda b, i: (b, jnp.minimum(i + 1, nb - 1))
    return pl.pallas_call(
        _win_attn_kernel,
        grid=(B, nb),
        in_specs=[
            pl.BlockSpec((None, BLOCK, A_Q), lambda b, i: (b, i, 0)),
            pl.BlockSpec((None, BLOCK, A_KV), lambda b, i: (*prev(b, i), kcol)),
            pl.BlockSpec((None, BLOCK, A_KV), lambda b, i: (b, i, kcol)),
            pl.BlockSpec((None, BLOCK, A_KV), lambda b, i: (*nxt(b, i), kcol)),
            pl.BlockSpec((None, BLOCK, A_KV), lambda b, i: (*prev(b, i), vcol)),
            pl.BlockSpec((None, BLOCK, A_KV), lambda b, i: (b, i, vcol)),
            pl.BlockSpec((None, BLOCK, A_KV), lambda b, i: (*nxt(b, i), vcol)),
            pl.BlockSpec((A_KV_HEADS, A_GROUP * BLOCK, 3 * BLOCK), lambda b, i: (0, 0, 0)),
            pl.BlockSpec((A_KV_HEADS, A_GROUP * BLOCK, 1), lambda b, i: (0, 0, 0)),
        ],
        out_specs=pl.BlockSpec((None, BLOCK, A_Q), lambda b, i: (b, i, 0)),
        out_shape=jax.ShapeDtypeStruct((B, S, A_Q), BF16),
        compiler_params=_cp("parallel", "arbitrary"),
        name="window_attn",
    )(z, z, z, z, z, z, z, bias, sink)


CONV_TS = 256
CONV_TC = 512
CONV_HALO = 16


def _conv_kernel(prev_ref, cur_ref, next_ref, w_ref, o_ref, ext_ref):
    i = pl.program_id(1)
    j = pl.program_id(2)
    ns = pl.num_programs(1)
    ts = cur_ref.shape[0]
    pv = jnp.where(i == 0, 0.0, prev_ref[...].astype(F32))
    nx = jnp.where(i == ns - 1, 0.0, next_ref[...].astype(F32))
    ext_ref[0:CONV_HALO, :] = pv
    ext_ref[CONV_HALO:CONV_HALO + ts, :] = cur_ref[...].astype(F32)
    ext_ref[CONV_HALO + ts:, :] = nx
    half = CONV_WIDTH // 2
    acc = None
    for t in range(CONV_WIDTH):
        term = w_ref[t:t + 1, :] * ext_ref[pl.ds(CONV_HALO - half + t, ts), :]
        acc = term if acc is None else acc + term
    y = _silu(acc)
    heads_per_step = CONV_TC // B_DK
    q_steps = B_QK // CONV_TC

    def l2(scale):
        for hh in range(heads_per_step):
            seg = y[:, hh * B_DK:(hh + 1) * B_DK]
            r = lax.rsqrt(jnp.sum(seg * seg, axis=-1, keepdims=True) + EPS)
            o_ref[:, hh * B_DK:(hh + 1) * B_DK] = seg * (r * scale)

    @pl.when(j < q_steps)
    def _():
        l2(B_DK ** -0.5)

    @pl.when((j >= q_steps) & (j < 2 * q_steps))
    def _():
        l2(1.0)

    @pl.when(j >= 2 * q_steps)
    def _():
        o_ref[...] = y


def conv_qkv(z, conv_w):
    B, S, _ = z.shape
    ts = min(CONV_TS, S)
    c0 = (A_Q + 2 * A_KV) // CONV_TC
    hb = ts // CONV_HALO
    nh = S // CONV_HALO
    return pl.pallas_call(
        _conv_kernel,
        grid=(B, S // ts, B_CONV // CONV_TC),
        in_specs=[
            pl.BlockSpec((None, CONV_HALO, CONV_TC), lambda b, i, j: (b, jnp.maximum(i * hb - 1, 0), c0 + j)),
            pl.BlockSpec((None, ts, CONV_TC), lambda b, i, j: (b, i, c0 + j)),
            pl.BlockSpec((None, CONV_HALO, CONV_TC), lambda b, i, j: (b, jnp.minimum((i + 1) * hb, nh - 1), c0 + j)),
            pl.BlockSpec((CONV_WIDTH, CONV_TC), lambda b, i, j: (0, j)),
        ],
        out_specs=pl.BlockSpec((None, ts, CONV_TC), lambda b, i, j: (b, i, j)),
        out_shape=jax.ShapeDtypeStruct((B, S, B_CONV), F32),
        scratch_shapes=[pltpu.VMEM((ts + 2 * CONV_HALO, CONV_TC), F32)],
        compiler_params=_cp("parallel", "parallel", "arbitrary"),
        name="conv_qkv",
    )(z, z, z, conv_w.astype(F32))


def _bdot(a, b):
    return _dot(a.astype(DELTA_MXU), b.astype(DELTA_MXU), DELTA_PREC)


def _delta_chains(ch, eye, lmask_ref):
    n = ch[0]["q"].shape[0]
    for c in ch:
        c["decay"] = jnp.exp(jnp.where(c["causal"], c["gc_col"] - c["gc_row"], -jnp.inf))
        c["kb"] = c["k"] * c["beta"]
        c["k16"] = c["k"].astype(DELTA_MXU)
    for c in ch:
        c["m"] = jnp.where(c["strict"], _dot_nt(c["kb"].astype(DELTA_MXU), c["k16"], DELTA_PREC) * c["decay"], 0.0)
    for c in ch:
        c["inv"] = eye - c["m"] * lmask_ref[0]
    for l in range(1, lmask_ref.shape[0]):
        for c in ch:
            c["i16"] = c["inv"].astype(DELTA_MXU)
            c["p"] = _bdot(c["i16"], c["m"] * lmask_ref[l])
        for c in ch:
            c["inv"] = c["inv"] - _bdot(c["p"], c["i16"])
    for c in ch:
        c["eg"] = jnp.exp(c["gc_col"])
        c["sol"] = _bdot(c["inv"], jnp.concatenate([c["v"] * c["beta"], c["kb"] * c["eg"]], axis=1))
        c["a"] = _dot_nt(c["q"].astype(DELTA_MXU), c["k16"], DELTA_PREC) * c["decay"]
    for c in ch:
        w = c["sol"][:, B_DV:]
        c["ws"] = _bdot(jnp.concatenate([w, c["q"] * c["eg"]], axis=0), c["state"])
    out = []
    for c in ch:
        v_new = c["sol"][:, :B_DV] - c["ws"][:n]
        v16 = v_new.astype(DELTA_MXU)
        o = c["ws"][n:] + _bdot(c["a"], v16)
        k_tail = (c["k"] * jnp.exp(c["g_end"] - c["gc_col"])).astype(DELTA_MXU)
        state = c["state"] * jnp.exp(c["g_end"]) + lax.dot_general(
            k_tail, v16, (((0,), (0,)), ((), ())), precision=DELTA_PREC, preferred_element_type=F32)
        out.append((o, state))
    return out


def _delta_kernel(qf_ref, kf_ref, vf_ref, glf_ref, gtf_ref, qb_ref, kb_ref, vb_ref, glb_ref, gtb_ref,
                  arow_ref, drow_ref, acol_ref, dcol_ref, lmask_ref, of_ref, ob_ref, st_ref):
    c = pl.program_id(2)

    @pl.when(c == 0)
    def _():
        st_ref[...] = jnp.zeros_like(st_ref)

    CHUNK = DELTA_CH
    ri = lax.broadcasted_iota(jnp.int32, (CHUNK, CHUNK), 0)
    ci = lax.broadcasted_iota(jnp.int32, (CHUNK, CHUNK), 1)
    eye = (ri == ci).astype(F32)
    lower = (ri >= ci).astype(F32)
    upper = (ri <= ci).astype(F32)
    hb = DELTA_HB

    chains, sinks = [], []
    for d, (q_ref, k_ref, v_ref, gl_ref, gt_ref, o_ref) in enumerate(
            ((qf_ref, kf_ref, vf_ref, glf_ref, gtf_ref, of_ref),
             (qb_ref, kb_ref, vb_ref, glb_ref, gtb_ref, ob_ref))):
        fwd = d == 0
        gl = gl_ref[...]
        gt = gt_ref[...]
        beta_all = _sigmoid(gl)
        g_all = arow_ref[...] * _softplus(gl + drow_ref[...])
        gT_all = acol_ref[...] * _softplus(gt + dcol_ref[...])
        if fwd:
            gc_all = _dot(lower, g_all, HI)
            gcT_all = _dot(gT_all, upper, HI)
            causal, strict = ri >= ci, ri > ci
        else:
            gc_all = _dot(upper, g_all, HI)
            gcT_all = _dot(gT_all, lower, HI)
            causal, strict = ri <= ci, ri < ci
        end = CHUNK - 1 if fwd else 0
        for hh in range(hb):
            bl = d * hb + hh
            al = (2 + d) * hb + hh
            sl = slice(hh * B_DK, (hh + 1) * B_DK)
            gc_col = gc_all[:, al:al + 1]
            chains.append(dict(
                q=q_ref[:, sl], k=k_ref[:, sl], v=v_ref[:, sl], beta=beta_all[:, bl:bl + 1],
                gc_col=gc_col, gc_row=gcT_all[al:al + 1, :], g_end=gc_col[end:end + 1, :],
                state=st_ref[d * hb + hh], causal=causal, strict=strict))
            sinks.append((o_ref, sl, d * hb + hh))
    for (o, st), (o_ref, sl, si) in zip(_delta_chains(chains, eye, lmask_ref), sinks):
        o_ref[:, sl] = o
        st_ref[si] = st


def delta_rule(qkv, gl, a_log, dt_bias):
    B, S, _ = qkv.shape
    CHUNK = DELTA_CH
    n = S // CHUNK
    hb, hg = DELTA_HB, DELTA_HG
    ri = jnp.arange(CHUNK)[:, None]
    ci = jnp.arange(CHUNK)[None, :]
    lmask = jnp.stack([((ri >> (l + 1)) == (ci >> (l + 1))) & ((ri >> l) != (ci >> l))
                       for l in range(int(math.log2(CHUNK)))]).astype(F32)
    gw = 4 * hb
    g4 = gl[..., :N_GATE].reshape(B, S, 4, hg, hb)
    g4 = jnp.transpose(g4, (0, 3, 1, 2, 4)).reshape(B, hg, S, gw)
    g_rows = jnp.pad(g4, ((0, 0), (0, 0), (0, 0), (0, LANE - gw)))
    g_cols = jnp.transpose(g4.reshape(B, hg, n, CHUNK, gw), (0, 1, 2, 4, 3))
    neg_a = -jnp.exp(a_log.astype(F32)).reshape(2, hg, hb)
    dtb = dt_bias.astype(F32).reshape(2, hg, hb)
    zeros = jnp.zeros((2, hg, hb), F32)
    a4 = jnp.transpose(jnp.concatenate([zeros, neg_a], 0), (1, 0, 2)).reshape(hg, gw)
    d4 = jnp.transpose(jnp.concatenate([zeros, dtb], 0), (1, 0, 2)).reshape(hg, gw)
    arow = jnp.pad(a4, ((0, 0), (0, LANE - gw))).reshape(hg, 1, LANE)
    drow = jnp.pad(d4, ((0, 0), (0, LANE - gw))).reshape(hg, 1, LANE)
    acol = a4.reshape(hg, gw, 1)
    dcol = d4.reshape(hg, gw, 1)
    cw = hb * B_DK
    kq, kk, kv = 0, B_QK // cw, 2 * B_QK // cw

    def seq_specs(cidx):
        return [
            pl.BlockSpec((None, CHUNK, cw), lambda b, g, c: (b, cidx(c), kq + g)),
            pl.BlockSpec((None, CHUNK, cw), lambda b, g, c: (b, cidx(c), kk + g)),
            pl.BlockSpec((None, CHUNK, cw), lambda b, g, c: (b, cidx(c), kv + g)),
            pl.BlockSpec((None, None, CHUNK, LANE), lambda b, g, c: (b, g, cidx(c), 0)),
            pl.BlockSpec((None, None, None, gw, CHUNK), lambda b, g, c: (b, g, cidx(c), 0, 0)),
        ]

    fwd_idx = lambda c: c
    bwd_idx = lambda c: n - 1 - c
    par_specs = [
        pl.BlockSpec((None, 1, LANE), lambda b, g, c: (g, 0, 0)),
        pl.BlockSpec((None, 1, LANE), lambda b, g, c: (g, 0, 0)),
        pl.BlockSpec((None, gw, 1), lambda b, g, c: (g, 0, 0)),
        pl.BlockSpec((None, gw, 1), lambda b, g, c: (g, 0, 0)),
        pl.BlockSpec(lmask.shape, lambda b, g, c: (0, 0, 0)),
    ]
    o_f, o_b = pl.pallas_call(
        _delta_kernel,
        grid=(B, hg, n),
        in_specs=seq_specs(fwd_idx) + seq_specs(bwd_idx) + par_specs,
        out_specs=[
            pl.BlockSpec((None, CHUNK, cw), lambda b, g, c: (b, fwd_idx(c), g)),
            pl.BlockSpec((None, CHUNK, cw), lambda b, g, c: (b, bwd_idx(c), g)),
        ],
        out_shape=[jax.ShapeDtypeStruct((B, S, B_V), F32)] * 2,
        scratch_shapes=[pltpu.VMEM((2 * hb, B_DK, B_DV), F32)],
        compiler_params=_cp("parallel", "parallel", "arbitrary"),
        name="delta_rule",
    )(qkv, qkv, qkv, g_rows, g_cols, qkv, qkv, qkv, g_rows, g_cols, arow, drow, acol, dcol, lmask)
    return o_f, o_b


GATE_TC = 1024


def _delta_out_kernel(of_ref, ob_ref, zb_ref, g_ref, o_ref):
    for hh in range(GATE_TC // B_DV):
        sl = slice(hh * B_DV, (hh + 1) * B_DV)
        ob = of_ref[:, sl] + ob_ref[:, sl]
        y = _rms_rows(ob, g_ref[...])
        o_ref[:, sl] = (y * _silu(zb_ref[:, sl].astype(F32))).astype(o_ref.dtype)


def delta_out(o_f, o_b, z2d, onorm, *, tm=512):
    T = o_f.shape[0]
    tm = min(tm, T)
    c0 = (A_Q + 2 * A_KV + B_CONV) // GATE_TC
    return pl.pallas_call(
        _delta_out_kernel,
        grid=(T // tm, B_V // GATE_TC),
        in_specs=[
            pl.BlockSpec((tm, GATE_TC), lambda i, j: (i, j)),
            pl.BlockSpec((tm, GATE_TC), lambda i, j: (i, j)),
            pl.BlockSpec((tm, GATE_TC), lambda i, j: (i, c0 + j)),
            pl.BlockSpec((1, B_DV), lambda i, j: (0, 0)),
        ],
        out_specs=pl.BlockSpec((tm, GATE_TC), lambda i, j: (i, j)),
        out_shape=jax.ShapeDtypeStruct((T, B_V), BF16),
        compiler_params=_cp("parallel", "arbitrary"),
        name="delta_out",
    )(o_f, o_b, z2d, onorm.reshape(1, B_DV).astype(F32))


DIFF_TQ = 1024
DIFF_TK = 512


def _diff_attn_kernel(lam_ref, far_ref, q_ref, k_ref, v_ref, b_ref, g_ref, o_ref, m_ref, l_ref, acc_ref, *, out_scale, tk):
    h = pl.program_id(1)
    qi = pl.program_id(2)
    nk = k_ref.shape[0] // tk
    r = q_ref.shape[0] // tk
    base = qi * r

    m_ref[...] = jnp.full_like(m_ref, -jnp.inf)
    l_ref[...] = jnp.zeros_like(l_ref)
    acc_ref[...] = jnp.zeros_like(acc_ref)

    def step(kj, near_dj, far_const):
        rows = pl.ds(pl.multiple_of(kj * tk, tk), tk)
        v = v_ref[rows, :]
        units = [(c, t) for c in range(r) for t in range(2)]

        def scores(c, t):
            sl = slice(t * C_QK_DIM, (t + 1) * C_QK_DIM)
            return _dot_nt(q_ref[c * tk:(c + 1) * tk, sl], k_ref[rows, sl])

        s_next = scores(*units[0])
        for i, (c, t) in enumerate(units):
            s = s_next
            if i + 1 < len(units):
                s_next = scores(*units[i + 1])
            qr = slice(c * tk, (c + 1) * tk)
            off = None if near_dj is None else near_dj - c
            if off is None:
                const = far_const
            elif abs(off) <= 1:
                const = None
                s = s + b_ref[off + 1]
            else:
                const = far_ref[0 if off < 0 else 1, h]
            m_loc = jnp.max(s, axis=-1, keepdims=True)
            if const is not None:
                m_loc = m_loc + const
            m_old = m_ref[t, qr, :]
            m_new = jnp.maximum(m_old, m_loc)
            alpha = jnp.exp2(m_old - m_new)
            shift = m_new if const is None else m_new - const
            p = jnp.exp2(s - jnp.concatenate([shift] * (tk // LANE), axis=1))
            psum = p[:, :LANE]
            for cc in range(1, tk // LANE):
                psum = psum + p[:, cc * LANE:(cc + 1) * LANE]
            l_ref[t, qr, :] = alpha * l_ref[t, qr, :] + psum
            acc_ref[t, qr, :] = (jnp.concatenate([alpha] * (C_V_DIM // LANE), axis=1) * acc_ref[t, qr, :]
                                 + _dot(p.astype(BF16), v))
            m_ref[t, qr, :] = m_new

    def far_left(kj, carry):
        step(kj, None, far_ref[0, h])
        return carry

    def far_right(kj, carry):
        step(kj, None, far_ref[1, h])
        return carry

    lax.fori_loop(0, jnp.maximum(base - 1, 0), far_left, 0)
    for dj in range(-1, r + 1):
        @pl.when((base + dj >= 0) & (base + dj < nk))
        def _():
            step(base + dj, dj, None)
    lax.fori_loop(jnp.minimum(base + r + 1, nk), nk, far_right, 0)

    lam = lam_ref[0, 0]
    r0 = 1.0 / jnp.sum(l_ref[0], axis=-1, keepdims=True)
    r1 = 1.0 / jnp.sum(l_ref[1], axis=-1, keepdims=True)
    o = acc_ref[0] * r0 - acc_ref[1] * (lam * r1)
    o_ref[...] = (_rms_rows(o, g_ref[...]) * out_scale).astype(o_ref.dtype)


def diff_attention(z, lam, tab, subln, lam_init):
    B, S, _ = z.shape
    tq, tk = min(DIFF_TQ, S), min(DIFF_TK, S)
    assert tk > MAX_DISTANCE and tq % tk == 0
    nq = S // tq
    r = tq // tk
    tab2 = tab.T * LOG2E
    nd = 3
    period = 2 * tk + 1
    u = jnp.arange(period)[None, :]
    d = jnp.arange(-1, 2)[:, None]
    idx = jnp.clip(d * tk + u - tk, -MAX_DISTANCE, MAX_DISTANCE) + MAX_DISTANCE
    sig = tab2[:, idx]
    skew = jnp.tile(sig, (1, 1, tk))[..., :tk * (period - 1)].reshape(C_HEADS, nd, tk, period - 1)
    btile = skew[..., tk:2 * tk]
    far = jnp.stack([tab2[:, 0], tab2[:, -1]])
    hw = 2 * C_QK_DIM
    return pl.pallas_call(
        functools.partial(_diff_attn_kernel, out_scale=1.0 - lam_init, tk=tk),
        grid=(B, C_HEADS, nq),
        in_specs=[
            pl.BlockSpec(memory_space=pltpu.SMEM),
            pl.BlockSpec(memory_space=pltpu.SMEM),
            pl.BlockSpec((None, tq, hw), lambda b, h, i: (b, i, h)),
            pl.BlockSpec((None, S, hw), lambda b, h, i: (b, 0, C_HEADS + h)),
            pl.BlockSpec((None, S, hw), lambda b, h, i: (b, 0, 2 * C_HEADS + h)),
            pl.BlockSpec((None, nd, tk, tk), lambda b, h, i: (h, 0, 0, 0)),
            pl.BlockSpec((1, C_V_DIM), lambda b, h, i: (0, 0)),
        ],
        out_specs=pl.BlockSpec((None, tq, C_V_DIM), lambda b, h, i: (b, i, h)),
        out_shape=jax.ShapeDtypeStruct((B, S, C_HEADS * C_V_DIM), BF16),
        scratch_shapes=[pltpu.VMEM((2, tq, LANE), F32), pltpu.VMEM((2, tq, LANE), F32),
                        pltpu.VMEM((2, tq, C_V_DIM), F32)],
        compiler_params=_cp("parallel", "parallel", "arbitrary"),
        name="diff_attn",
    )(lam.reshape(1, 1).astype(F32), far, z, z, z, btile, subln.reshape(1, C_V_DIM).astype(F32))


def _mem_attn_kernel(x_ref, g_ref, wq_ref, kv_ref, wo_ref, o_ref):
    x = x_ref[...]
    h = _rms_rows(x, g_ref[...]).astype(BF16)
    q = _dot(h, wq_ref[...]).astype(BF16)
    outs = []
    for hd in range(X_HEADS):
        sl = slice(hd * X_HEAD_DIM, (hd + 1) * X_HEAD_DIM)
        s = _dot_nt(q[:, sl], kv_ref[:, sl]) * (X_HEAD_DIM ** -0.5)
        m = jnp.max(s, axis=-1, keepdims=True)
        e = jnp.exp(s - m)
        p = (e / jnp.sum(e, axis=-1, keepdims=True)).astype(BF16)
        outs.append(_dot(p, kv_ref[:, X_WIDTH + hd * X_HEAD_DIM:X_WIDTH + (hd + 1) * X_HEAD_DIM]))
    o = jnp.concatenate(outs, axis=1).astype(BF16)
    o_ref[...] = x + _dot(o, wo_ref[...])


def memory_attention(x, kv, g, wq, wo, *, tm=256):
    B, S, D = x.shape
    tm = min(tm, S)
    return pl.pallas_call(
        _mem_attn_kernel,
        grid=(B, S // tm),
        in_specs=[
            pl.BlockSpec((None, tm, D), lambda b, i: (b, i, 0)),
            pl.BlockSpec((1, D), lambda b, i: (0, 0)),
            pl.BlockSpec((D, X_WIDTH), lambda b, i: (0, 0)),
            pl.BlockSpec((None, MEM_TOKENS, 2 * X_WIDTH), lambda b, i: (b, 0, 0)),
            pl.BlockSpec((X_WIDTH, D), lambda b, i: (0, 0)),
        ],
        out_specs=pl.BlockSpec((None, tm, D), lambda b, i: (b, i, 0)),
        out_shape=jax.ShapeDtypeStruct((B, S, D), F32),
        compiler_params=_cp("parallel", "arbitrary"),
        name="mem_attn",
    )(x, g.reshape(1, D).astype(F32), wq, kv, wo)


def _first_max_onehot(vals, rows):
    m = jnp.max(vals, axis=0, keepdims=True)
    idx = jnp.min(jnp.where(vals == m, rows, vals.shape[0]), axis=0, keepdims=True)
    return rows == idx


def _router_kernel(x_ref, g_ref, w_ref, b_ref, o_ref, h_ref):
    tm = x_ref.shape[0]

    def body(r, c):
        sl = pl.ds(pl.multiple_of(r * LANE, LANE), LANE)
        h_ref[sl, :] = _rms_rows(x_ref[sl, :], g_ref[...])
        return c
    lax.fori_loop(0, tm // LANE, body, 0)
    logits = _dot(h_ref[...], w_ref[...], HI)
    lt = jnp.transpose(logits)[:N_EXPERTS, :]
    scores = _sigmoid(lt)
    biased = scores + b_ref[...]
    per_group = N_EXPERTS // N_GROUPS
    rows8 = lax.broadcasted_iota(jnp.int32, (per_group, tm), 0)
    gscore = []
    for gi in range(N_GROUPS):
        blk = biased[gi * per_group:(gi + 1) * per_group]
        first = _first_max_onehot(blk, rows8)
        m1 = jnp.max(blk, axis=0, keepdims=True)
        m2 = jnp.max(jnp.where(first, -jnp.inf, blk), axis=0, keepdims=True)
        gscore.append(m1 + m2)
    gscore = jnp.concatenate(gscore, axis=0)
    growi = lax.broadcasted_iota(jnp.int32, (N_GROUPS, tm), 0)
    gsel = jnp.zeros((N_GROUPS, tm), jnp.bool_)
    work = gscore
    for _ in range(TOPK_GROUPS):
        oh = _first_max_onehot(work, growi)
        gsel = gsel | oh
        work = jnp.where(oh, -jnp.inf, work)
    gself = gsel.astype(F32)
    masked = jnp.concatenate(
        [jnp.where(gself[gi:gi + 1] > 0.0, biased[gi * per_group:(gi + 1) * per_group], -jnp.inf)
         for gi in range(N_GROUPS)], axis=0)
    erow = lax.broadcasted_iota(jnp.int32, (N_EXPERTS, tm), 0)
    esel = jnp.zeros((N_EXPERTS, tm), jnp.bool_)
    work = masked
    for _ in range(TOP_K):
        oh = _first_max_onehot(work, erow)
        esel = esel | oh
        work = jnp.where(oh, -jnp.inf, work)
    w = jnp.where(esel, scores, 0.0)
    gates = w / jnp.sum(w, axis=0, keepdims=True) * ROUTED_SCALE
    n_shared = SHARED_FF // EXPERT_FF
    full = jnp.concatenate([gates, jnp.ones((8, tm), F32), jnp.zeros((LANE - N_EXPERTS - 8, tm), F32)], axis=0)
    frow = lax.broadcasted_iota(jnp.int32, (LANE, tm), 0)
    full = jnp.where(frow < N_EXPERTS + n_shared, full, 0.0)
    o_ref[...] = jnp.transpose(full)


def route_tokens(x, g, w_router, r_bias, *, tm=256):
    T, D = x.shape
    tm = min(tm, T)
    w = jnp.pad(w_router.astype(F32), ((0, 0), (0, LANE - N_EXPERTS)))
    return pl.pallas_call(
        _router_kernel,
        grid=(T // tm,),
        in_specs=[
            pl.BlockSpec((tm, D), lambda i: (i, 0)),
            pl.BlockSpec((1, D), lambda i: (0, 0)),
            pl.BlockSpec((D, LANE), lambda i: (0, 0)),
            pl.BlockSpec((N_EXPERTS, 1), lambda i: (0, 0)),
        ],
        out_specs=pl.BlockSpec((tm, LANE), lambda i: (i, 0)),
        out_shape=jax.ShapeDtypeStruct((T, LANE), F32),
        scratch_shapes=[pltpu.VMEM((tm, D), F32)],
        compiler_params=_cp("parallel"),
        name="moe_router",
    )(x, g.reshape(1, D).astype(F32), w, r_bias.astype(F32).reshape(N_EXPERTS, 1))


def _moe_kernel(x_ref, g_ref, gate_ref, wgu_ref, wd_ref, gf_ref, o_ref, h_ref, *, tm, final_norm):
    p = pl.program_id(1)
    npair = pl.num_programs(1)

    @pl.when(p == 0)
    def _():
        def body(r, c):
            sl = pl.ds(pl.multiple_of(r * LANE, LANE), LANE)
            x = x_ref[sl, :]
            h_ref[sl, :] = _rms_rows(x, g_ref[...]).astype(BF16)
            o_ref[sl, :] = x
            return c
        lax.fori_loop(0, tm // LANE, body, 0)

    gu = _dot(h_ref[...], wgu_ref[...])
    gates = gate_ref[...]
    lane = lax.broadcasted_iota(jnp.int32, gates.shape, 1)
    F = EXPERT_FF
    hid = []
    for e in range(MOE_EPS):
        ge = jnp.sum(jnp.where(lane == MOE_EPS * p + e, gates, 0.0), axis=1, keepdims=True)
        hid.append(_silu(gu[:, e * F:(e + 1) * F]) * gu[:, (MOE_EPS + e) * F:(MOE_EPS + 1 + e) * F] * ge)
    hid = jnp.concatenate(hid, axis=1).astype(BF16)
    o_ref[...] += _dot(hid, wd_ref[...])

    if final_norm:
        @pl.when(p == npair - 1)
        def _():
            def body(r, c):
                sl = pl.ds(pl.multiple_of(r * LANE, LANE), LANE)
                o_ref[sl, :] = _rms_rows(o_ref[sl, :], gf_ref[...])
                return c
            lax.fori_loop(0, tm // LANE, body, 0)


def moe_experts(x, g, gates, w_gu, w_d, g_final, *, final_norm, tm=512):
    T, D = x.shape
    tm = min(tm, T)
    return pl.pallas_call(
        functools.partial(_moe_kernel, tm=tm, final_norm=final_norm),
        grid=(T // tm, N_PAIRS),
        in_specs=[
            pl.BlockSpec((tm, D), lambda i, p: (i, 0), pipeline_mode=pl.Buffered(1)),
            pl.BlockSpec((1, D), lambda i, p: (0, 0)),
            pl.BlockSpec((tm, LANE), lambda i, p: (i, 0)),
            pl.BlockSpec((None, D, 2 * MOE_EPS * EXPERT_FF), lambda i, p: (p, 0, 0)),
            pl.BlockSpec((None, MOE_EPS * EXPERT_FF, D), lambda i, p: (p, 0, 0)),
            pl.BlockSpec((1, D), lambda i, p: (0, 0)),
        ],
        out_specs=pl.BlockSpec((tm, D), lambda i, p: (i, 0), pipeline_mode=pl.Buffered(1)),
        out_shape=jax.ShapeDtypeStruct((T, D), F32),
        scratch_shapes=[pltpu.VMEM((tm, D), BF16)],
        compiler_params=_cp("parallel", "arbitrary"),
        name="moe_experts",
    )(x, g.reshape(1, D).astype(F32), gates, w_gu, w_d, g_final.reshape(1, D).astype(F32))


def _moe_weights(w_gate, w_up, w_down, s_gate, s_up, s_down):
    D = w_gate.shape[1]
    ns = SHARED_FF // EXPERT_FF
    sg = jnp.transpose(s_gate.reshape(D, ns, EXPERT_FF), (1, 0, 2))
    su = jnp.transpose(s_up.reshape(D, ns, EXPERT_FF), (1, 0, 2))
    wg = jnp.concatenate([w_gate, sg], axis=0).astype(BF16).reshape(N_PAIRS, MOE_EPS, D, EXPERT_FF)
    wu = jnp.concatenate([w_up, su], axis=0).astype(BF16).reshape(N_PAIRS, MOE_EPS, D, EXPERT_FF)
    w_gu = jnp.concatenate([wg[:, e] for e in range(MOE_EPS)] + [wu[:, e] for e in range(MOE_EPS)], axis=-1)
    w_d = jnp.concatenate([w_down, s_down.reshape(ns, EXPERT_FF, D)], axis=0).astype(BF16)
    return w_gu, w_d.reshape(N_PAIRS, MOE_EPS * EXPERT_FF, D)


def kernel(x_prompt, x_sample, mem_prompt, mem_sample, rel_bias, norm_mix, norm_mem, norm_memkv, norm_ffn, norm_final, ev_w_in, ev_sinks, ev_conv, ev_a_log, ev_dt_bias, ev_onorm, ev_w_out, od_w_in, od_lambda, od_subln, od_w_out, mx_wq, mx_wkv, mx_wo, moe_router, moe_bias, moe_w_gate, moe_w_up, moe_w_down, sh_w_gate, sh_w_up, sh_w_down):
    tab = _bias_by_rel(rel_bias)

    layers = []
    for l in range(DEPTH):
        i = l // 2
        lw = {}
        if l % 2 == 0:
            w_in = ev_w_in[i]
            lw["w_in"] = w_in[:, :EVEN_MAIN].astype(BF16)
            lw["w_gate_tail"] = jnp.pad(w_in[:, EVEN_MAIN:], ((0, 0), (0, LANE - N_GATE))).astype(BF16)
            lw["w_out_a"] = ev_w_out[i][:A_Q].astype(BF16)
            lw["w_out_b"] = ev_w_out[i][A_Q:].astype(BF16)
        else:
            w_in = od_w_in[i]
            lw["w_in"] = jnp.concatenate([w_in[:, :C_QK] * (C_QK_DIM ** -0.5 * LOG2E), w_in[:, C_QK:]],
                                         axis=1).astype(BF16)
            lw["w_out"] = od_w_out[i].astype(BF16)
            lp = od_lambda[i].astype(F32)
            lam_init = 0.8 - 0.6 * math.exp(-0.3 * l)
            lw["lam_init"] = lam_init
            lw["lam"] = jnp.exp(jnp.sum(lp[0] * lp[1])) - jnp.exp(jnp.sum(lp[2] * lp[3])) + lam_init
        lw["wq"] = mx_wq[l].astype(BF16)
        lw["wkv"] = mx_wkv[l].astype(BF16)
        lw["wo"] = mx_wo[l].astype(BF16)
        lw["w_gu"], lw["w_d"] = _moe_weights(moe_w_gate[l], moe_w_up[l], moe_w_down[l],
                                             sh_w_gate[l], sh_w_up[l], sh_w_down[l])
        layers.append(lw)

    def run(x3, mem):
        B, S, D = x3.shape
        T = B * S
        x = x3.reshape(T, D)
        mem2 = mem.reshape(B * MEM_TOKENS, D)
        for l in range(DEPTH):
            lw = layers[l]
            i = l // 2
            if l % 2 == 0:
                z, gl = norm_mm(x, norm_mix[l], lw["w_in"], lw["w_gate_tail"])
                z3 = z.reshape(B, S, EVEN_MAIN)
                out_a = window_attention(z3, ev_sinks[i], tab)
                qkv = conv_qkv(z3, ev_conv[i])
                o_f, o_b = delta_rule(qkv, gl.reshape(B, S, LANE), ev_a_log[i], ev_dt_bias[i])
                out_b = delta_out(o_f.reshape(T, B_V), o_b.reshape(T, B_V), z, ev_onorm[i])
                x = mm_res(x, [out_a.reshape(T, A_Q), out_b], [lw["w_out_a"], lw["w_out_b"]])
            else:
                z = norm_mm(x, norm_mix[l], lw["w_in"])
                o = diff_attention(z.reshape(B, S, 3 * C_QK), lw["lam"], tab, od_subln[i], lw["lam_init"])
                x = mm_res(x, [o.reshape(T, C_HEADS * C_V_DIM)], [lw["w_out"]])
            kv = norm_mm(mem2, norm_memkv[l], lw["wkv"])
            x = memory_attention(x.reshape(B, S, D), kv.reshape(B, MEM_TOKENS, 2 * X_WIDTH),
                                 norm_mem[l], lw["wq"], lw["wo"]).reshape(T, D)
            gates = route_tokens(x, norm_ffn[l], moe_router[l], moe_bias[l])
            x = moe_experts(x, norm_ffn[l], gates, lw["w_gu"], lw["w_d"], norm_final,
                            final_norm=(l == DEPTH - 1))
        return x.reshape(B, S, D)

    return (run(x_prompt, mem_prompt), run(x_sample, mem_sample))
```

```python
import functools
import math

import jax
import jax.numpy as jnp
from jax import lax
from jax.experimental import pallas as pl
from jax.experimental.pallas import tpu as pltpu

F32 = jnp.float32
BF16 = jnp.bfloat16

D_MODEL = 4096
DEPTH = 2
HEAD_DIM = 128
BLOCK = 128
WINDOW = 128
A_HEADS = 16
A_KV_HEADS = 4
A_GROUP = A_HEADS // A_KV_HEADS
B_HEADS = 16
B_DK = 128
B_DV = 128
CONV_WIDTH = 5
CHUNK = 64
C_HEADS = 16
C_QK_DIM = 128
C_V_DIM = 2 * C_QK_DIM
N_BUCKETS = 32
MAX_DISTANCE = 128
MEM_TOKENS = 256
X_HEADS = 4
X_HEAD_DIM = 128
N_EXPERTS = 64
TOP_K = 8
N_GROUPS = 8
TOPK_GROUPS = 4
EXPERT_FF = 128
SHARED_FF = 512
ROUTED_SCALE = 2.5
EPS = 1e-6

A_Q = A_HEADS * HEAD_DIM
A_KV = A_KV_HEADS * HEAD_DIM
B_QK = B_HEADS * B_DK
B_V = B_HEADS * B_DV
B_CONV = 2 * B_QK + B_V
EVEN_MAIN = A_Q + 2 * A_KV + B_CONV + B_V
N_GATE = 4 * B_HEADS
C_QK = 2 * C_HEADS * C_QK_DIM
X_WIDTH = X_HEADS * X_HEAD_DIM

LANE = 128
VMEM_LIMIT = 56 * 1024 * 1024
HI = lax.Precision.HIGHEST
LOG2E = math.log2(math.e)

DELTA_HB = 4
DELTA_HG = B_HEADS // DELTA_HB
DELTA_CH = 256
DELTA_MXU = BF16
DELTA_PREC = None
MOE_EPS = 4
N_PAIRS = (N_EXPERTS + SHARED_FF // EXPERT_FF) // MOE_EPS


def _cp(*sem):
    return pltpu.CompilerParams(dimension_semantics=sem, vmem_limit_bytes=VMEM_LIMIT)


def _dot(a, b, prec=None):
    return lax.dot_general(a, b, (((1,), (0,)), ((), ())), precision=prec,
                           preferred_element_type=F32)


def _dot_nt(a, b, prec=None):
    return lax.dot_general(a, b, (((1,), (1,)), ((), ())), precision=prec,
                           preferred_element_type=F32)


def _sigmoid(x):
    return 1.0 / (1.0 + jnp.exp(-x))


def _silu(x):
    return x * _sigmoid(x)


def _softplus(x):
    return jnp.maximum(x, 0.0) + jnp.log(1.0 + jnp.exp(-jnp.abs(x)))


def _rms_rows(x, g):
    ms = jnp.mean(x * x, axis=-1, keepdims=True)
    return x * lax.rsqrt(ms + EPS) * g


def _norm_mm_kernel(x_ref, g_ref, w_ref, *rest, tm, has_tail):
    if has_tail:
        wt_ref, o_ref, ot_ref, h_ref = rest
    else:
        o_ref, h_ref = rest

    @pl.when(pl.program_id(1) == 0)
    def _():
        def body(r, c):
            sl = pl.ds(pl.multiple_of(r * LANE, LANE), LANE)
            h_ref[sl, :] = _rms_rows(x_ref[sl, :], g_ref[...]).astype(BF16)
            return c
        lax.fori_loop(0, tm // LANE, body, 0)
        if has_tail:
            ot_ref[...] = _dot(h_ref[...], wt_ref[...])

    o_ref[...] = _dot(h_ref[...], w_ref[...]).astype(o_ref.dtype)


def norm_mm(x, g, w, w_tail=None, *, tm=1024, tn=512, out_dtype=BF16):
    T, D = x.shape
    N = w.shape[1]
    tm = min(tm, T)
    assert T % tm == 0 and N % tn == 0 and tm % LANE == 0
    has_tail = w_tail is not None
    in_specs = [
        pl.BlockSpec((tm, D), lambda i, j: (i, 0), pipeline_mode=pl.Buffered(1)),
        pl.BlockSpec((1, D), lambda i, j: (0, 0)),
        pl.BlockSpec((D, tn), lambda i, j: (0, j)),
    ]
    args = [x, g.reshape(1, D).astype(F32), w]
    out_shape = [jax.ShapeDtypeStruct((T, N), out_dtype)]
    out_specs = [pl.BlockSpec((tm, tn), lambda i, j: (i, j))]
    if has_tail:
        nt = w_tail.shape[1]
        in_specs.append(pl.BlockSpec((D, nt), lambda i, j: (0, 0)))
        args.append(w_tail)
        out_shape.append(jax.ShapeDtypeStruct((T, nt), F32))
        out_specs.append(pl.BlockSpec((tm, nt), lambda i, j: (i, 0)))
    outs = pl.pallas_call(
        functools.partial(_norm_mm_kernel, tm=tm, has_tail=has_tail),
        grid=(T // tm, N // tn),
        in_specs=in_specs, out_specs=out_specs, out_shape=out_shape,
        scratch_shapes=[pltpu.VMEM((tm, D), BF16)],
        compiler_params=_cp("parallel", "arbitrary"),
        name="norm_mm",
    )(*args)
    return outs if has_tail else outs[0]


def _mm_res_kernel(*refs, n_a):
    res_ref = refs[0]
    a_refs = refs[1:1 + n_a]
    w_refs = refs[1 + n_a:1 + 2 * n_a]
    o_ref = refs[-1]
    acc = res_ref[...]
    for a, w in zip(a_refs, w_refs):
        acc = acc + _dot(a[...], w[...])
    o_ref[...] = acc


def mm_res(res, a_list, w_list, *, tm=1024, tn=512):
    T, N = res.shape
    tm = min(tm, T)
    assert T % tm == 0 and N % tn == 0
    n_a = len(a_list)
    in_specs = [pl.BlockSpec((tm, tn), lambda i, j: (i, j))]
    for a in a_list:
        in_specs.append(pl.BlockSpec((tm, a.shape[1]), lambda i, j: (i, 0)))
    for w in w_list:
        in_specs.append(pl.BlockSpec((w.shape[0], tn), lambda i, j: (0, j)))
    return pl.pallas_call(
        functools.partial(_mm_res_kernel, n_a=n_a),
        grid=(T // tm, N // tn),
        in_specs=in_specs,
        out_specs=pl.BlockSpec((tm, tn), lambda i, j: (i, j)),
        out_shape=jax.ShapeDtypeStruct((T, N), F32),
        compiler_params=_cp("parallel", "arbitrary"),
        name="mm_res",
    )(res, *a_list, *w_list)


def _t5_bucket(rel):
    nb = N_BUCKETS // 2
    max_exact = nb // 2
    ret = jnp.where(rel > 0, nb, 0)
    n = jnp.abs(rel)
    n_f = jnp.maximum(n, max_exact).astype(F32)
    large = max_exact + (jnp.log(n_f / max_exact) / math.log(MAX_DISTANCE / max_exact)
                         * (nb - max_exact)).astype(jnp.int32)
    large = jnp.minimum(large, nb - 1)
    return ret + jnp.where(n < max_exact, n, large)


def _bias_by_rel(rel_bias):
    rel = jnp.arange(-MAX_DISTANCE, MAX_DISTANCE + 1, dtype=jnp.int32)
    return rel_bias.astype(F32)[_t5_bucket(rel)]


def _win_attn_kernel(q_ref, kp_ref, kc_ref, kn_ref, vp_ref, vc_ref, vn_ref, bias_ref, sink_ref, o_ref):
    i = pl.program_id(1)
    nb = pl.num_programs(1)
    col = lax.broadcasted_iota(jnp.int32, (1, 3 * BLOCK), 1)
    invalid = ((col < BLOCK) & (i == 0)) | ((col >= 2 * BLOCK) & (i == nb - 1))
    scale = HEAD_DIM ** -0.5
    for hk in range(A_KV_HEADS):
        ks = slice(hk * HEAD_DIM, (hk + 1) * HEAD_DIM)
        qh = jnp.concatenate(
            [q_ref[:, (hk * A_GROUP + g) * HEAD_DIM:(hk * A_GROUP + g + 1) * HEAD_DIM]
             for g in range(A_GROUP)], axis=0)
        kb = jnp.concatenate([kp_ref[:, ks], kc_ref[:, ks], kn_ref[:, ks]], axis=0)
        vb = jnp.concatenate([vp_ref[:, ks], vc_ref[:, ks], vn_ref[:, ks]], axis=0)
        s = _dot_nt(qh, kb) * scale + bias_ref[hk]
        s = jnp.where(invalid, -jnp.inf, s)
        sink = sink_ref[hk]
        m = jnp.maximum(jnp.max(s, axis=-1, keepdims=True), sink)
        p = jnp.exp(s - m)
        den = jnp.sum(p, axis=-1, keepdims=True) + jnp.exp(sink - m)
        p = (p / den).astype(BF16)
        o = _dot(p, vb)
        for g in range(A_GROUP):
            h = hk * A_GROUP + g
            o_ref[:, h * HEAD_DIM:(h + 1) * HEAD_DIM] = o[g * BLOCK:(g + 1) * BLOCK].astype(o_ref.dtype)


def window_attention(z, sinks, tab):
    B, S, _ = z.shape
    nb = S // BLOCK
    q_loc = jnp.arange(BLOCK)[:, None]
    k_loc = jnp.arange(3 * BLOCK)[None, :] - BLOCK
    rel = k_loc - q_loc
    bias = jnp.where((jnp.abs(rel) <= WINDOW)[..., None],
                     tab[jnp.clip(rel, -MAX_DISTANCE, MAX_DISTANCE) + MAX_DISTANCE], -jnp.inf)
    bias = jnp.moveaxis(bias, -1, 0).reshape(A_KV_HEADS, A_GROUP * BLOCK, 3 * BLOCK)
    sink = jnp.broadcast_to(sinks.astype(F32).reshape(A_KV_HEADS, A_GROUP, 1, 1),
                            (A_KV_HEADS, A_GROUP, BLOCK, 1)).reshape(A_KV_HEADS, A_GROUP * BLOCK, 1)
    kcol = A_Q // A_KV
    vcol = (A_Q + A_KV) // A_KV
    prev = lambda b, i: (b, jnp.maximum(i - 1, 0))
    nxt = lambda b, i: (b, jnp.minimum(i + 1, nb - 1))
    return pl.pallas_call(
        _win_attn_kernel,
        grid=(B, nb),
        in_specs=[
            pl.BlockSpec((None, BLOCK, A_Q), lambda b, i: (b, i, 0)),
            pl.BlockSpec((None, BLOCK, A_KV), lambda b, i: (*prev(b, i), kcol)),
            pl.BlockSpec((None, BLOCK, A_KV), lambda b, i: (b, i, kcol)),
            pl.BlockSpec((None, BLOCK, A_KV), lambda b, i: (*nxt(b, i), kcol)),
            pl.BlockSpec((None, BLOCK, A_KV), lambda b, i: (*prev(b, i), vcol)),
            pl.BlockSpec((None, BLOCK, A_KV), lambda b, i: (b, i, vcol)),
            pl.BlockSpec((None, BLOCK, A_KV), lambda b, i: (*nxt(b, i), vcol)),
            pl.BlockSpec((A_KV_HEADS, A_GROUP * BLOCK, 3 * BLOCK), lambda b, i: (0, 0, 0)),
            pl.BlockSpec((A_KV_HEADS, A_GROUP * BLOCK, 1), lambda b, i: (0, 0, 0)),
        ],
        out_specs=pl.BlockSpec((None, BLOCK, A_Q), lambda b, i: (b, i, 0)),
        out_shape=jax.ShapeDtypeStruct((B, S, A_Q), BF16),
        compiler_params=_cp("parallel", "arbitrary"),
        name="window_attn",
    )(z, z, z, z, z, z, z, bias, sink)


CONV_TS = 512
CONV_TC = 512
CONV_HALO = 16


def _conv_kernel(prev_ref, cur_ref, next_ref, w_ref, o_ref, ext_ref):
    i = pl.program_id(1)
    j = pl.program_id(2)
    ns = pl.num_programs(1)
    ts = cur_ref.shape[0]
    pv = jnp.where(i == 0, 0.0, prev_ref[...].astype(F32))
    nx = jnp.where(i == ns - 1, 0.0, next_ref[...].astype(F32))
    ext_ref[0:CONV_HALO, :] = pv
    ext_ref[CONV_HALO:CONV_HALO + ts, :] = cur_ref[...].astype(F32)
    ext_ref[CONV_HALO + ts:, :] = nx
    half = CONV_WIDTH // 2
    acc = None
    for t in range(CONV_WIDTH):
        term = w_ref[t:t + 1, :] * ext_ref[pl.ds(CONV_HALO - half + t, ts), :]
        acc = term if acc is None else acc + term
    y = _silu(acc)
    heads_per_step = CONV_TC // B_DK
    q_steps = B_QK // CONV_TC

    def l2(scale):
        for hh in range(heads_per_step):
            seg = y[:, hh * B_DK:(hh + 1) * B_DK]
            r = lax.rsqrt(jnp.sum(seg * seg, axis=-1, keepdims=True) + EPS)
            o_ref[:, hh * B_DK:(hh + 1) * B_DK] = seg * (r * scale)

    @pl.when(j < q_steps)
    def _():
        l2(B_DK ** -0.5)

    @pl.when((j >= q_steps) & (j < 2 * q_steps))
    def _():
        l2(1.0)

    @pl.when(j >= 2 * q_steps)
    def _():
        o_ref[...] = y


def conv_qkv(z, conv_w):
    B, S, _ = z.shape
    ts = min(CONV_TS, S)
    c0 = (A_Q + 2 * A_KV) // CONV_TC
    hb = ts // CONV_HALO
    nh = S // CONV_HALO
    return pl.pallas_call(
        _conv_kernel,
        grid=(B, S // ts, B_CONV // CONV_TC),
        in_specs=[
            pl.BlockSpec((None, CONV_HALO, CONV_TC), lambda b, i, j: (b, jnp.maximum(i * hb - 1, 0), c0 + j)),
            pl.BlockSpec((None, ts, CONV_TC), lambda b, i, j: (b, i, c0 + j)),
            pl.BlockSpec((None, CONV_HALO, CONV_TC), lambda b, i, j: (b, jnp.minimum((i + 1) * hb, nh - 1), c0 + j)),
            pl.BlockSpec((CONV_WIDTH, CONV_TC), lambda b, i, j: (0, j)),
        ],
        out_specs=pl.BlockSpec((None, ts, CONV_TC), lambda b, i, j: (b, i, j)),
        out_shape=jax.ShapeDtypeStruct((B, S, B_CONV), F32),
        scratch_shapes=[pltpu.VMEM((ts + 2 * CONV_HALO, CONV_TC), F32)],
        compiler_params=_cp("parallel", "parallel", "arbitrary"),
        name="conv_qkv",
    )(z, z, z, conv_w.astype(F32))


def _bdot(a, b):
    return _dot(a.astype(DELTA_MXU), b.astype(DELTA_MXU), DELTA_PREC)


def _delta_chains(ch, eye, lmask_ref):
    n = ch[0]["q"].shape[0]
    for c in ch:
        c["decay"] = jnp.exp(jnp.where(c["causal"], c["gc_col"] - c["gc_row"], -jnp.inf))
        c["kb"] = c["k"] * c["beta"]
        c["k16"] = c["k"].astype(DELTA_MXU)
    for c in ch:
        c["m"] = jnp.where(c["strict"], _dot_nt(c["kb"].astype(DELTA_MXU), c["k16"], DELTA_PREC) * c["decay"], 0.0)
    for c in ch:
        c["inv"] = eye - c["m"] * lmask_ref[0]
    for l in range(1, lmask_ref.shape[0]):
        for c in ch:
            c["i16"] = c["inv"].astype(DELTA_MXU)
            c["p"] = _bdot(c["i16"], c["m"] * lmask_ref[l])
        for c in ch:
            c["inv"] = c["inv"] - _bdot(c["p"], c["i16"])
    for c in ch:
        c["eg"] = jnp.exp(c["gc_col"])
        c["sol"] = _bdot(c["inv"], jnp.concatenate([c["v"] * c["beta"], c["kb"] * c["eg"]], axis=1))
        c["a"] = _dot_nt(c["q"].astype(DELTA_MXU), c["k16"], DELTA_PREC) * c["decay"]
    for c in ch:
        w = c["sol"][:, B_DV:]
        c["ws"] = _bdot(jnp.concatenate([w, c["q"] * c["eg"]], axis=0), c["state"])
    out = []
    for c in ch:
        v_new = c["sol"][:, :B_DV] - c["ws"][:n]
        v16 = v_new.astype(DELTA_MXU)
        o = c["ws"][n:] + _bdot(c["a"], v16)
        k_tail = (c["k"] * jnp.exp(c["g_end"] - c["gc_col"])).astype(DELTA_MXU)
        state = c["state"] * jnp.exp(c["g_end"]) + lax.dot_general(
            k_tail, v16, (((0,), (0,)), ((), ())), precision=DELTA_PREC, preferred_element_type=F32)
        out.append((o, state))
    return out


def _delta_kernel(qf_ref, kf_ref, vf_ref, glf_ref, gtf_ref, qb_ref, kb_ref, vb_ref, glb_ref, gtb_ref,
                  arow_ref, drow_ref, acol_ref, dcol_ref, lmask_ref, of_ref, ob_ref, st_ref):
    c = pl.program_id(2)

    @pl.when(c == 0)
    def _():
        st_ref[...] = jnp.zeros_like(st_ref)

    CHUNK = DELTA_CH
    ri = lax.broadcasted_iota(jnp.int32, (CHUNK, CHUNK), 0)
    ci = lax.broadcasted_iota(jnp.int32, (CHUNK, CHUNK), 1)
    eye = (ri == ci).astype(F32)
    lower = (ri >= ci).astype(F32)
    upper = (ri <= ci).astype(F32)
    hb = DELTA_HB

    chains, sinks = [], []
    for d, (q_ref, k_ref, v_ref, gl_ref, gt_ref, o_ref) in enumerate(
            ((qf_ref, kf_ref, vf_ref, glf_ref, gtf_ref, of_ref),
             (qb_ref, kb_ref, vb_ref, glb_ref, gtb_ref, ob_ref))):
        fwd = d == 0
        gl = gl_ref[...]
        gt = gt_ref[...]
        beta_all = _sigmoid(gl)
        g_all = arow_ref[...] * _softplus(gl + drow_ref[...])
        gT_all = acol_ref[...] * _softplus(gt + dcol_ref[...])
        if fwd:
            gc_all = _dot(lower, g_all, HI)
            gcT_all = _dot(gT_all, upper, HI)
            causal, strict = ri >= ci, ri > ci
        else:
            gc_all = _dot(upper, g_all, HI)
            gcT_all = _dot(gT_all, lower, HI)
            causal, strict = ri <= ci, ri < ci
        end = CHUNK - 1 if fwd else 0
        for hh in range(hb):
            bl = d * hb + hh
            al = (2 + d) * hb + hh
            sl = slice(hh * B_DK, (hh + 1) * B_DK)
            gc_col = gc_all[:, al:al + 1]
            chains.append(dict(
                q=q_ref[:, sl], k=k_ref[:, sl], v=v_ref[:, sl], beta=beta_all[:, bl:bl + 1],
                gc_col=gc_col, gc_row=gcT_all[al:al + 1, :], g_end=gc_col[end:end + 1, :],
                state=st_ref[d * hb + hh], causal=causal, strict=strict))
            sinks.append((o_ref, sl, d * hb + hh))
    for (o, st), (o_ref, sl, si) in zip(_delta_chains(chains, eye, lmask_ref), sinks):
        o_ref[:, sl] = o
        st_ref[si] = st


def delta_rule(qkv, gl, a_log, dt_bias):
    B, S, _ = qkv.shape
    CHUNK = DELTA_CH
    n = S // CHUNK
    hb, hg = DELTA_HB, DELTA_HG
    ri = jnp.arange(CHUNK)[:, None]
    ci = jnp.arange(CHUNK)[None, :]
    lmask = jnp.stack([((ri >> (l + 1)) == (ci >> (l + 1))) & ((ri >> l) != (ci >> l))
                       for l in range(int(math.log2(CHUNK)))]).astype(F32)
    gw = 4 * hb
    g4 = gl[..., :N_GATE].reshape(B, S, 4, hg, hb)
    g4 = jnp.transpose(g4, (0, 3, 1, 2, 4)).reshape(B, hg, S, gw)
    g_rows = jnp.pad(g4, ((0, 0), (0, 0), (0, 0), (0, LANE - gw)))
    g_cols = jnp.transpose(g4.reshape(B, hg, n, CHUNK, gw), (0, 1, 2, 4, 3))
    neg_a = -jnp.exp(a_log.astype(F32)).reshape(2, hg, hb)
    dtb = dt_bias.astype(F32).reshape(2, hg, hb)
    zeros = jnp.zeros((2, hg, hb), F32)
    a4 = jnp.transpose(jnp.concatenate([zeros, neg_a], 0), (1, 0, 2)).reshape(hg, gw)
    d4 = jnp.transpose(jnp.concatenate([zeros, dtb], 0), (1, 0, 2)).reshape(hg, gw)
    arow = jnp.pad(a4, ((0, 0), (0, LANE - gw))).reshape(hg, 1, LANE)
    drow = jnp.pad(d4, ((0, 0), (0, LANE - gw))).reshape(hg, 1, LANE)
    acol = a4.reshape(hg, gw, 1)
    dcol = d4.reshape(hg, gw, 1)
    cw = hb * B_DK
    kq, kk, kv = 0, B_QK // cw, 2 * B_QK // cw

    def seq_specs(cidx):
        return [
            pl.BlockSpec((None, CHUNK, cw), lambda b, g, c: (b, cidx(c), kq + g)),
            pl.BlockSpec((None, CHUNK, cw), lambda b, g, c: (b, cidx(c), kk + g)),
            pl.BlockSpec((None, CHUNK, cw), lambda b, g, c: (b, cidx(c), kv + g)),
            pl.BlockSpec((None, None, CHUNK, LANE), lambda b, g, c: (b, g, cidx(c), 0)),
            pl.BlockSpec((None, None, None, gw, CHUNK), lambda b, g, c: (b, g, cidx(c), 0, 0)),
        ]

    fwd_idx = lambda c: c
    bwd_idx = lambda c: n - 1 - c
    par_specs = [
        pl.BlockSpec((None, 1, LANE), lambda b, g, c: (g, 0, 0)),
        pl.BlockSpec((None, 1, LANE), lambda b, g, c: (g, 0, 0)),
        pl.BlockSpec((None, gw, 1), lambda b, g, c: (g, 0, 0)),
        pl.BlockSpec((None, gw, 1), lambda b, g, c: (g, 0, 0)),
        pl.BlockSpec(lmask.shape, lambda b, g, c: (0, 0, 0)),
    ]
    o_f, o_b = pl.pallas_call(
        _delta_kernel,
        grid=(B, hg, n),
        in_specs=seq_specs(fwd_idx) + seq_specs(bwd_idx) + par_specs,
        out_specs=[
            pl.BlockSpec((None, CHUNK, cw), lambda b, g, c: (b, fwd_idx(c), g)),
            pl.BlockSpec((None, CHUNK, cw), lambda b, g, c: (b, bwd_idx(c), g)),
        ],
        out_shape=[jax.ShapeDtypeStruct((B, S, B_V), F32)] * 2,
        scratch_shapes=[pltpu.VMEM((2 * hb, B_DK, B_DV), F32)],
        compiler_params=_cp("parallel", "parallel", "arbitrary"),
        name="delta_rule",
    )(qkv, qkv, qkv, g_rows, g_cols, qkv, qkv, qkv, g_rows, g_cols, arow, drow, acol, dcol, lmask)
    return o_f, o_b


GATE_TC = 1024


def _delta_out_kernel(of_ref, ob_ref, zb_ref, g_ref, o_ref):
    for hh in range(GATE_TC // B_DV):
        sl = slice(hh * B_DV, (hh + 1) * B_DV)
        ob = of_ref[:, sl] + ob_ref[:, sl]
        y = _rms_rows(ob, g_ref[...])
        o_ref[:, sl] = (y * _silu(zb_ref[:, sl].astype(F32))).astype(o_ref.dtype)


def delta_out(o_f, o_b, z2d, onorm, *, tm=512):
    T = o_f.shape[0]
    tm = min(tm, T)
    c0 = (A_Q + 2 * A_KV + B_CONV) // GATE_TC
    return pl.pallas_call(
        _delta_out_kernel,
        grid=(T // tm, B_V // GATE_TC),
        in_specs=[
            pl.BlockSpec((tm, GATE_TC), lambda i, j: (i, j)),
            pl.BlockSpec((tm, GATE_TC), lambda i, j: (i, j)),
            pl.BlockSpec((tm, GATE_TC), lambda i, j: (i, c0 + j)),
            pl.BlockSpec((1, B_DV), lambda i, j: (0, 0)),
        ],
        out_specs=pl.BlockSpec((tm, GATE_TC), lambda i, j: (i, j)),
        out_shape=jax.ShapeDtypeStruct((T, B_V), BF16),
        compiler_params=_cp("parallel", "arbitrary"),
        name="delta_out",
    )(o_f, o_b, z2d, onorm.reshape(1, B_DV).astype(F32))


DIFF_TQ = 1024
DIFF_TK = 512


def _diff_attn_kernel(lam_ref, far_ref, q_ref, k_ref, v_ref, b_ref, g_ref, o_ref, m_ref, l_ref, acc_ref, *, out_scale, tk):
    h = pl.program_id(1)
    qi = pl.program_id(2)
    nk = k_ref.shape[0] // tk
    r = q_ref.shape[0] // tk
    base = qi * r

    m_ref[...] = jnp.full_like(m_ref, -jnp.inf)
    l_ref[...] = jnp.zeros_like(l_ref)
    acc_ref[...] = jnp.zeros_like(acc_ref)

    def step(kj, near_dj, far_const):
        rows = pl.ds(pl.multiple_of(kj * tk, tk), tk)
        v = v_ref[rows, :]
        units = [(c, t) for c in range(r) for t in range(2)]

        def scores(c, t):
            sl = slice(t * C_QK_DIM, (t + 1) * C_QK_DIM)
            return _dot_nt(q_ref[c * tk:(c + 1) * tk, sl], k_ref[rows, sl])

        s_next = scores(*units[0])
        for i, (c, t) in enumerate(units):
            s = s_next
            if i + 1 < len(units):
                s_next = scores(*units[i + 1])
            qr = slice(c * tk, (c + 1) * tk)
            off = None if near_dj is None else near_dj - c
            if off is None:
                const = far_const
            elif abs(off) <= 1:
                const = None
                s = s + b_ref[off + 1]
            else:
                const = far_ref[0 if off < 0 else 1, h]
            m_loc = jnp.max(s, axis=-1, keepdims=True)
            if const is not None:
                m_loc = m_loc + const
            m_old = m_ref[t, qr, :]
            m_new = jnp.maximum(m_old, m_loc)
            alpha = jnp.exp2(m_old - m_new)
            shift = m_new if const is None else m_new - const
            p = jnp.exp2(s - jnp.concatenate([shift] * (tk // LANE), axis=1))
            psum = p[:, :LANE]
            for cc in range(1, tk // LANE):
                psum = psum + p[:, cc * LANE:(cc + 1) * LANE]
            l_ref[t, qr, :] = alpha * l_ref[t, qr, :] + psum
            acc_ref[t, qr, :] = (jnp.concatenate([alpha] * (C_V_DIM // LANE), axis=1) * acc_ref[t, qr, :]
                                 + _dot(p.astype(BF16), v))
            m_ref[t, qr, :] = m_new

    def far_left(kj, carry):
        step(kj, None, far_ref[0, h])
        return carry

    def far_right(kj, carry):
        step(kj, None, far_ref[1, h])
        return carry

    lax.fori_loop(0, jnp.maximum(base - 1, 0), far_left, 0)
    for dj in range(-1, r + 1):
        @pl.when((base + dj >= 0) & (base + dj < nk))
        def _():
            step(base + dj, dj, None)
    lax.fori_loop(jnp.minimum(base + r + 1, nk), nk, far_right, 0)

    lam = lam_ref[0, 0]
    r0 = 1.0 / jnp.sum(l_ref[0], axis=-1, keepdims=True)
    r1 = 1.0 / jnp.sum(l_ref[1], axis=-1, keepdims=True)
    o = acc_ref[0] * r0 - acc_ref[1] * (lam * r1)
    o_ref[...] = (_rms_rows(o, g_ref[...]) * out_scale).astype(o_ref.dtype)


def diff_attention(z, lam, tab, subln, lam_init):
    B, S, _ = z.shape
    tq, tk = min(DIFF_TQ, S), min(DIFF_TK, S)
    assert tk > MAX_DISTANCE and tq % tk == 0
    nq = S // tq
    r = tq // tk
    tab2 = tab.T * LOG2E
    nd = 3
    period = 2 * tk + 1
    u = jnp.arange(period)[None, :]
    d = jnp.arange(-1, 2)[:, None]
    idx = jnp.clip(d * tk + u - tk, -MAX_DISTANCE, MAX_DISTANCE) + MAX_DISTANCE
    sig = tab2[:, idx]
    skew = jnp.tile(sig, (1, 1, tk))[..., :tk * (period - 1)].reshape(C_HEADS, nd, tk, period - 1)
    btile = skew[..., tk:2 * tk]
    far = jnp.stack([tab2[:, 0], tab2[:, -1]])
    hw = 2 * C_QK_DIM
    return pl.pallas_call(
        functools.partial(_diff_attn_kernel, out_scale=1.0 - lam_init, tk=tk),
        grid=(B, C_HEADS, nq),
        in_specs=[
            pl.BlockSpec(memory_space=pltpu.SMEM),
            pl.BlockSpec(memory_space=pltpu.SMEM),
            pl.BlockSpec((None, tq, hw), lambda b, h, i: (b, i, h)),
            pl.BlockSpec((None, S, hw), lambda b, h, i: (b, 0, C_HEADS + h)),
            pl.BlockSpec((None, S, hw), lambda b, h, i: (b, 0, 2 * C_HEADS + h)),
            pl.BlockSpec((None, nd, tk, tk), lambda b, h, i: (h, 0, 0, 0)),
            pl.BlockSpec((1, C_V_DIM), lambda b, h, i: (0, 0)),
        ],
        out_specs=pl.BlockSpec((None, tq, C_V_DIM), lambda b, h, i: (b, i, h)),
        out_shape=jax.ShapeDtypeStruct((B, S, C_HEADS * C_V_DIM), BF16),
        scratch_shapes=[pltpu.VMEM((2, tq, LANE), F32), pltpu.VMEM((2, tq, LANE), F32),
                        pltpu.VMEM((2, tq, C_V_DIM), F32)],
        compiler_params=_cp("parallel", "parallel", "arbitrary"),
        name="diff_attn",
    )(lam.reshape(1, 1).astype(F32), far, z, z, z, btile, subln.reshape(1, C_V_DIM).astype(F32))


def _mem_attn_kernel(x_ref, g_ref, wq_ref, kv_ref, wo_ref, o_ref):
    x = x_ref[...]
    h = _rms_rows(x, g_ref[...]).astype(BF16)
    q = _dot(h, wq_ref[...]).astype(BF16)
    outs = []
    for hd in range(X_HEADS):
        sl = slice(hd * X_HEAD_DIM, (hd + 1) * X_HEAD_DIM)
        s = _dot_nt(q[:, sl], kv_ref[:, sl]) * (X_HEAD_DIM ** -0.5)
        m = jnp.max(s, axis=-1, keepdims=True)
        e = jnp.exp(s - m)
        p = (e / jnp.sum(e, axis=-1, keepdims=True)).astype(BF16)
        outs.append(_dot(p, kv_ref[:, X_WIDTH + hd * X_HEAD_DIM:X_WIDTH + (hd + 1) * X_HEAD_DIM]))
    o = jnp.concatenate(outs, axis=1).astype(BF16)
    o_ref[...] = x + _dot(o, wo_ref[...])


def memory_attention(x, kv, g, wq, wo, *, tm=256):
    B, S, D = x.shape
    tm = min(tm, S)
    return pl.pallas_call(
        _mem_attn_kernel,
        grid=(B, S // tm),
        in_specs=[
            pl.BlockSpec((None, tm, D), lambda b, i: (b, i, 0)),
            pl.BlockSpec((1, D), lambda b, i: (0, 0)),
            pl.BlockSpec((D, X_WIDTH), lambda b, i: (0, 0)),
            pl.BlockSpec((None, MEM_TOKENS, 2 * X_WIDTH), lambda b, i: (b, 0, 0)),
            pl.BlockSpec((X_WIDTH, D), lambda b, i: (0, 0)),
        ],
        out_specs=pl.BlockSpec((None, tm, D), lambda b, i: (b, i, 0)),
        out_shape=jax.ShapeDtypeStruct((B, S, D), F32),
        compiler_params=_cp("parallel", "arbitrary"),
        name="mem_attn",
    )(x, g.reshape(1, D).astype(F32), wq, kv, wo)


def _first_max_onehot(vals, rows):
    m = jnp.max(vals, axis=0, keepdims=True)
    idx = jnp.min(jnp.where(vals == m, rows, vals.shape[0]), axis=0, keepdims=True)
    return rows == idx


def _router_kernel(x_ref, g_ref, w_ref, b_ref, o_ref, h_ref):
    tm = x_ref.shape[0]

    def body(r, c):
        sl = pl.ds(pl.multiple_of(r * LANE, LANE), LANE)
        h_ref[sl, :] = _rms_rows(x_ref[sl, :], g_ref[...])
        return c
    lax.fori_loop(0, tm // LANE, body, 0)
    logits = _dot(h_ref[...], w_ref[...], HI)
    lt = jnp.transpose(logits)[:N_EXPERTS, :]
    scores = _sigmoid(lt)
    biased = scores + b_ref[...]
    per_group = N_EXPERTS // N_GROUPS
    rows8 = lax.broadcasted_iota(jnp.int32, (per_group, tm), 0)
    gscore = []
    for gi in range(N_GROUPS):
        blk = biased[gi * per_group:(gi + 1) * per_group]
        first = _first_max_onehot(blk, rows8)
        m1 = jnp.max(blk, axis=0, keepdims=True)
        m2 = jnp.max(jnp.where(first, -jnp.inf, blk), axis=0, keepdims=True)
        gscore.append(m1 + m2)
    gscore = jnp.concatenate(gscore, axis=0)
    growi = lax.broadcasted_iota(jnp.int32, (N_GROUPS, tm), 0)
    gsel = jnp.zeros((N_GROUPS, tm), jnp.bool_)
    work = gscore
    for _ in range(TOPK_GROUPS):
        oh = _first_max_onehot(work, growi)
        gsel = gsel | oh
        work = jnp.where(oh, -jnp.inf, work)
    gself = gsel.astype(F32)
    masked = jnp.concatenate(
        [jnp.where(gself[gi:gi + 1] > 0.0, biased[gi * per_group:(gi + 1) * per_group], -jnp.inf)
         for gi in range(N_GROUPS)], axis=0)
    erow = lax.broadcasted_iota(jnp.int32, (N_EXPERTS, tm), 0)
    esel = jnp.zeros((N_EXPERTS, tm), jnp.bool_)
    work = masked
    for _ in range(TOP_K):
        oh = _first_max_onehot(work, erow)
        esel = esel | oh
        work = jnp.where(oh, -jnp.inf, work)
    w = jnp.where(esel, scores, 0.0)
    gates = w / jnp.sum(w, axis=0, keepdims=True) * ROUTED_SCALE
    n_shared = SHARED_FF // EXPERT_FF
    full = jnp.concatenate([gates, jnp.ones((8, tm), F32), jnp.zeros((LANE - N_EXPERTS - 8, tm), F32)], axis=0)
    frow = lax.broadcasted_iota(jnp.int32, (LANE, tm), 0)
    full = jnp.where(frow < N_EXPERTS + n_shared, full, 0.0)
    o_ref[...] = jnp.transpose(full)


def route_tokens(x, g, w_router, r_bias, *, tm=512):
    T, D = x.shape
    tm = min(tm, T)
    w = jnp.pad(w_router.astype(F32), ((0, 0), (0, LANE - N_EXPERTS)))
    return pl.pallas_call(
        _router_kernel,
        grid=(T // tm,),
        in_specs=[
            pl.BlockSpec((tm, D), lambda i: (i, 0)),
            pl.BlockSpec((1, D), lambda i: (0, 0)),
            pl.BlockSpec((D, LANE), lambda i: (0, 0)),
            pl.BlockSpec((N_EXPERTS, 1), lambda i: (0, 0)),
        ],
        out_specs=pl.BlockSpec((tm, LANE), lambda i: (i, 0)),
        out_shape=jax.ShapeDtypeStruct((T, LANE), F32),
        scratch_shapes=[pltpu.VMEM((tm, D), F32)],
        compiler_params=_cp("parallel"),
        name="moe_router",
    )(x, g.reshape(1, D).astype(F32), w, r_bias.astype(F32).reshape(N_EXPERTS, 1))


def _pattern_rank_table():
    def revolving(n, k):
        if k == 0:
            return [[]]
        if k == n:
            return [list(range(n))]
        return revolving(n - 1, k) + [c + [n - 1] for c in reversed(revolving(n - 1, k - 1))]

    combos = [sum(1 << g for g in c) for c in revolving(N_GROUPS, TOPK_GROUPS)]
    table = []
    for m in range(1 << N_GROUPS):
        ranks = [i for i, c in enumerate(combos) if (c & m) == m]
        table.append(min(ranks) if ranks else len(combos))
    return table


_PATTERN_RANK = _pattern_rank_table()


def _moe_plan(gates, tm):
    T = gates.shape[0]
    nt = T // tm
    per_group = N_EXPERTS // N_GROUPS
    sel = gates[:, :N_EXPERTS] > 0.0
    gsel = jnp.any(sel.reshape(T, N_GROUPS, per_group), axis=-1)
    gmask = jnp.sum(gsel.astype(jnp.int32) << jnp.arange(N_GROUPS, dtype=jnp.int32), axis=-1)
    order = jnp.argsort(jnp.asarray(_PATTERN_RANK, jnp.int32)[gmask]).astype(jnp.int32)
    gates_s = gates[order]
    nr = N_EXPERTS // MOE_EPS
    used = jnp.any((gates_s[:, :N_EXPERTS] > 0.0).reshape(nt, tm, nr, MOE_EPS), axis=(1, 3))
    used = jnp.concatenate([used, jnp.ones((nt, N_PAIRS - nr), jnp.bool_)], axis=1)
    fetch = lax.cummax(jnp.where(used, jnp.arange(N_PAIRS, dtype=jnp.int32), 0), axis=1)
    return order, gates_s, used.astype(jnp.int32).reshape(-1), fetch.astype(jnp.int32).reshape(-1)


def _moe_kernel(order_ref, used_ref, fetch_ref, x_hbm, g_ref, gate_ref, wgu_ref, wd_ref, gf_ref, o_hbm,
                acc_ref, h_ref, sem_ref, *, tm, final_norm):
    i = pl.program_id(0)
    p = pl.program_id(1)
    npair = pl.num_programs(1)
    base = i * tm

    def row_in(r):
        return pltpu.make_async_copy(x_hbm.at[pl.ds(order_ref[base + r], 1)], acc_ref.at[pl.ds(r, 1)],
                                     sem_ref.at[0])

    def row_out(r):
        return pltpu.make_async_copy(acc_ref.at[pl.ds(r, 1)], o_hbm.at[pl.ds(order_ref[base + r], 1)],
                                     sem_ref.at[1])

    def for_rows(fn):
        def body(r, c):
            fn(r)
            return c
        lax.fori_loop(0, tm, body, 0)

    @pl.when(p == 0)
    def _():
        for_rows(lambda r: row_in(r).start())
        for_rows(lambda r: row_in(r).wait())

        def body(r, c):
            sl = pl.ds(pl.multiple_of(r * LANE, LANE), LANE)
            h_ref[sl, :] = _rms_rows(acc_ref[sl, :], g_ref[...]).astype(BF16)
            return c
        lax.fori_loop(0, tm // LANE, body, 0)

    @pl.when(used_ref[i * npair + p] != 0)
    def _():
        gu = _dot(h_ref[...], wgu_ref[...])
        gates = gate_ref[...]
        lane = lax.broadcasted_iota(jnp.int32, gates.shape, 1)
        F = EXPERT_FF
        hid = []
        for e in range(MOE_EPS):
            ge = jnp.sum(jnp.where(lane == MOE_EPS * p + e, gates, 0.0), axis=1, keepdims=True)
            hid.append(_silu(gu[:, e * F:(e + 1) * F]) * gu[:, (MOE_EPS + e) * F:(MOE_EPS + 1 + e) * F] * ge)
        hid = jnp.concatenate(hid, axis=1).astype(BF16)
        acc_ref[...] += _dot(hid, wd_ref[...])

    @pl.when(p == npair - 1)
    def _():
        if final_norm:
            def body(r, c):
                sl = pl.ds(pl.multiple_of(r * LANE, LANE), LANE)
                acc_ref[sl, :] = _rms_rows(acc_ref[sl, :], gf_ref[...])
                return c
            lax.fori_loop(0, tm // LANE, body, 0)
        for_rows(lambda r: row_out(r).start())
        for_rows(lambda r: row_out(r).wait())


def moe_experts(x, g, gates, w_gu, w_d, g_final, *, layer, final_norm, tm=512):
    T, D = x.shape
    tm = min(tm, T)
    order, gates_s, used, fetch = _moe_plan(gates, tm)
    wblock = lambda i, p, order, used, fetch: (layer * N_PAIRS + fetch[i * N_PAIRS + p], 0, 0)
    return pl.pallas_call(
        functools.partial(_moe_kernel, tm=tm, final_norm=final_norm),
        grid_spec=pltpu.PrefetchScalarGridSpec(
            num_scalar_prefetch=3,
            grid=(T // tm, N_PAIRS),
            in_specs=[
                pl.BlockSpec(memory_space=pl.ANY),
                pl.BlockSpec((1, D), lambda i, p, *_: (0, 0)),
                pl.BlockSpec((tm, LANE), lambda i, p, *_: (i, 0)),
                pl.BlockSpec((None, D, 2 * MOE_EPS * EXPERT_FF), wblock),
                pl.BlockSpec((None, MOE_EPS * EXPERT_FF, D), wblock),
                pl.BlockSpec((1, D), lambda i, p, *_: (0, 0)),
            ],
            out_specs=pl.BlockSpec(memory_space=pl.ANY),
            scratch_shapes=[pltpu.VMEM((tm, D), F32), pltpu.VMEM((tm, D), BF16),
                            pltpu.SemaphoreType.DMA((2,))],
        ),
        out_shape=jax.ShapeDtypeStruct((T, D), F32),
        compiler_params=_cp("arbitrary", "arbitrary"),
        name="moe_experts",
    )(order, used, fetch, x, g.reshape(1, D).astype(F32), gates_s, w_gu, w_d, g_final.reshape(1, D).astype(F32))


def _moe_weights(w_gate, w_up, w_down, s_gate, s_up, s_down):
    L, _, D, F = w_gate.shape
    ns = SHARED_FF // F

    def grouped(w, s):
        sh = jnp.transpose(s.reshape(L, D, ns, F), (0, 2, 1, 3))
        return jnp.concatenate([w, sh], axis=1).reshape(L, N_PAIRS, MOE_EPS, D, F)

    gu = jnp.concatenate([grouped(w_gate, s_gate), grouped(w_up, s_up)], axis=2)
    w_gu = jnp.transpose(gu, (0, 1, 3, 2, 4)).reshape(L * N_PAIRS, D, 2 * MOE_EPS * F).astype(BF16)
    w_d = jnp.concatenate([w_down, s_down.reshape(L, ns, F, D)], axis=1).astype(BF16)
    return w_gu, w_d.reshape(L * N_PAIRS, MOE_EPS * F, D)


def kernel(x_prompt, x_sample, mem_prompt, mem_sample, rel_bias, norm_mix, norm_mem, norm_memkv, norm_ffn, norm_final, ev_w_in, ev_sinks, ev_conv, ev_a_log, ev_dt_bias, ev_onorm, ev_w_out, od_w_in, od_lambda, od_subln, od_w_out, mx_wq, mx_wkv, mx_wo, moe_router, moe_bias, moe_w_gate, moe_w_up, moe_w_down, sh_w_gate, sh_w_up, sh_w_down):
    tab = _bias_by_rel(rel_bias)

    w_gu, w_d = _moe_weights(moe_w_gate, moe_w_up, moe_w_down, sh_w_gate, sh_w_up, sh_w_down)
    layers = []
    for l in range(DEPTH):
        i = l // 2
        lw = {}
        if l % 2 == 0:
            w_in = ev_w_in[i]
            lw["w_in"] = w_in[:, :EVEN_MAIN].astype(BF16)
            lw["w_gate_tail"] = jnp.pad(w_in[:, EVEN_MAIN:], ((0, 0), (0, LANE - N_GATE))).astype(BF16)
            lw["w_out_a"] = ev_w_out[i][:A_Q].astype(BF16)
            lw["w_out_b"] = ev_w_out[i][A_Q:].astype(BF16)
        else:
            w_in = od_w_in[i]
            lw["w_in"] = jnp.concatenate([w_in[:, :C_QK] * (C_QK_DIM ** -0.5 * LOG2E), w_in[:, C_QK:]],
                                         axis=1).astype(BF16)
            lw["w_out"] = od_w_out[i].astype(BF16)
            lp = od_lambda[i].astype(F32)
            lam_init = 0.8 - 0.6 * math.exp(-0.3 * l)
            lw["lam_init"] = lam_init
            lw["lam"] = jnp.exp(jnp.sum(lp[0] * lp[1])) - jnp.exp(jnp.sum(lp[2] * lp[3])) + lam_init
        lw["wq"] = mx_wq[l].astype(BF16)
        lw["wkv"] = mx_wkv[l].astype(BF16)
        lw["wo"] = mx_wo[l].astype(BF16)
        layers.append(lw)

    def run(x3, mem):
        B, S, D = x3.shape
        T = B * S
        x = x3.reshape(T, D)
        mem2 = mem.reshape(B * MEM_TOKENS, D)
        for l in range(DEPTH):
            lw = layers[l]
            i = l // 2
            if l % 2 == 0:
                z, gl = norm_mm(x, norm_mix[l], lw["w_in"], lw["w_gate_tail"])
                z3 = z.reshape(B, S, EVEN_MAIN)
                out_a = window_attention(z3, ev_sinks[i], tab)
                qkv = conv_qkv(z3, ev_conv[i])
                o_f, o_b = delta_rule(qkv, gl.reshape(B, S, LANE), ev_a_log[i], ev_dt_bias[i])
                out_b = delta_out(o_f.reshape(T, B_V), o_b.reshape(T, B_V), z, ev_onorm[i])
                x = mm_res(x, [out_a.reshape(T, A_Q), out_b], [lw["w_out_a"], lw["w_out_b"]])
            else:
                z = norm_mm(x, norm_mix[l], lw["w_in"])
                o = diff_attention(z.reshape(B, S, 3 * C_QK), lw["lam"], tab, od_subln[i], lw["lam_init"])
                x = mm_res(x, [o.reshape(T, C_HEADS * C_V_DIM)], [lw["w_out"]])
            kv = norm_mm(mem2, norm_memkv[l], lw["wkv"])
            x = memory_attention(x.reshape(B, S, D), kv.reshape(B, MEM_TOKENS, 2 * X_WIDTH),
                                 norm_mem[l], lw["wq"], lw["wo"]).reshape(T, D)
            gates = route_tokens(x, norm_ffn[l], moe_router[l], moe_bias[l])
            x = moe_experts(x, norm_ffn[l], gates, w_gu, w_d, norm_final,
                            layer=l, final_norm=(l == DEPTH - 1))
        return x.reshape(B, S, D)

    return (run(x_prompt, mem_prompt), run(x_sample, mem_sample))
```

```python
import functools
import math

import jax
import jax.numpy as jnp
from jax import lax
from jax.experimental import pallas as pl
from jax.experimental.pallas import tpu as pltpu

F32 = jnp.float32
BF16 = jnp.bfloat16

D_MODEL = 4096
DEPTH = 2
HEAD_DIM = 128
BLOCK = 128
WINDOW = 128
A_HEADS = 16
A_KV_HEADS = 4
A_GROUP = A_HEADS // A_KV_HEADS
B_HEADS = 16
B_DK = 128
B_DV = 128
CONV_WIDTH = 5
CHUNK = 64
C_HEADS = 16
C_QK_DIM = 128
C_V_DIM = 2 * C_QK_DIM
N_BUCKETS = 32
MAX_DISTANCE = 128
MEM_TOKENS = 256
X_HEADS = 4
X_HEAD_DIM = 128
N_EXPERTS = 64
TOP_K = 8
N_GROUPS = 8
TOPK_GROUPS = 4
EXPERT_FF = 128
SHARED_FF = 512
ROUTED_SCALE = 2.5
EPS = 1e-6

A_Q = A_HEADS * HEAD_DIM
A_KV = A_KV_HEADS * HEAD_DIM
B_QK = B_HEADS * B_DK
B_V = B_HEADS * B_DV
B_CONV = 2 * B_QK + B_V
EVEN_MAIN = A_Q + 2 * A_KV + B_CONV + B_V
N_GATE = 4 * B_HEADS
C_QK = 2 * C_HEADS * C_QK_DIM
X_WIDTH = X_HEADS * X_HEAD_DIM

LANE = 128
VMEM_LIMIT = 56 * 1024 * 1024
HI = lax.Precision.HIGHEST
LOG2E = math.log2(math.e)

DELTA_HB = 4
DELTA_HG = B_HEADS // DELTA_HB
DELTA_CH = 256
DELTA_MXU = BF16
DELTA_PREC = None
MOE_EPS = 4
N_PAIRS = (N_EXPERTS + SHARED_FF // EXPERT_FF) // MOE_EPS


def _cp(*sem):
    return pltpu.CompilerParams(dimension_semantics=sem, vmem_limit_bytes=VMEM_LIMIT)


def _dot(a, b, prec=None):
    return lax.dot_general(a, b, (((1,), (0,)), ((), ())), precision=prec,
                           preferred_element_type=F32)


def _dot_nt(a, b, prec=None):
    return lax.dot_general(a, b, (((1,), (1,)), ((), ())), precision=prec,
                           preferred_element_type=F32)


def _sigmoid(x):
    return 1.0 / (1.0 + jnp.exp(-x))


def _silu(x):
    return x * _sigmoid(x)


def _softplus(x):
    return jnp.maximum(x, 0.0) + jnp.log(1.0 + jnp.exp(-jnp.abs(x)))


def _rms_rows(x, g):
    ms = jnp.mean(x * x, axis=-1, keepdims=True)
    return x * lax.rsqrt(ms + EPS) * g


def _norm_mm_kernel(x_ref, g_ref, w_ref, *rest, tm, has_tail):
    if has_tail:
        wt_ref, o_ref, ot_ref, h_ref = rest
    else:
        o_ref, h_ref = rest

    @pl.when(pl.program_id(1) == 0)
    def _():
        def body(r, c):
            sl = pl.ds(pl.multiple_of(r * LANE, LANE), LANE)
            h_ref[sl, :] = _rms_rows(x_ref[sl, :], g_ref[...]).astype(BF16)
            return c
        lax.fori_loop(0, tm // LANE, body, 0)
        if has_tail:
            ot_ref[...] = _dot(h_ref[...], wt_ref[...])

    o_ref[...] = _dot(h_ref[...], w_ref[...]).astype(o_ref.dtype)


def norm_mm(x, g, w, w_tail=None, *, tm=1024, tn=512, out_dtype=BF16):
    T, D = x.shape
    N = w.shape[1]
    tm = min(tm, T)
    assert T % tm == 0 and N % tn == 0 and tm % LANE == 0
    has_tail = w_tail is not None
    in_specs = [
        pl.BlockSpec((tm, D), lambda i, j: (i, 0), pipeline_mode=pl.Buffered(1)),
        pl.BlockSpec((1, D), lambda i, j: (0, 0)),
        pl.BlockSpec((D, tn), lambda i, j: (0, j)),
    ]
    args = [x, g.reshape(1, D).astype(F32), w]
    out_shape = [jax.ShapeDtypeStruct((T, N), out_dtype)]
    out_specs = [pl.BlockSpec((tm, tn), lambda i, j: (i, j))]
    if has_tail:
        nt = w_tail.shape[1]
        in_specs.append(pl.BlockSpec((D, nt), lambda i, j: (0, 0)))
        args.append(w_tail)
        out_shape.append(jax.ShapeDtypeStruct((T, nt), F32))
        out_specs.append(pl.BlockSpec((tm, nt), lambda i, j: (i, 0)))
    outs = pl.pallas_call(
        functools.partial(_norm_mm_kernel, tm=tm, has_tail=has_tail),
        grid=(T // tm, N // tn),
        in_specs=in_specs, out_specs=out_specs, out_shape=out_shape,
        scratch_shapes=[pltpu.VMEM((tm, D), BF16)],
        compiler_params=_cp("parallel", "arbitrary"),
        name="norm_mm",
    )(*args)
    return outs if has_tail else outs[0]


def _mm_res_kernel(*refs, n_a):
    res_ref = refs[0]
    a_refs = refs[1:1 + n_a]
    w_refs = refs[1 + n_a:1 + 2 * n_a]
    o_ref = refs[-1]
    acc = res_ref[...]
    for a, w in zip(a_refs, w_refs):
        acc = acc + _dot(a[...], w[...])
    o_ref[...] = acc


def mm_res(res, a_list, w_list, *, tm=1024, tn=512):
    T, N = res.shape
    tm = min(tm, T)
    assert T % tm == 0 and N % tn == 0
    n_a = len(a_list)
    in_specs = [pl.BlockSpec((tm, tn), lambda i, j: (i, j))]
    for a in a_list:
        in_specs.append(pl.BlockSpec((tm, a.shape[1]), lambda i, j: (i, 0)))
    for w in w_list:
        in_specs.append(pl.BlockSpec((w.shape[0], tn), lambda i, j: (0, j)))
    return pl.pallas_call(
        functools.partial(_mm_res_kernel, n_a=n_a),
        grid=(T // tm, N // tn),
        in_specs=in_specs,
        out_specs=pl.BlockSpec((tm, tn), lambda i, j: (i, j)),
        out_shape=jax.ShapeDtypeStruct((T, N), F32),
        compiler_params=_cp("parallel", "arbitrary"),
        name="mm_res",
    )(res, *a_list, *w_list)


def _t5_bucket(rel):
    nb = N_BUCKETS // 2
    max_exact = nb // 2
    ret = jnp.where(rel > 0, nb, 0)
    n = jnp.abs(rel)
    n_f = jnp.maximum(n, max_exact).astype(F32)
    large = max_exact + (jnp.log(n_f / max_exact) / math.log(MAX_DISTANCE / max_exact)
                         * (nb - max_exact)).astype(jnp.int32)
    large = jnp.minimum(large, nb - 1)
    return ret + jnp.where(n < max_exact, n, large)


def _bias_by_rel(rel_bias):
    rel = jnp.arange(-MAX_DISTANCE, MAX_DISTANCE + 1, dtype=jnp.int32)
    return rel_bias.astype(F32)[_t5_bucket(rel)]


def _win_attn_kernel(q_ref, kp_ref, kc_ref, kn_ref, vp_ref, vc_ref, vn_ref, bias_ref, sink_ref, o_ref):
    i = pl.program_id(1)
    nb = pl.num_programs(1)
    col = lax.broadcasted_iota(jnp.int32, (1, 3 * BLOCK), 1)
    invalid = ((col < BLOCK) & (i == 0)) | ((col >= 2 * BLOCK) & (i == nb - 1))
    scale = HEAD_DIM ** -0.5
    for hk in range(A_KV_HEADS):
        ks = slice(hk * HEAD_DIM, (hk + 1) * HEAD_DIM)
        qh = jnp.concatenate(
            [q_ref[:, (hk * A_GROUP + g) * HEAD_DIM:(hk * A_GROUP + g + 1) * HEAD_DIM]
             for g in range(A_GROUP)], axis=0)
        kb = jnp.concatenate([kp_ref[:, ks], kc_ref[:, ks], kn_ref[:, ks]], axis=0)
        vb = jnp.concatenate([vp_ref[:, ks], vc_ref[:, ks], vn_ref[:, ks]], axis=0)
        s = _dot_nt(qh, kb) * scale + bias_ref[hk]
        s = jnp.where(invalid, -jnp.inf, s)
        sink = sink_ref[hk]
        m = jnp.maximum(jnp.max(s, axis=-1, keepdims=True), sink)
        p = jnp.exp(s - m)
        den = jnp.sum(p, axis=-1, keepdims=True) + jnp.exp(sink - m)
        p = (p / den).astype(BF16)
        o = _dot(p, vb)
        for g in range(A_GROUP):
            h = hk * A_GROUP + g
            o_ref[:, h * HEAD_DIM:(h + 1) * HEAD_DIM] = o[g * BLOCK:(g + 1) * BLOCK].astype(o_ref.dtype)


def window_attention(z, sinks, tab):
    B, S, _ = z.shape
    nb = S // BLOCK
    q_loc = jnp.arange(BLOCK)[:, None]
    k_loc = jnp.arange(3 * BLOCK)[None, :] - BLOCK
    rel = k_loc - q_loc
    bias = jnp.where((jnp.abs(rel) <= WINDOW)[..., None],
                     tab[jnp.clip(rel, -MAX_DISTANCE, MAX_DISTANCE) + MAX_DISTANCE], -jnp.inf)
    bias = jnp.moveaxis(bias, -1, 0).reshape(A_KV_HEADS, A_GROUP * BLOCK, 3 * BLOCK)
    sink = jnp.broadcast_to(sinks.astype(F32).reshape(A_KV_HEADS, A_GROUP, 1, 1),
                            (A_KV_HEADS, A_GROUP, BLOCK, 1)).reshape(A_KV_HEADS, A_GROUP * BLOCK, 1)
    kcol = A_Q // A_KV
    vcol = (A_Q + A_KV) // A_KV
    prev = lambda b, i: (b, jnp.maximum(i - 1, 0))
    nxt = lambda b, i: (b, jnp.minimum(i + 1, nb - 1))
    return pl.pallas_call(
        _win_attn_kernel,
        grid=(B, nb),
        in_specs=[
            pl.BlockSpec((None, BLOCK, A_Q), lambda b, i: (b, i, 0)),
            pl.BlockSpec((None, BLOCK, A_KV), lambda b, i: (*prev(b, i), kcol)),
            pl.BlockSpec((None, BLOCK, A_KV), lambda b, i: (b, i, kcol)),
            pl.BlockSpec((None, BLOCK, A_KV), lambda b, i: (*nxt(b, i), kcol)),
            pl.BlockSpec((None, BLOCK, A_KV), lambda b, i: (*prev(b, i), vcol)),
            pl.BlockSpec((None, BLOCK, A_KV), lambda b, i: (b, i, vcol)),
            pl.BlockSpec((None, BLOCK, A_KV), lambda b, i: (*nxt(b, i), vcol)),
            pl.BlockSpec((A_KV_HEADS, A_GROUP * BLOCK, 3 * BLOCK), lambda b, i: (0, 0, 0)),
            pl.BlockSpec((A_KV_HEADS, A_GROUP * BLOCK, 1), lambda b, i: (0, 0, 0)),
        ],
        out_specs=pl.BlockSpec((None, BLOCK, A_Q), lambda b, i: (b, i, 0)),
        out_shape=jax.ShapeDtypeStruct((B, S, A_Q), BF16),
        compiler_params=_cp("parallel", "arbitrary"),
        name="window_attn",
    )(z, z, z, z, z, z, z, bias, sink)


CONV_TS = 512
CONV_TC = 512
CONV_HALO = 16


def _conv_kernel(prev_ref, cur_ref, next_ref, w_ref, o_ref, ext_ref):
    i = pl.program_id(1)
    j = pl.program_id(2)
    ns = pl.num_programs(1)
    ts = cur_ref.shape[0]
    pv = jnp.where(i == 0, 0.0, prev_ref[...].astype(F32))
    nx = jnp.where(i == ns - 1, 0.0, next_ref[...].astype(F32))
    ext_ref[0:CONV_HALO, :] = pv
    ext_ref[CONV_HALO:CONV_HALO + ts, :] = cur_ref[...].astype(F32)
    ext_ref[CONV_HALO + ts:, :] = nx
    half = CONV_WIDTH // 2
    acc = None
    for t in range(CONV_WIDTH):
        term = w_ref[t:t + 1, :] * ext_ref[pl.ds(CONV_HALO - half + t, ts), :]
        acc = term if acc is None else acc + term
    y = _silu(acc)
    heads_per_step = CONV_TC // B_DK
    q_steps = B_QK // CONV_TC

    def l2(scale):
        for hh in range(heads_per_step):
            seg = y[:, hh * B_DK:(hh + 1) * B_DK]
            r = lax.rsqrt(jnp.sum(seg * seg, axis=-1, keepdims=True) + EPS)
            o_ref[:, hh * B_DK:(hh + 1) * B_DK] = seg * (r * scale)

    @pl.when(j < q_steps)
    def _():
        l2(B_DK ** -0.5)

    @pl.when((j >= q_steps) & (j < 2 * q_steps))
    def _():
        l2(1.0)

    @pl.when(j >= 2 * q_steps)
    def _():
        o_ref[...] = y


def conv_qkv(z, conv_w):
    B, S, _ = z.shape
    ts = min(CONV_TS, S)
    c0 = (A_Q + 2 * A_KV) // CONV_TC
    hb = ts // CONV_HALO
    nh = S // CONV_HALO
    return pl.pallas_call(
        _conv_kernel,
        grid=(B, S // ts, B_CONV // CONV_TC),
        in_specs=[
            pl.BlockSpec((None, CONV_HALO, CONV_TC), lambda b, i, j: (b, jnp.maximum(i * hb - 1, 0), c0 + j)),
            pl.BlockSpec((None, ts, CONV_TC), lambda b, i, j: (b, i, c0 + j)),
            pl.BlockSpec((None, CONV_HALO, CONV_TC), lambda b, i, j: (b, jnp.minimum((i + 1) * hb, nh - 1), c0 + j)),
            pl.BlockSpec((CONV_WIDTH, CONV_TC), lambda b, i, j: (0, j)),
        ],
        out_specs=pl.BlockSpec((None, ts, CONV_TC), lambda b, i, j: (b, i, j)),
        out_shape=jax.ShapeDtypeStruct((B, S, B_CONV), F32),
        scratch_shapes=[pltpu.VMEM((ts + 2 * CONV_HALO, CONV_TC), F32)],
        compiler_params=_cp("parallel", "parallel", "arbitrary"),
        name="conv_qkv",
    )(z, z, z, conv_w.astype(F32))


def _bdot(a, b):
    return _dot(a.astype(DELTA_MXU), b.astype(DELTA_MXU), DELTA_PREC)


def _delta_chains(ch, eye, lmask_ref):
    n = ch[0]["q"].shape[0]
    for c in ch:
        c["decay"] = jnp.exp(jnp.where(c["causal"], c["gc_col"] - c["gc_row"], -jnp.inf))
        c["kb"] = c["k"] * c["beta"]
        c["k16"] = c["k"].astype(DELTA_MXU)
    for c in ch:
        c["m"] = jnp.where(c["strict"], _dot_nt(c["kb"].astype(DELTA_MXU), c["k16"], DELTA_PREC) * c["decay"], 0.0)
    for c in ch:
        c["inv"] = eye - c["m"] * lmask_ref[0]
    for l in range(1, lmask_ref.shape[0]):
        for c in ch:
            c["i16"] = c["inv"].astype(DELTA_MXU)
            c["p"] = _bdot(c["i16"], c["m"] * lmask_ref[l])
        for c in ch:
            c["inv"] = c["inv"] - _bdot(c["p"], c["i16"])
    for c in ch:
        c["eg"] = jnp.exp(c["gc_col"])
        c["sol"] = _bdot(c["inv"], jnp.concatenate([c["v"] * c["beta"], c["kb"] * c["eg"]], axis=1))
        c["a"] = _dot_nt(c["q"].astype(DELTA_MXU), c["k16"], DELTA_PREC) * c["decay"]
    for c in ch:
        w = c["sol"][:, B_DV:]
        c["ws"] = _bdot(jnp.concatenate([w, c["q"] * c["eg"]], axis=0), c["state"])
    out = []
    for c in ch:
        v_new = c["sol"][:, :B_DV] - c["ws"][:n]
        v16 = v_new.astype(DELTA_MXU)
        o = c["ws"][n:] + _bdot(c["a"], v16)
        k_tail = (c["k"] * jnp.exp(c["g_end"] - c["gc_col"])).astype(DELTA_MXU)
        state = c["state"] * jnp.exp(c["g_end"]) + lax.dot_general(
            k_tail, v16, (((0,), (0,)), ((), ())), precision=DELTA_PREC, preferred_element_type=F32)
        out.append((o, state))
    return out


def _delta_kernel(qf_ref, kf_ref, vf_ref, glf_ref, gtf_ref, qb_ref, kb_ref, vb_ref, glb_ref, gtb_ref,
                  arow_ref, drow_ref, acol_ref, dcol_ref, lmask_ref, of_ref, ob_ref, st_ref):
    c = pl.program_id(2)

    @pl.when(c == 0)
    def _():
        st_ref[...] = jnp.zeros_like(st_ref)

    CHUNK = DELTA_CH
    ri = lax.broadcasted_iota(jnp.int32, (CHUNK, CHUNK), 0)
    ci = lax.broadcasted_iota(jnp.int32, (CHUNK, CHUNK), 1)
    eye = (ri == ci).astype(F32)
    lower = (ri >= ci).astype(F32)
    upper = (ri <= ci).astype(F32)
    hb = DELTA_HB

    chains, sinks = [], []
    for d, (q_ref, k_ref, v_ref, gl_ref, gt_ref, o_ref) in enumerate(
            ((qf_ref, kf_ref, vf_ref, glf_ref, gtf_ref, of_ref),
             (qb_ref, kb_ref, vb_ref, glb_ref, gtb_ref, ob_ref))):
        fwd = d == 0
        gl = gl_ref[...]
        gt = gt_ref[...]
        beta_all = _sigmoid(gl)
        g_all = arow_ref[...] * _softplus(gl + drow_ref[...])
        gT_all = acol_ref[...] * _softplus(gt + dcol_ref[...])
        if fwd:
            gc_all = _dot(lower, g_all, HI)
            gcT_all = _dot(gT_all, upper, HI)
            causal, strict = ri >= ci, ri > ci
        else:
            gc_all = _dot(upper, g_all, HI)
            gcT_all = _dot(gT_all, lower, HI)
            causal, strict = ri <= ci, ri < ci
        end = CHUNK - 1 if fwd else 0
        for hh in range(hb):
            bl = d * hb + hh
            al = (2 + d) * hb + hh
            sl = slice(hh * B_DK, (hh + 1) * B_DK)
            gc_col = gc_all[:, al:al + 1]
            chains.append(dict(
                q=q_ref[:, sl], k=k_ref[:, sl], v=v_ref[:, sl], beta=beta_all[:, bl:bl + 1],
                gc_col=gc_col, gc_row=gcT_all[al:al + 1, :], g_end=gc_col[end:end + 1, :],
                state=st_ref[d * hb + hh], causal=causal, strict=strict))
            sinks.append((o_ref, sl, d * hb + hh))
    for (o, st), (o_ref, sl, si) in zip(_delta_chains(chains, eye, lmask_ref), sinks):
        o_ref[:, sl] = o
        st_ref[si] = st


def delta_rule(qkv, gl, a_log, dt_bias):
    B, S, _ = qkv.shape
    CHUNK = DELTA_CH
    n = S // CHUNK
    hb, hg = DELTA_HB, DELTA_HG
    ri = jnp.arange(CHUNK)[:, None]
    ci = jnp.arange(CHUNK)[None, :]
    lmask = jnp.stack([((ri >> (l + 1)) == (ci >> (l + 1))) & ((ri >> l) != (ci >> l))
                       for l in range(int(math.log2(CHUNK)))]).astype(F32)
    gw = 4 * hb
    g4 = gl[..., :N_GATE].reshape(B, S, 4, hg, hb)
    g4 = jnp.transpose(g4, (0, 3, 1, 2, 4)).reshape(B, hg, S, gw)
    g_rows = jnp.pad(g4, ((0, 0), (0, 0), (0, 0), (0, LANE - gw)))
    g_cols = jnp.transpose(g4.reshape(B, hg, n, CHUNK, gw), (0, 1, 2, 4, 3))
    neg_a = -jnp.exp(a_log.astype(F32)).reshape(2, hg, hb)
    dtb = dt_bias.astype(F32).reshape(2, hg, hb)
    zeros = jnp.zeros((2, hg, hb), F32)
    a4 = jnp.transpose(jnp.concatenate([zeros, neg_a], 0), (1, 0, 2)).reshape(hg, gw)
    d4 = jnp.transpose(jnp.concatenate([zeros, dtb], 0), (1, 0, 2)).reshape(hg, gw)
    arow = jnp.pad(a4, ((0, 0), (0, LANE - gw))).reshape(hg, 1, LANE)
    drow = jnp.pad(d4, ((0, 0), (0, LANE - gw))).reshape(hg, 1, LANE)
    acol = a4.reshape(hg, gw, 1)
    dcol = d4.reshape(hg, gw, 1)
    cw = hb * B_DK
    kq, kk, kv = 0, B_QK // cw, 2 * B_QK // cw

    def seq_specs(cidx):
        return [
            pl.BlockSpec((None, CHUNK, cw), lambda b, g, c: (b, cidx(c), kq + g)),
            pl.BlockSpec((None, CHUNK, cw), lambda b, g, c: (b, cidx(c), kk + g)),
            pl.BlockSpec((None, CHUNK, cw), lambda b, g, c: (b, cidx(c), kv + g)),
            pl.BlockSpec((None, None, CHUNK, LANE), lambda b, g, c: (b, g, cidx(c), 0)),
            pl.BlockSpec((None, None, None, gw, CHUNK), lambda b, g, c: (b, g, cidx(c), 0, 0)),
        ]

    fwd_idx = lambda c: c
    bwd_idx = lambda c: n - 1 - c
    par_specs = [
        pl.BlockSpec((None, 1, LANE), lambda b, g, c: (g, 0, 0)),
        pl.BlockSpec((None, 1, LANE), lambda b, g, c: (g, 0, 0)),
        pl.BlockSpec((None, gw, 1), lambda b, g, c: (g, 0, 0)),
        pl.BlockSpec((None, gw, 1), lambda b, g, c: (g, 0, 0)),
        pl.BlockSpec(lmask.shape, lambda b, g, c: (0, 0, 0)),
    ]
    o_f, o_b = pl.pallas_call(
        _delta_kernel,
        grid=(B, hg, n),
        in_specs=seq_specs(fwd_idx) + seq_specs(bwd_idx) + par_specs,
        out_specs=[
            pl.BlockSpec((None, CHUNK, cw), lambda b, g, c: (b, fwd_idx(c), g)),
            pl.BlockSpec((None, CHUNK, cw), lambda b, g, c: (b, bwd_idx(c), g)),
        ],
        out_shape=[jax.ShapeDtypeStruct((B, S, B_V), F32)] * 2,
        scratch_shapes=[pltpu.VMEM((2 * hb, B_DK, B_DV), F32)],
        compiler_params=_cp("parallel", "parallel", "arbitrary"),
        name="delta_rule",
    )(qkv, qkv, qkv, g_rows, g_cols, qkv, qkv, qkv, g_rows, g_cols, arow, drow, acol, dcol, lmask)
    return o_f, o_b


GATE_TC = 1024


def _delta_out_kernel(of_ref, ob_ref, zb_ref, g_ref, o_ref):
    for hh in range(GATE_TC // B_DV):
        sl = slice(hh * B_DV, (hh + 1) * B_DV)
        ob = of_ref[:, sl] + ob_ref[:, sl]
        y = _rms_rows(ob, g_ref[...])
        o_ref[:, sl] = (y * _silu(zb_ref[:, sl].astype(F32))).astype(o_ref.dtype)


def delta_out(o_f, o_b, z2d, onorm, *, tm=512):
    T = o_f.shape[0]
    tm = min(tm, T)
    c0 = (A_Q + 2 * A_KV + B_CONV) // GATE_TC
    return pl.pallas_call(
        _delta_out_kernel,
        grid=(T // tm, B_V // GATE_TC),
        in_specs=[
            pl.BlockSpec((tm, GATE_TC), lambda i, j: (i, j)),
            pl.BlockSpec((tm, GATE_TC), lambda i, j: (i, j)),
            pl.BlockSpec((tm, GATE_TC), lambda i, j: (i, c0 + j)),
            pl.BlockSpec((1, B_DV), lambda i, j: (0, 0)),
        ],
        out_specs=pl.BlockSpec((tm, GATE_TC), lambda i, j: (i, j)),
        out_shape=jax.ShapeDtypeStruct((T, B_V), BF16),
        compiler_params=_cp("parallel", "arbitrary"),
        name="delta_out",
    )(o_f, o_b, z2d, onorm.reshape(1, B_DV).astype(F32))


DIFF_TQ = 1024
DIFF_TK = 512


def _diff_attn_kernel(lam_ref, far_ref, q_ref, k_ref, v_ref, b_ref, g_ref, o_ref, m_ref, l_ref, acc_ref, *, out_scale, tk):
    h = pl.program_id(1)
    qi = pl.program_id(2)
    nk = k_ref.shape[0] // tk
    r = q_ref.shape[0] // tk
    base = qi * r

    m_ref[...] = jnp.full_like(m_ref, -jnp.inf)
    l_ref[...] = jnp.zeros_like(l_ref)
    acc_ref[...] = jnp.zeros_like(acc_ref)

    def step(kj, near_dj, far_const):
        rows = pl.ds(pl.multiple_of(kj * tk, tk), tk)
        v = v_ref[rows, :]
        units = [(c, t) for c in range(r) for t in range(2)]

        def scores(c, t):
            sl = slice(t * C_QK_DIM, (t + 1) * C_QK_DIM)
            return _dot_nt(q_ref[c * tk:(c + 1) * tk, sl], k_ref[rows, sl])

        s_next = scores(*units[0])
        for i, (c, t) in enumerate(units):
            s = s_next
            if i + 1 < len(units):
                s_next = scores(*units[i + 1])
            qr = slice(c * tk, (c + 1) * tk)
            off = None if near_dj is None else near_dj - c
            if off is None:
                const = far_const
            elif abs(off) <= 1:
                const = None
                s = s + b_ref[off + 1]
            else:
                const = far_ref[0 if off < 0 else 1, h]
            m_loc = jnp.max(s, axis=-1, keepdims=True)
            if const is not None:
                m_loc = m_loc + const
            m_old = m_ref[t, qr, :]
            m_new = jnp.maximum(m_old, m_loc)
            alpha = jnp.exp2(m_old - m_new)
            shift = m_new if const is None else m_new - const
            p = jnp.exp2(s - jnp.concatenate([shift] * (tk // LANE), axis=1))
            psum = p[:, :LANE]
            for cc in range(1, tk // LANE):
                psum = psum + p[:, cc * LANE:(cc + 1) * LANE]
            l_ref[t, qr, :] = alpha * l_ref[t, qr, :] + psum
            acc_ref[t, qr, :] = (jnp.concatenate([alpha] * (C_V_DIM // LANE), axis=1) * acc_ref[t, qr, :]
                                 + _dot(p.astype(BF16), v))
            m_ref[t, qr, :] = m_new

    def far_left(kj, carry):
        step(kj, None, far_ref[0, h])
        return carry

    def far_right(kj, carry):
        step(kj, None, far_ref[1, h])
        return carry

    lax.fori_loop(0, jnp.maximum(base - 1, 0), far_left, 0)
    for dj in range(-1, r + 1):
        @pl.when((base + dj >= 0) & (base + dj < nk))
        def _():
            step(base + dj, dj, None)
    lax.fori_loop(jnp.minimum(base + r + 1, nk), nk, far_right, 0)

    lam = lam_ref[0, 0]
    r0 = 1.0 / jnp.sum(l_ref[0], axis=-1, keepdims=True)
    r1 = 1.0 / jnp.sum(l_ref[1], axis=-1, keepdims=True)
    o = acc_ref[0] * r0 - acc_ref[1] * (lam * r1)
    o_ref[...] = (_rms_rows(o, g_ref[...]) * out_scale).astype(o_ref.dtype)


def diff_attention(z, lam, tab, subln, lam_init):
    B, S, _ = z.shape
    tq, tk = min(DIFF_TQ, S), min(DIFF_TK, S)
    assert tk > MAX_DISTANCE and tq % tk == 0
    nq = S // tq
    r = tq // tk
    tab2 = tab.T * LOG2E
    nd = 3
    period = 2 * tk + 1
    u = jnp.arange(period)[None, :]
    d = jnp.arange(-1, 2)[:, None]
    idx = jnp.clip(d * tk + u - tk, -MAX_DISTANCE, MAX_DISTANCE) + MAX_DISTANCE
    sig = tab2[:, idx]
    skew = jnp.tile(sig, (1, 1, tk))[..., :tk * (period - 1)].reshape(C_HEADS, nd, tk, period - 1)
    btile = skew[..., tk:2 * tk]
    far = jnp.stack([tab2[:, 0], tab2[:, -1]])
    hw = 2 * C_QK_DIM
    return pl.pallas_call(
        functools.partial(_diff_attn_kernel, out_scale=1.0 - lam_init, tk=tk),
        grid=(B, C_HEADS, nq),
        in_specs=[
            pl.BlockSpec(memory_space=pltpu.SMEM),
            pl.BlockSpec(memory_space=pltpu.SMEM),
            pl.BlockSpec((None, tq, hw), lambda b, h, i: (b, i, h)),
            pl.BlockSpec((None, S, hw), lambda b, h, i: (b, 0, C_HEADS + h)),
            pl.BlockSpec((None, S, hw), lambda b, h, i: (b, 0, 2 * C_HEADS + h)),
            pl.BlockSpec((None, nd, tk, tk), lambda b, h, i: (h, 0, 0, 0)),
            pl.BlockSpec((1, C_V_DIM), lambda b, h, i: (0, 0)),
        ],
        out_specs=pl.BlockSpec((None, tq, C_V_DIM), lambda b, h, i: (b, i, h)),
        out_shape=jax.ShapeDtypeStruct((B, S, C_HEADS * C_V_DIM), BF16),
        scratch_shapes=[pltpu.VMEM((2, tq, LANE), F32), pltpu.VMEM((2, tq, LANE), F32),
                        pltpu.VMEM((2, tq, C_V_DIM), F32)],
        compiler_params=_cp("parallel", "parallel", "arbitrary"),
        name="diff_attn",
    )(lam.reshape(1, 1).astype(F32), far, z, z, z, btile, subln.reshape(1, C_V_DIM).astype(F32))


def _mem_attn_kernel(x_ref, g_ref, wq_ref, kv_ref, wo_ref, o_ref):
    x = x_ref[...]
    h = _rms_rows(x, g_ref[...]).astype(BF16)
    q = _dot(h, wq_ref[...]).astype(BF16)
    outs = []
    for hd in range(X_HEADS):
        sl = slice(hd * X_HEAD_DIM, (hd + 1) * X_HEAD_DIM)
        s = _dot_nt(q[:, sl], kv_ref[:, sl]) * (X_HEAD_DIM ** -0.5)
        m = jnp.max(s, axis=-1, keepdims=True)
        e = jnp.exp(s - m)
        p = (e / jnp.sum(e, axis=-1, keepdims=True)).astype(BF16)
        outs.append(_dot(p, kv_ref[:, X_WIDTH + hd * X_HEAD_DIM:X_WIDTH + (hd + 1) * X_HEAD_DIM]))
    o = jnp.concatenate(outs, axis=1).astype(BF16)
    o_ref[...] = x + _dot(o, wo_ref[...])


def memory_attention(x, kv, g, wq, wo, *, tm=256):
    B, S, D = x.shape
    tm = min(tm, S)
    return pl.pallas_call(
        _mem_attn_kernel,
        grid=(B, S // tm),
        in_specs=[
            pl.BlockSpec((None, tm, D), lambda b, i: (b, i, 0)),
            pl.BlockSpec((1, D), lambda b, i: (0, 0)),
            pl.BlockSpec((D, X_WIDTH), lambda b, i: (0, 0)),
            pl.BlockSpec((None, MEM_TOKENS, 2 * X_WIDTH), lambda b, i: (b, 0, 0)),
            pl.BlockSpec((X_WIDTH, D), lambda b, i: (0, 0)),
        ],
        out_specs=pl.BlockSpec((None, tm, D), lambda b, i: (b, i, 0)),
        out_shape=jax.ShapeDtypeStruct((B, S, D), F32),
        compiler_params=_cp("parallel", "arbitrary"),
        name="mem_attn",
    )(x, g.reshape(1, D).astype(F32), wq, kv, wo)


def _first_max_onehot(vals, rows):
    m = jnp.max(vals, axis=0, keepdims=True)
    idx = jnp.min(jnp.where(vals == m, rows, vals.shape[0]), axis=0, keepdims=True)
    return rows == idx


def _router_kernel(x_ref, g_ref, w_ref, b_ref, o_ref, h_ref):
    tm = x_ref.shape[0]

    def body(r, c):
        sl = pl.ds(pl.multiple_of(r * LANE, LANE), LANE)
        h_ref[sl, :] = _rms_rows(x_ref[sl, :], g_ref[...])
        return c
    lax.fori_loop(0, tm // LANE, body, 0)
    logits = _dot(h_ref[...], w_ref[...], HI)
    lt = jnp.transpose(logits)[:N_EXPERTS, :]
    scores = _sigmoid(lt)
    biased = scores + b_ref[...]
    per_group = N_EXPERTS // N_GROUPS
    rows8 = lax.broadcasted_iota(jnp.int32, (per_group, tm), 0)
    gscore = []
    for gi in range(N_GROUPS):
        blk = biased[gi * per_group:(gi + 1) * per_group]
        first = _first_max_onehot(blk, rows8)
        m1 = jnp.max(blk, axis=0, keepdims=True)
        m2 = jnp.max(jnp.where(first, -jnp.inf, blk), axis=0, keepdims=True)
        gscore.append(m1 + m2)
    gscore = jnp.concatenate(gscore, axis=0)
    growi = lax.broadcasted_iota(jnp.int32, (N_GROUPS, tm), 0)
    gsel = jnp.zeros((N_GROUPS, tm), jnp.bool_)
    work = gscore
    for _ in range(TOPK_GROUPS):
        oh = _first_max_onehot(work, growi)
        gsel = gsel | oh
        work = jnp.where(oh, -jnp.inf, work)
    gself = gsel.astype(F32)
    masked = jnp.concatenate(
        [jnp.where(gself[gi:gi + 1] > 0.0, biased[gi * per_group:(gi + 1) * per_group], -jnp.inf)
         for gi in range(N_GROUPS)], axis=0)
    erow = lax.broadcasted_iota(jnp.int32, (N_EXPERTS, tm), 0)
    esel = jnp.zeros((N_EXPERTS, tm), jnp.bool_)
    work = masked
    for _ in range(TOP_K):
        oh = _first_max_onehot(work, erow)
        esel = esel | oh
        work = jnp.where(oh, -jnp.inf, work)
    w = jnp.where(esel, scores, 0.0)
    gates = w / jnp.sum(w, axis=0, keepdims=True) * ROUTED_SCALE
    n_shared = SHARED_FF // EXPERT_FF
    full = jnp.concatenate([gates, jnp.ones((8, tm), F32), jnp.zeros((LANE - N_EXPERTS - 8, tm), F32)], axis=0)
    frow = lax.broadcasted_iota(jnp.int32, (LANE, tm), 0)
    full = jnp.where(frow < N_EXPERTS + n_shared, full, 0.0)
    o_ref[...] = jnp.transpose(full)


def route_tokens(x, g, w_router, r_bias, *, tm=512):
    T, D = x.shape
    tm = min(tm, T)
    w = jnp.pad(w_router.astype(F32), ((0, 0), (0, LANE - N_EXPERTS)))
    return pl.pallas_call(
        _router_kernel,
        grid=(T // tm,),
        in_specs=[
            pl.BlockSpec((tm, D), lambda i: (i, 0)),
            pl.BlockSpec((1, D), lambda i: (0, 0)),
            pl.BlockSpec((D, LANE), lambda i: (0, 0)),
            pl.BlockSpec((N_EXPERTS, 1), lambda i: (0, 0)),
        ],
        out_specs=pl.BlockSpec((tm, LANE), lambda i: (i, 0)),
        out_shape=jax.ShapeDtypeStruct((T, LANE), F32),
        scratch_shapes=[pltpu.VMEM((tm, D), F32)],
        compiler_params=_cp("parallel"),
        name="moe_router",
    )(x, g.reshape(1, D).astype(F32), w, r_bias.astype(F32).reshape(N_EXPERTS, 1))


def _pattern_rank_table():
    def revolving(n, k):
        if k == 0:
            return [[]]
        if k == n:
            return [list(range(n))]
        return revolving(n - 1, k) + [c + [n - 1] for c in reversed(revolving(n - 1, k - 1))]

    combos = [sum(1 << g for g in c) for c in revolving(N_GROUPS, TOPK_GROUPS)]
    table = []
    for m in range(1 << N_GROUPS):
        ranks = [i for i, c in enumerate(combos) if (c & m) == m]
        table.append(min(ranks) if ranks else len(combos))
    return table


_PATTERN_RANK = _pattern_rank_table()


def _moe_plan(gates, tm):
    T = gates.shape[0]
    nt = T // tm
    per_group = N_EXPERTS // N_GROUPS
    sel = gates[:, :N_EXPERTS] > 0.0
    gsel = jnp.any(sel.reshape(T, N_GROUPS, per_group), axis=-1)
    gmask = jnp.sum(gsel.astype(jnp.int32) << jnp.arange(N_GROUPS, dtype=jnp.int32), axis=-1)
    order = jnp.argsort(jnp.asarray(_PATTERN_RANK, jnp.int32)[gmask]).astype(jnp.int32)
    gates_s = gates[order]
    nr = N_EXPERTS // MOE_EPS
    used = jnp.any((gates_s[:, :N_EXPERTS] > 0.0).reshape(nt, tm, nr, MOE_EPS), axis=(1, 3))
    used = jnp.concatenate([used, jnp.ones((nt, N_PAIRS - nr), jnp.bool_)], axis=1)
    fetch = lax.cummin(jnp.where(used, jnp.arange(N_PAIRS, dtype=jnp.int32), N_PAIRS - 1), axis=1, reverse=True)
    return order, gates_s, used.astype(jnp.int32).reshape(-1), fetch.astype(jnp.int32).reshape(-1)


def _moe_kernel(order_ref, used_ref, fetch_ref, x_hbm, g_ref, gate_ref, wgu_ref, wd_ref, gf_ref, o_hbm,
                acc_ref, xin_ref, h_ref, sem_ref, *, tm, final_norm):
    i = pl.program_id(0)
    p = pl.program_id(1)
    nt = pl.num_programs(0)
    npair = pl.num_programs(1)

    def row_in(tile, r):
        return pltpu.make_async_copy(x_hbm.at[pl.ds(order_ref[tile * tm + r], 1)], xin_ref.at[pl.ds(r, 1)],
                                     sem_ref.at[0])

    def row_out(tile, r):
        return pltpu.make_async_copy(acc_ref.at[pl.ds(r, 1)], o_hbm.at[pl.ds(order_ref[tile * tm + r], 1)],
                                     sem_ref.at[1])

    def for_rows(fn):
        def body(r, c):
            fn(r)
            return c
        lax.fori_loop(0, tm, body, 0, unroll=8)

    @pl.when(p == 0)
    def _():
        @pl.when(i == 0)
        def _():
            for_rows(lambda r: row_in(i, r).start())
        for_rows(lambda r: row_in(i, r).wait())

        def body(r, c):
            sl = pl.ds(pl.multiple_of(r * LANE, LANE), LANE)
            h_ref[sl, :] = _rms_rows(xin_ref[sl, :], g_ref[...]).astype(BF16)
            return c
        lax.fori_loop(0, tm // LANE, body, 0)

        @pl.when(i > 0)
        def _():
            for_rows(lambda r: row_out(i - 1, r).wait())
        acc_ref[...] = xin_ref[...]

        @pl.when(i + 1 < nt)
        def _():
            for_rows(lambda r: row_in(i + 1, r).start())

    @pl.when(used_ref[i * npair + p] != 0)
    def _():
        gu = _dot(h_ref[...], wgu_ref[...])
        gates = gate_ref[...]
        lane = lax.broadcasted_iota(jnp.int32, gates.shape, 1)
        F = EXPERT_FF
        hid = []
        for e in range(MOE_EPS):
            ge = jnp.sum(jnp.where(lane == MOE_EPS * p + e, gates, 0.0), axis=1, keepdims=True)
            hid.append(_silu(gu[:, e * F:(e + 1) * F]) * gu[:, (MOE_EPS + e) * F:(MOE_EPS + 1 + e) * F] * ge)
        hid = jnp.concatenate(hid, axis=1).astype(BF16)
        acc_ref[...] += _dot(hid, wd_ref[...])

    @pl.when(p == npair - 1)
    def _():
        if final_norm:
            def body(r, c):
                sl = pl.ds(pl.multiple_of(r * LANE, LANE), LANE)
                acc_ref[sl, :] = _rms_rows(acc_ref[sl, :], gf_ref[...])
                return c
            lax.fori_loop(0, tm // LANE, body, 0)
        for_rows(lambda r: row_out(i, r).start())

        @pl.when(i == nt - 1)
        def _():
            for_rows(lambda r: row_out(i, r).wait())


def moe_experts(x, g, gates, w_gu, w_d, g_final, *, layer, final_norm, tm=512):
    T, D = x.shape
    tm = min(tm, T)
    order, gates_s, used, fetch = _moe_plan(gates, tm)
    wblock = lambda i, p, order, used, fetch: (layer * N_PAIRS + fetch[i * N_PAIRS + p], 0, 0)
    return pl.pallas_call(
        functools.partial(_moe_kernel, tm=tm, final_norm=final_norm),
        grid_spec=pltpu.PrefetchScalarGridSpec(
            num_scalar_prefetch=3,
            grid=(T // tm, N_PAIRS),
            in_specs=[
                pl.BlockSpec(memory_space=pl.ANY),
                pl.BlockSpec((1, D), lambda i, p, *_: (0, 0)),
                pl.BlockSpec((tm, LANE), lambda i, p, *_: (i, 0)),
                pl.BlockSpec((None, D, 2 * MOE_EPS * EXPERT_FF), wblock),
                pl.BlockSpec((None, MOE_EPS * EXPERT_FF, D), wblock),
                pl.BlockSpec((1, D), lambda i, p, *_: (0, 0)),
            ],
            out_specs=pl.BlockSpec(memory_space=pl.ANY),
            scratch_shapes=[pltpu.VMEM((tm, D), F32), pltpu.VMEM((tm, D), F32), pltpu.VMEM((tm, D), BF16),
                            pltpu.SemaphoreType.DMA((2,))],
        ),
        out_shape=jax.ShapeDtypeStruct((T, D), F32),
        compiler_params=_cp("arbitrary", "arbitrary"),
        name="moe_experts",
    )(order, used, fetch, x, g.reshape(1, D).astype(F32), gates_s, w_gu, w_d, g_final.reshape(1, D).astype(F32))


def _moe_weights(w_gate, w_up, w_down, s_gate, s_up, s_down):
    L, _, D, F = w_gate.shape
    ns = SHARED_FF // F

    def grouped(w, s):
        sh = jnp.transpose(s.reshape(L, D, ns, F), (0, 2, 1, 3))
        return jnp.concatenate([w, sh], axis=1).reshape(L, N_PAIRS, MOE_EPS, D, F)

    gu = jnp.concatenate([grouped(w_gate, s_gate), grouped(w_up, s_up)], axis=2)
    w_gu = jnp.transpose(gu, (0, 1, 3, 2, 4)).reshape(L * N_PAIRS, D, 2 * MOE_EPS * F).astype(BF16)
    w_d = jnp.concatenate([w_down, s_down.reshape(L, ns, F, D)], axis=1).astype(BF16)
    return w_gu, w_d.reshape(L * N_PAIRS, MOE_EPS * F, D)


def kernel(x_prompt, x_sample, mem_prompt, mem_sample, rel_bias, norm_mix, norm_mem, norm_memkv, norm_ffn, norm_final, ev_w_in, ev_sinks, ev_conv, ev_a_log, ev_dt_bias, ev_onorm, ev_w_out, od_w_in, od_lambda, od_subln, od_w_out, mx_wq, mx_wkv, mx_wo, moe_router, moe_bias, moe_w_gate, moe_w_up, moe_w_down, sh_w_gate, sh_w_up, sh_w_down):
    tab = _bias_by_rel(rel_bias)

    w_gu, w_d = _moe_weights(moe_w_gate, moe_w_up, moe_w_down, sh_w_gate, sh_w_up, sh_w_down)
    layers = []
    for l in range(DEPTH):
        i = l // 2
        lw = {}
        if l % 2 == 0:
            w_in = ev_w_in[i]
            lw["w_in"] = w_in[:, :EVEN_MAIN].astype(BF16)
            lw["w_gate_tail"] = jnp.pad(w_in[:, EVEN_MAIN:], ((0, 0), (0, LANE - N_GATE))).astype(BF16)
            lw["w_out_a"] = ev_w_out[i][:A_Q].astype(BF16)
            lw["w_out_b"] = ev_w_out[i][A_Q:].astype(BF16)
        else:
            w_in = od_w_in[i]
            lw["w_in"] = jnp.concatenate([w_in[:, :C_QK] * (C_QK_DIM ** -0.5 * LOG2E), w_in[:, C_QK:]],
                                         axis=1).astype(BF16)
            lw["w_out"] = od_w_out[i].astype(BF16)
            lp = od_lambda[i].astype(F32)
            lam_init = 0.8 - 0.6 * math.exp(-0.3 * l)
            lw["lam_init"] = lam_init
            lw["lam"] = jnp.exp(jnp.sum(lp[0] * lp[1])) - jnp.exp(jnp.sum(lp[2] * lp[3])) + lam_init
        lw["wq"] = mx_wq[l].astype(BF16)
        lw["wkv"] = mx_wkv[l].astype(BF16)
        lw["wo"] = mx_wo[l].astype(BF16)
        layers.append(lw)

    def run(x3, mem):
        B, S, D = x3.shape
        T = B * S
        x = x3.reshape(T, D)
        mem2 = mem.reshape(B * MEM_TOKENS, D)
        for l in range(DEPTH):
            lw = layers[l]
            i = l // 2
            if l % 2 == 0:
                z, gl = norm_mm(x, norm_mix[l], lw["w_in"], lw["w_gate_tail"])
                z3 = z.reshape(B, S, EVEN_MAIN)
                out_a = window_attention(z3, ev_sinks[i], tab)
                qkv = conv_qkv(z3, ev_conv[i])
                o_f, o_b = delta_rule(qkv, gl.reshape(B, S, LANE), ev_a_log[i], ev_dt_bias[i])
                out_b = delta_out(o_f.reshape(T, B_V), o_b.reshape(T, B_V), z, ev_onorm[i])
                x = mm_res(x, [out_a.reshape(T, A_Q), out_b], [lw["w_out_a"], lw["w_out_b"]])
            else:
                z = norm_mm(x, norm_mix[l], lw["w_in"])
                o = diff_attention(z.reshape(B, S, 3 * C_QK), lw["lam"], tab, od_subln[i], lw["lam_init"])
                x = mm_res(x, [o.reshape(T, C_HEADS * C_V_DIM)], [lw["w_out"]])
            kv = norm_mm(mem2, norm_memkv[l], lw["wkv"])
            x = memory_attention(x.reshape(B, S, D), kv.reshape(B, MEM_TOKENS, 2 * X_WIDTH),
                                 norm_mem[l], lw["wq"], lw["wo"]).reshape(T, D)
            gates = route_tokens(x, norm_ffn[l], moe_router[l], moe_bias[l])
            x = moe_experts(x, norm_ffn[l], gates, w_gu, w_d, norm_final,
                            layer=l, final_norm=(l == DEPTH - 1))
        return x.reshape(B, S, D)

    return (run(x_prompt, mem_prompt), run(x_sample, mem_sample))
```

```python
import functools
import math

import jax
import jax.numpy as jnp
from jax import lax
from jax.experimental import pallas as pl
from jax.experimental.pallas import tpu as pltpu

F32 = jnp.float32
BF16 = jnp.bfloat16

D_MODEL = 4096
DEPTH = 2
HEAD_DIM = 128
BLOCK = 128
WINDOW = 128
A_HEADS = 16
A_KV_HEADS = 4
A_GROUP = A_HEADS // A_KV_HEADS
B_HEADS = 16
B_DK = 128
B_DV = 128
CONV_WIDTH = 5
CHUNK = 64
C_HEADS = 16
C_QK_DIM = 128
C_V_DIM = 2 * C_QK_DIM
N_BUCKETS = 32
MAX_DISTANCE = 128
MEM_TOKENS = 256
X_HEADS = 4
X_HEAD_DIM = 128
N_EXPERTS = 64
TOP_K = 8
N_GROUPS = 8
TOPK_GROUPS = 4
EXPERT_FF = 128
SHARED_FF = 512
ROUTED_SCALE = 2.5
EPS = 1e-6

A_Q = A_HEADS * HEAD_DIM
A_KV = A_KV_HEADS * HEAD_DIM
B_QK = B_HEADS * B_DK
B_V = B_HEADS * B_DV
B_CONV = 2 * B_QK + B_V
EVEN_MAIN = A_Q + 2 * A_KV + B_CONV + B_V
N_GATE = 4 * B_HEADS
C_QK = 2 * C_HEADS * C_QK_DIM
X_WIDTH = X_HEADS * X_HEAD_DIM

LANE = 128
VMEM_LIMIT = 56 * 1024 * 1024
HI = lax.Precision.HIGHEST
LOG2E = math.log2(math.e)

DELTA_HB = 4
DELTA_HG = B_HEADS // DELTA_HB
DELTA_CH = 256
DELTA_MXU = BF16
DELTA_PREC = None
MOE_EPS = 4
N_PAIRS = (N_EXPERTS + SHARED_FF // EXPERT_FF) // MOE_EPS


def _cp(*sem):
    return pltpu.CompilerParams(dimension_semantics=sem, vmem_limit_bytes=VMEM_LIMIT)


def _dot(a, b, prec=None):
    return lax.dot_general(a, b, (((1,), (0,)), ((), ())), precision=prec,
                           preferred_element_type=F32)


def _dot_nt(a, b, prec=None):
    return lax.dot_general(a, b, (((1,), (1,)), ((), ())), precision=prec,
                           preferred_element_type=F32)


def _sigmoid(x):
    return 1.0 / (1.0 + jnp.exp(-x))


def _silu(x):
    return x * _sigmoid(x)


def _softplus(x):
    return jnp.maximum(x, 0.0) + jnp.log(1.0 + jnp.exp(-jnp.abs(x)))


def _rms_rows(x, g):
    ms = jnp.mean(x * x, axis=-1, keepdims=True)
    return x * lax.rsqrt(ms + EPS) * g


def _norm_mm_kernel(x_ref, g_ref, w_ref, *rest, tm, has_tail):
    if has_tail:
        wt_ref, o_ref, ot_ref, h_ref = rest
    else:
        o_ref, h_ref = rest

    @pl.when(pl.program_id(1) == 0)
    def _():
        def body(r, c):
            sl = pl.ds(pl.multiple_of(r * LANE, LANE), LANE)
            h_ref[sl, :] = _rms_rows(x_ref[sl, :], g_ref[...]).astype(BF16)
            return c
        lax.fori_loop(0, tm // LANE, body, 0)
        if has_tail:
            ot_ref[...] = _dot(h_ref[...], wt_ref[...])

    o_ref[...] = _dot(h_ref[...], w_ref[...]).astype(o_ref.dtype)


def norm_mm(x, g, w, w_tail=None, *, tm=1024, tn=512, out_dtype=BF16):
    T, D = x.shape
    N = w.shape[1]
    tm = min(tm, T)
    assert T % tm == 0 and N % tn == 0 and tm % LANE == 0
    has_tail = w_tail is not None
    in_specs = [
        pl.BlockSpec((tm, D), lambda i, j: (i, 0), pipeline_mode=pl.Buffered(1)),
        pl.BlockSpec((1, D), lambda i, j: (0, 0)),
        pl.BlockSpec((D, tn), lambda i, j: (0, j)),
    ]
    args = [x, g.reshape(1, D).astype(F32), w]
    out_shape = [jax.ShapeDtypeStruct((T, N), out_dtype)]
    out_specs = [pl.BlockSpec((tm, tn), lambda i, j: (i, j))]
    if has_tail:
        nt = w_tail.shape[1]
        in_specs.append(pl.BlockSpec((D, nt), lambda i, j: (0, 0)))
        args.append(w_tail)
        out_shape.append(jax.ShapeDtypeStruct((T, nt), F32))
        out_specs.append(pl.BlockSpec((tm, nt), lambda i, j: (i, 0)))
    outs = pl.pallas_call(
        functools.partial(_norm_mm_kernel, tm=tm, has_tail=has_tail),
        grid=(T // tm, N // tn),
        in_specs=in_specs, out_specs=out_specs, out_shape=out_shape,
        scratch_shapes=[pltpu.VMEM((tm, D), BF16)],
        compiler_params=_cp("parallel", "arbitrary"),
        name="norm_mm",
    )(*args)
    return outs if has_tail else outs[0]


def _mm_res_kernel(*refs, n_a):
    res_ref = refs[0]
    a_refs = refs[1:1 + n_a]
    w_refs = refs[1 + n_a:1 + 2 * n_a]
    o_ref = refs[-1]
    acc = res_ref[...]
    for a, w in zip(a_refs, w_refs):
        acc = acc + _dot(a[...], w[...])
    o_ref[...] = acc


def mm_res(res, a_list, w_list, *, tm=1024, tn=512):
    T, N = res.shape
    tm = min(tm, T)
    assert T % tm == 0 and N % tn == 0
    n_a = len(a_list)
    in_specs = [pl.BlockSpec((tm, tn), lambda i, j: (i, j))]
    for a in a_list:
        in_specs.append(pl.BlockSpec((tm, a.shape[1]), lambda i, j: (i, 0)))
    for w in w_list:
        in_specs.append(pl.BlockSpec((w.shape[0], tn), lambda i, j: (0, j)))
    return pl.pallas_call(
        functools.partial(_mm_res_kernel, n_a=n_a),
        grid=(T // tm, N // tn),
        in_specs=in_specs,
        out_specs=pl.BlockSpec((tm, tn), lambda i, j: (i, j)),
        out_shape=jax.ShapeDtypeStruct((T, N), F32),
        compiler_params=_cp("parallel", "arbitrary"),
        name="mm_res",
    )(res, *a_list, *w_list)


def _t5_bucket(rel):
    nb = N_BUCKETS // 2
    max_exact = nb // 2
    ret = jnp.where(rel > 0, nb, 0)
    n = jnp.abs(rel)
    n_f = jnp.maximum(n, max_exact).astype(F32)
    large = max_exact + (jnp.log(n_f / max_exact) / math.log(MAX_DISTANCE / max_exact)
                         * (nb - max_exact)).astype(jnp.int32)
    large = jnp.minimum(large, nb - 1)
    return ret + jnp.where(n < max_exact, n, large)


def _bias_by_rel(rel_bias):
    rel = jnp.arange(-MAX_DISTANCE, MAX_DISTANCE + 1, dtype=jnp.int32)
    return rel_bias.astype(F32)[_t5_bucket(rel)]


def _win_attn_kernel(q_ref, kp_ref, kc_ref, kn_ref, vp_ref, vc_ref, vn_ref, bias_ref, sink_ref, o_ref):
    i = pl.program_id(1)
    nb = pl.num_programs(1)
    col = lax.broadcasted_iota(jnp.int32, (1, 3 * BLOCK), 1)
    invalid = ((col < BLOCK) & (i == 0)) | ((col >= 2 * BLOCK) & (i == nb - 1))
    scale = HEAD_DIM ** -0.5
    for hk in range(A_KV_HEADS):
        ks = slice(hk * HEAD_DIM, (hk + 1) * HEAD_DIM)
        qh = jnp.concatenate(
            [q_ref[:, (hk * A_GROUP + g) * HEAD_DIM:(hk * A_GROUP + g + 1) * HEAD_DIM]
             for g in range(A_GROUP)], axis=0)
        kb = jnp.concatenate([kp_ref[:, ks], kc_ref[:, ks], kn_ref[:, ks]], axis=0)
        vb = jnp.concatenate([vp_ref[:, ks], vc_ref[:, ks], vn_ref[:, ks]], axis=0)
        s = _dot_nt(qh, kb) * scale + bias_ref[hk]
        s = jnp.where(invalid, -jnp.inf, s)
        sink = sink_ref[hk]
        m = jnp.maximum(jnp.max(s, axis=-1, keepdims=True), sink)
        p = jnp.exp(s - m)
        den = jnp.sum(p, axis=-1, keepdims=True) + jnp.exp(sink - m)
        p = (p / den).astype(BF16)
        o = _dot(p, vb)
        for g in range(A_GROUP):
            h = hk * A_GROUP + g
            o_ref[:, h * HEAD_DIM:(h + 1) * HEAD_DIM] = o[g * BLOCK:(g + 1) * BLOCK].astype(o_ref.dtype)


def window_attention(z, sinks, tab):
    B, S, _ = z.shape
    nb = S // BLOCK
    q_loc = jnp.arange(BLOCK)[:, None]
    k_loc = jnp.arange(3 * BLOCK)[None, :] - BLOCK
    rel = k_loc - q_loc
    bias = jnp.where((jnp.abs(rel) <= WINDOW)[..., None],
                     tab[jnp.clip(rel, -MAX_DISTANCE, MAX_DISTANCE) + MAX_DISTANCE], -jnp.inf)
    bias = jnp.moveaxis(bias, -1, 0).reshape(A_KV_HEADS, A_GROUP * BLOCK, 3 * BLOCK)
    sink = jnp.broadcast_to(sinks.astype(F32).reshape(A_KV_HEADS, A_GROUP, 1, 1),
                            (A_KV_HEADS, A_GROUP, BLOCK, 1)).reshape(A_KV_HEADS, A_GROUP * BLOCK, 1)
    kcol = A_Q // A_KV
    vcol = (A_Q + A_KV) // A_KV
    prev = lambda b, i: (b, jnp.maximum(i - 1, 0))
    nxt = lambda b, i: (b, jnp.minimum(i + 1, nb - 1))
    return pl.pallas_call(
        _win_attn_kernel,
        grid=(B, nb),
        in_specs=[
            pl.BlockSpec((None, BLOCK, A_Q), lambda b, i: (b, i, 0)),
            pl.BlockSpec((None, BLOCK, A_KV), lambda b, i: (*prev(b, i), kcol)),
            pl.BlockSpec((None, BLOCK, A_KV), lambda b, i: (b, i, kcol)),
            pl.BlockSpec((None, BLOCK, A_KV), lambda b, i: (*nxt(b, i), kcol)),
            pl.BlockSpec((None, BLOCK, A_KV), lambda b, i: (*prev(b, i), vcol)),
            pl.BlockSpec((None, BLOCK, A_KV), lambda b, i: (b, i, vcol)),
            pl.BlockSpec((None, BLOCK, A_KV), lambda b, i: (*nxt(b, i), vcol)),
            pl.BlockSpec((A_KV_HEADS, A_GROUP * BLOCK, 3 * BLOCK), lambda b, i: (0, 0, 0)),
            pl.BlockSpec((A_KV_HEADS, A_GROUP * BLOCK, 1), lambda b, i: (0, 0, 0)),
        ],
        out_specs=pl.BlockSpec((None, BLOCK, A_Q), lambda b, i: (b, i, 0)),
        out_shape=jax.ShapeDtypeStruct((B, S, A_Q), BF16),
        compiler_params=_cp("parallel", "arbitrary"),
        name="window_attn",
    )(z, z, z, z, z, z, z, bias, sink)


CONV_TS = 512
CONV_TC = 512
CONV_HALO = 16


def _conv_kernel(prev_ref, cur_ref, next_ref, w_ref, o_ref, ext_ref):
    i = pl.program_id(1)
    j = pl.program_id(2)
    ns = pl.num_programs(1)
    ts = cur_ref.shape[0]
    pv = jnp.where(i == 0, 0.0, prev_ref[...].astype(F32))
    nx = jnp.where(i == ns - 1, 0.0, next_ref[...].astype(F32))
    ext_ref[0:CONV_HALO, :] = pv
    ext_ref[CONV_HALO:CONV_HALO + ts, :] = cur_ref[...].astype(F32)
    ext_ref[CONV_HALO + ts:, :] = nx
    half = CONV_WIDTH // 2
    acc = None
    for t in range(CONV_WIDTH):
        term = w_ref[t:t + 1, :] * ext_ref[pl.ds(CONV_HALO - half + t, ts), :]
        acc = term if acc is None else acc + term
    y = _silu(acc)
    heads_per_step = CONV_TC // B_DK
    q_steps = B_QK // CONV_TC

    def l2(scale):
        for hh in range(heads_per_step):
            seg = y[:, hh * B_DK:(hh + 1) * B_DK]
            r = lax.rsqrt(jnp.sum(seg * seg, axis=-1, keepdims=True) + EPS)
            o_ref[:, hh * B_DK:(hh + 1) * B_DK] = seg * (r * scale)

    @pl.when(j < q_steps)
    def _():
        l2(B_DK ** -0.5)

    @pl.when((j >= q_steps) & (j < 2 * q_steps))
    def _():
        l2(1.0)

    @pl.when(j >= 2 * q_steps)
    def _():
        o_ref[...] = y


def conv_qkv(z, conv_w):
    B, S, _ = z.shape
    ts = min(CONV_TS, S)
    c0 = (A_Q + 2 * A_KV) // CONV_TC
    hb = ts // CONV_HALO
    nh = S // CONV_HALO
    return pl.pallas_call(
        _conv_kernel,
        grid=(B, S // ts, B_CONV // CONV_TC),
        in_specs=[
            pl.BlockSpec((None, CONV_HALO, CONV_TC), lambda b, i, j: (b, jnp.maximum(i * hb - 1, 0), c0 + j)),
            pl.BlockSpec((None, ts, CONV_TC), lambda b, i, j: (b, i, c0 + j)),
            pl.BlockSpec((None, CONV_HALO, CONV_TC), lambda b, i, j: (b, jnp.minimum((i + 1) * hb, nh - 1), c0 + j)),
            pl.BlockSpec((CONV_WIDTH, CONV_TC), lambda b, i, j: (0, j)),
        ],
        out_specs=pl.BlockSpec((None, ts, CONV_TC), lambda b, i, j: (b, i, j)),
        out_shape=jax.ShapeDtypeStruct((B, S, B_CONV), F32),
        scratch_shapes=[pltpu.VMEM((ts + 2 * CONV_HALO, CONV_TC), F32)],
        compiler_params=_cp("parallel", "parallel", "arbitrary"),
        name="conv_qkv",
    )(z, z, z, conv_w.astype(F32))


def _bdot(a, b):
    return _dot(a.astype(DELTA_MXU), b.astype(DELTA_MXU), DELTA_PREC)


def _delta_chains(ch, eye, lmask_ref):
    n = ch[0]["q"].shape[0]
    for c in ch:
        c["decay"] = jnp.exp(jnp.where(c["causal"], c["gc_col"] - c["gc_row"], -jnp.inf))
        c["kb"] = c["k"] * c["beta"]
        c["k16"] = c["k"].astype(DELTA_MXU)
    for c in ch:
        c["m"] = jnp.where(c["strict"], _dot_nt(c["kb"].astype(DELTA_MXU), c["k16"], DELTA_PREC) * c["decay"], 0.0)
    for c in ch:
        c["inv"] = eye - c["m"] * lmask_ref[0]
    for l in range(1, lmask_ref.shape[0]):
        for c in ch:
            c["i16"] = c["inv"].astype(DELTA_MXU)
            c["p"] = _bdot(c["i16"], c["m"] * lmask_ref[l])
        for c in ch:
            c["inv"] = c["inv"] - _bdot(c["p"], c["i16"])
    for c in ch:
        c["eg"] = jnp.exp(c["gc_col"])
        c["sol"] = _bdot(c["inv"], jnp.concatenate([c["v"] * c["beta"], c["kb"] * c["eg"]], axis=1))
        c["a"] = _dot_nt(c["q"].astype(DELTA_MXU), c["k16"], DELTA_PREC) * c["decay"]
    for c in ch:
        w = c["sol"][:, B_DV:]
        c["ws"] = _bdot(jnp.concatenate([w, c["q"] * c["eg"]], axis=0), c["state"])
    out = []
    for c in ch:
        v_new = c["sol"][:, :B_DV] - c["ws"][:n]
        v16 = v_new.astype(DELTA_MXU)
        o = c["ws"][n:] + _bdot(c["a"], v16)
        k_tail = (c["k"] * jnp.exp(c["g_end"] - c["gc_col"])).astype(DELTA_MXU)
        state = c["state"] * jnp.exp(c["g_end"]) + lax.dot_general(
            k_tail, v16, (((0,), (0,)), ((), ())), precision=DELTA_PREC, preferred_element_type=F32)
        out.append((o, state))
    return out


def _delta_kernel(qf_ref, kf_ref, vf_ref, glf_ref, gtf_ref, qb_ref, kb_ref, vb_ref, glb_ref, gtb_ref,
                  arow_ref, drow_ref, acol_ref, dcol_ref, lmask_ref, of_ref, ob_ref, st_ref):
    c = pl.program_id(2)

    @pl.when(c == 0)
    def _():
        st_ref[...] = jnp.zeros_like(st_ref)

    CHUNK = DELTA_CH
    ri = lax.broadcasted_iota(jnp.int32, (CHUNK, CHUNK), 0)
    ci = lax.broadcasted_iota(jnp.int32, (CHUNK, CHUNK), 1)
    eye = (ri == ci).astype(F32)
    lower = (ri >= ci).astype(F32)
    upper = (ri <= ci).astype(F32)
    hb = DELTA_HB

    chains, sinks = [], []
    for d, (q_ref, k_ref, v_ref, gl_ref, gt_ref, o_ref) in enumerate(
            ((qf_ref, kf_ref, vf_ref, glf_ref, gtf_ref, of_ref),
             (qb_ref, kb_ref, vb_ref, glb_ref, gtb_ref, ob_ref))):
        fwd = d == 0
        gl = gl_ref[...]
        gt = gt_ref[...]
        beta_all = _sigmoid(gl)
        g_all = arow_ref[...] * _softplus(gl + drow_ref[...])
        gT_all = acol_ref[...] * _softplus(gt + dcol_ref[...])
        if fwd:
            gc_all = _dot(lower, g_all, HI)
            gcT_all = _dot(gT_all, upper, HI)
            causal, strict = ri >= ci, ri > ci
        else:
            gc_all = _dot(upper, g_all, HI)
            gcT_all = _dot(gT_all, lower, HI)
            causal, strict = ri <= ci, ri < ci
        end = CHUNK - 1 if fwd else 0
        for hh in range(hb):
            bl = d * hb + hh
            al = (2 + d) * hb + hh
            sl = slice(hh * B_DK, (hh + 1) * B_DK)
            gc_col = gc_all[:, al:al + 1]
            chains.append(dict(
                q=q_ref[:, sl], k=k_ref[:, sl], v=v_ref[:, sl], beta=beta_all[:, bl:bl + 1],
                gc_col=gc_col, gc_row=gcT_all[al:al + 1, :], g_end=gc_col[end:end + 1, :],
                state=st_ref[d * hb + hh], causal=causal, strict=strict))
            sinks.append((o_ref, sl, d * hb + hh))
    for (o, st), (o_ref, sl, si) in zip(_delta_chains(chains, eye, lmask_ref), sinks):
        o_ref[:, sl] = o
        st_ref[si] = st


def delta_rule(qkv, gl, a_log, dt_bias):
    B, S, _ = qkv.shape
    CHUNK = DELTA_CH
    n = S // CHUNK
    hb, hg = DELTA_HB, DELTA_HG
    ri = jnp.arange(CHUNK)[:, None]
    ci = jnp.arange(CHUNK)[None, :]
    lmask = jnp.stack([((ri >> (l + 1)) == (ci >> (l + 1))) & ((ri >> l) != (ci >> l))
                       for l in range(int(math.log2(CHUNK)))]).astype(F32)
    gw = 4 * hb
    g4 = gl[..., :N_GATE].reshape(B, S, 4, hg, hb)
    g4 = jnp.transpose(g4, (0, 3, 1, 2, 4)).reshape(B, hg, S, gw)
    g_rows = jnp.pad(g4, ((0, 0), (0, 0), (0, 0), (0, LANE - gw)))
    g_cols = jnp.transpose(g4.reshape(B, hg, n, CHUNK, gw), (0, 1, 2, 4, 3))
    neg_a = -jnp.exp(a_log.astype(F32)).reshape(2, hg, hb)
    dtb = dt_bias.astype(F32).reshape(2, hg, hb)
    zeros = jnp.zeros((2, hg, hb), F32)
    a4 = jnp.transpose(jnp.concatenate([zeros, neg_a], 0), (1, 0, 2)).reshape(hg, gw)
    d4 = jnp.transpose(jnp.concatenate([zeros, dtb], 0), (1, 0, 2)).reshape(hg, gw)
    arow = jnp.pad(a4, ((0, 0), (0, LANE - gw))).reshape(hg, 1, LANE)
    drow = jnp.pad(d4, ((0, 0), (0, LANE - gw))).reshape(hg, 1, LANE)
    acol = a4.reshape(hg, gw, 1)
    dcol = d4.reshape(hg, gw, 1)
    cw = hb * B_DK
    kq, kk, kv = 0, B_QK // cw, 2 * B_QK // cw

    def seq_specs(cidx):
        return [
            pl.BlockSpec((None, CHUNK, cw), lambda b, g, c: (b, cidx(c), kq + g)),
            pl.BlockSpec((None, CHUNK, cw), lambda b, g, c: (b, cidx(c), kk + g)),
            pl.BlockSpec((None, CHUNK, cw), lambda b, g, c: (b, cidx(c), kv + g)),
            pl.BlockSpec((None, None, CHUNK, LANE), lambda b, g, c: (b, g, cidx(c), 0)),
            pl.BlockSpec((None, None, None, gw, CHUNK), lambda b, g, c: (b, g, cidx(c), 0, 0)),
        ]

    fwd_idx = lambda c: c
    bwd_idx = lambda c: n - 1 - c
    par_specs = [
        pl.BlockSpec((None, 1, LANE), lambda b, g, c: (g, 0, 0)),
        pl.BlockSpec((None, 1, LANE), lambda b, g, c: (g, 0, 0)),
        pl.BlockSpec((None, gw, 1), lambda b, g, c: (g, 0, 0)),
        pl.BlockSpec((None, gw, 1), lambda b, g, c: (g, 0, 0)),
        pl.BlockSpec(lmask.shape, lambda b, g, c: (0, 0, 0)),
    ]
    o_f, o_b = pl.pallas_call(
        _delta_kernel,
        grid=(B, hg, n),
        in_specs=seq_specs(fwd_idx) + seq_specs(bwd_idx) + par_specs,
        out_specs=[
            pl.BlockSpec((None, CHUNK, cw), lambda b, g, c: (b, fwd_idx(c), g)),
            pl.BlockSpec((None, CHUNK, cw), lambda b, g, c: (b, bwd_idx(c), g)),
        ],
        out_shape=[jax.ShapeDtypeStruct((B, S, B_V), F32)] * 2,
        scratch_shapes=[pltpu.VMEM((2 * hb, B_DK, B_DV), F32)],
        compiler_params=_cp("parallel", "parallel", "arbitrary"),
        name="delta_rule",
    )(qkv, qkv, qkv, g_rows, g_cols, qkv, qkv, qkv, g_rows, g_cols, arow, drow, acol, dcol, lmask)
    return o_f, o_b


GATE_TC = 1024


def _delta_out_kernel(of_ref, ob_ref, zb_ref, g_ref, o_ref):
    for hh in range(GATE_TC // B_DV):
        sl = slice(hh * B_DV, (hh + 1) * B_DV)
        ob = of_ref[:, sl] + ob_ref[:, sl]
        y = _rms_rows(ob, g_ref[...])
        o_ref[:, sl] = (y * _silu(zb_ref[:, sl].astype(F32))).astype(o_ref.dtype)


def delta_out(o_f, o_b, z2d, onorm, *, tm=512):
    T = o_f.shape[0]
    tm = min(tm, T)
    c0 = (A_Q + 2 * A_KV + B_CONV) // GATE_TC
    return pl.pallas_call(
        _delta_out_kernel,
        grid=(T // tm, B_V // GATE_TC),
        in_specs=[
            pl.BlockSpec((tm, GATE_TC), lambda i, j: (i, j)),
            pl.BlockSpec((tm, GATE_TC), lambda i, j: (i, j)),
            pl.BlockSpec((tm, GATE_TC), lambda i, j: (i, c0 + j)),
            pl.BlockSpec((1, B_DV), lambda i, j: (0, 0)),
        ],
        out_specs=pl.BlockSpec((tm, GATE_TC), lambda i, j: (i, j)),
        out_shape=jax.ShapeDtypeStruct((T, B_V), BF16),
        compiler_params=_cp("parallel", "arbitrary"),
        name="delta_out",
    )(o_f, o_b, z2d, onorm.reshape(1, B_DV).astype(F32))


DIFF_TQ = 1024
DIFF_TK = 512


def _diff_attn_kernel(lam_ref, far_ref, q_ref, k_ref, v_ref, b_ref, g_ref, o_ref, m_ref, l_ref, acc_ref, *, out_scale, tk):
    h = pl.program_id(1)
    qi = pl.program_id(2)
    nk = k_ref.shape[0] // tk
    r = q_ref.shape[0] // tk
    base = qi * r

    m_ref[...] = jnp.full_like(m_ref, -jnp.inf)
    l_ref[...] = jnp.zeros_like(l_ref)
    acc_ref[...] = jnp.zeros_like(acc_ref)

    def step(kj, near_dj, far_const):
        rows = pl.ds(pl.multiple_of(kj * tk, tk), tk)
        v = v_ref[rows, :]
        units = [(c, t) for c in range(r) for t in range(2)]

        def scores(c, t):
            sl = slice(t * C_QK_DIM, (t + 1) * C_QK_DIM)
            return _dot_nt(q_ref[c * tk:(c + 1) * tk, sl], k_ref[rows, sl])

        s_next = scores(*units[0])
        for i, (c, t) in enumerate(units):
            s = s_next
            if i + 1 < len(units):
                s_next = scores(*units[i + 1])
            qr = slice(c * tk, (c + 1) * tk)
            off = None if near_dj is None else near_dj - c
            if off is None:
                const = far_const
            elif abs(off) <= 1:
                const = None
                s = s + b_ref[off + 1]
            else:
                const = far_ref[0 if off < 0 else 1, h]
            m_loc = jnp.max(s, axis=-1, keepdims=True)
            if const is not None:
                m_loc = m_loc + const
            m_old = m_ref[t, qr, :]
            m_new = jnp.maximum(m_old, m_loc)
            alpha = jnp.exp2(m_old - m_new)
            shift = m_new if const is None else m_new - const
            p = jnp.exp2(s - jnp.concatenate([shift] * (tk // LANE), axis=1))
            psum = p[:, :LANE]
            for cc in range(1, tk // LANE):
                psum = psum + p[:, cc * LANE:(cc + 1) * LANE]
            l_ref[t, qr, :] = alpha * l_ref[t, qr, :] + psum
            acc_ref[t, qr, :] = (jnp.concatenate([alpha] * (C_V_DIM // LANE), axis=1) * acc_ref[t, qr, :]
                                 + _dot(p.astype(BF16), v))
            m_ref[t, qr, :] = m_new

    def far_left(kj, carry):
        step(kj, None, far_ref[0, h])
        return carry

    def far_right(kj, carry):
        step(kj, None, far_ref[1, h])
        return carry

    lax.fori_loop(0, jnp.maximum(base - 1, 0), far_left, 0)
    for dj in range(-1, r + 1):
        @pl.when((base + dj >= 0) & (base + dj < nk))
        def _():
            step(base + dj, dj, None)
    lax.fori_loop(jnp.minimum(base + r + 1, nk), nk, far_right, 0)

    lam = lam_ref[0, 0]
    r0 = 1.0 / jnp.sum(l_ref[0], axis=-1, keepdims=True)
    r1 = 1.0 / jnp.sum(l_ref[1], axis=-1, keepdims=True)
    o = acc_ref[0] * r0 - acc_ref[1] * (lam * r1)
    o_ref[...] = (_rms_rows(o, g_ref[...]) * out_scale).astype(o_ref.dtype)


def diff_attention(z, lam, tab, subln, lam_init):
    B, S, _ = z.shape
    tq, tk = min(DIFF_TQ, S), min(DIFF_TK, S)
    assert tk > MAX_DISTANCE and tq % tk == 0
    nq = S // tq
    r = tq // tk
    tab2 = tab.T * LOG2E
    nd = 3
    period = 2 * tk + 1
    u = jnp.arange(period)[None, :]
    d = jnp.arange(-1, 2)[:, None]
    idx = jnp.clip(d * tk + u - tk, -MAX_DISTANCE, MAX_DISTANCE) + MAX_DISTANCE
    sig = tab2[:, idx]
    skew = jnp.tile(sig, (1, 1, tk))[..., :tk * (period - 1)].reshape(C_HEADS, nd, tk, period - 1)
    btile = skew[..., tk:2 * tk]
    far = jnp.stack([tab2[:, 0], tab2[:, -1]])
    hw = 2 * C_QK_DIM
    return pl.pallas_call(
        functools.partial(_diff_attn_kernel, out_scale=1.0 - lam_init, tk=tk),
        grid=(B, C_HEADS, nq),
        in_specs=[
            pl.BlockSpec(memory_space=pltpu.SMEM),
            pl.BlockSpec(memory_space=pltpu.SMEM),
            pl.BlockSpec((None, tq, hw), lambda b, h, i: (b, i, h)),
            pl.BlockSpec((None, S, hw), lambda b, h, i: (b, 0, C_HEADS + h)),
            pl.BlockSpec((None, S, hw), lambda b, h, i: (b, 0, 2 * C_HEADS + h)),
            pl.BlockSpec((None, nd, tk, tk), lambda b, h, i: (h, 0, 0, 0)),
            pl.BlockSpec((1, C_V_DIM), lambda b, h, i: (0, 0)),
        ],
        out_specs=pl.BlockSpec((None, tq, C_V_DIM), lambda b, h, i: (b, i, h)),
        out_shape=jax.ShapeDtypeStruct((B, S, C_HEADS * C_V_DIM), BF16),
        scratch_shapes=[pltpu.VMEM((2, tq, LANE), F32), pltpu.VMEM((2, tq, LANE), F32),
                        pltpu.VMEM((2, tq, C_V_DIM), F32)],
        compiler_params=_cp("parallel", "parallel", "arbitrary"),
        name="diff_attn",
    )(lam.reshape(1, 1).astype(F32), far, z, z, z, btile, subln.reshape(1, C_V_DIM).astype(F32))


def _mem_attn_kernel(x_ref, g_ref, wq_ref, kv_ref, wo_ref, o_ref):
    x = x_ref[...]
    h = _rms_rows(x, g_ref[...]).astype(BF16)
    q = _dot(h, wq_ref[...]).astype(BF16)
    outs = []
    for hd in range(X_HEADS):
        sl = slice(hd * X_HEAD_DIM, (hd + 1) * X_HEAD_DIM)
        s = _dot_nt(q[:, sl], kv_ref[:, sl]) * (X_HEAD_DIM ** -0.5)
        m = jnp.max(s, axis=-1, keepdims=True)
        e = jnp.exp(s - m)
        p = (e / jnp.sum(e, axis=-1, keepdims=True)).astype(BF16)
        outs.append(_dot(p, kv_ref[:, X_WIDTH + hd * X_HEAD_DIM:X_WIDTH + (hd + 1) * X_HEAD_DIM]))
    o = jnp.concatenate(outs, axis=1).astype(BF16)
    o_ref[...] = x + _dot(o, wo_ref[...])


def memory_attention(x, kv, g, wq, wo, *, tm=256):
    B, S, D = x.shape
    tm = min(tm, S)
    return pl.pallas_call(
        _mem_attn_kernel,
        grid=(B, S // tm),
        in_specs=[
            pl.BlockSpec((None, tm, D), lambda b, i: (b, i, 0)),
            pl.BlockSpec((1, D), lambda b, i: (0, 0)),
            pl.BlockSpec((D, X_WIDTH), lambda b, i: (0, 0)),
            pl.BlockSpec((None, MEM_TOKENS, 2 * X_WIDTH), lambda b, i: (b, 0, 0)),
            pl.BlockSpec((X_WIDTH, D), lambda b, i: (0, 0)),
        ],
        out_specs=pl.BlockSpec((None, tm, D), lambda b, i: (b, i, 0)),
        out_shape=jax.ShapeDtypeStruct((B, S, D), F32),
        compiler_params=_cp("parallel", "arbitrary"),
        name="mem_attn",
    )(x, g.reshape(1, D).astype(F32), wq, kv, wo)


def _first_max_onehot(vals, rows):
    m = jnp.max(vals, axis=0, keepdims=True)
    idx = jnp.min(jnp.where(vals == m, rows, vals.shape[0]), axis=0, keepdims=True)
    return rows == idx


def _router_kernel(x_ref, g_ref, w_ref, b_ref, o_ref, h_ref):
    tm = x_ref.shape[0]

    def body(r, c):
        sl = pl.ds(pl.multiple_of(r * LANE, LANE), LANE)
        h_ref[sl, :] = _rms_rows(x_ref[sl, :], g_ref[...])
        return c
    lax.fori_loop(0, tm // LANE, body, 0)
    logits = _dot(h_ref[...], w_ref[...], HI)
    lt = jnp.transpose(logits)[:N_EXPERTS, :]
    scores = _sigmoid(lt)
    biased = scores + b_ref[...]
    per_group = N_EXPERTS // N_GROUPS
    rows8 = lax.broadcasted_iota(jnp.int32, (per_group, tm), 0)
    gscore = []
    for gi in range(N_GROUPS):
        blk = biased[gi * per_group:(gi + 1) * per_group]
        first = _first_max_onehot(blk, rows8)
        m1 = jnp.max(blk, axis=0, keepdims=True)
        m2 = jnp.max(jnp.where(first, -jnp.inf, blk), axis=0, keepdims=True)
        gscore.append(m1 + m2)
    gscore = jnp.concatenate(gscore, axis=0)
    growi = lax.broadcasted_iota(jnp.int32, (N_GROUPS, tm), 0)
    gsel = jnp.zeros((N_GROUPS, tm), jnp.bool_)
    work = gscore
    for _ in range(TOPK_GROUPS):
        oh = _first_max_onehot(work, growi)
        gsel = gsel | oh
        work = jnp.where(oh, -jnp.inf, work)
    gself = gsel.astype(F32)
    masked = jnp.concatenate(
        [jnp.where(gself[gi:gi + 1] > 0.0, biased[gi * per_group:(gi + 1) * per_group], -jnp.inf)
         for gi in range(N_GROUPS)], axis=0)
    erow = lax.broadcasted_iota(jnp.int32, (N_EXPERTS, tm), 0)
    esel = jnp.zeros((N_EXPERTS, tm), jnp.bool_)
    work = masked
    for _ in range(TOP_K):
        oh = _first_max_onehot(work, erow)
        esel = esel | oh
        work = jnp.where(oh, -jnp.inf, work)
    w = jnp.where(esel, scores, 0.0)
    gates = w / jnp.sum(w, axis=0, keepdims=True) * ROUTED_SCALE
    n_shared = SHARED_FF // EXPERT_FF
    full = jnp.concatenate([gates, jnp.ones((8, tm), F32), jnp.zeros((LANE - N_EXPERTS - 8, tm), F32)], axis=0)
    frow = lax.broadcasted_iota(jnp.int32, (LANE, tm), 0)
    full = jnp.where(frow < N_EXPERTS + n_shared, full, 0.0)
    o_ref[...] = jnp.transpose(full)


def route_tokens(x, g, w_router, r_bias, *, tm=512):
    T, D = x.shape
    tm = min(tm, T)
    w = jnp.pad(w_router.astype(F32), ((0, 0), (0, LANE - N_EXPERTS)))
    return pl.pallas_call(
        _router_kernel,
        grid=(T // tm,),
        in_specs=[
            pl.BlockSpec((tm, D), lambda i: (i, 0)),
            pl.BlockSpec((1, D), lambda i: (0, 0)),
            pl.BlockSpec((D, LANE), lambda i: (0, 0)),
            pl.BlockSpec((N_EXPERTS, 1), lambda i: (0, 0)),
        ],
        out_specs=pl.BlockSpec((tm, LANE), lambda i: (i, 0)),
        out_shape=jax.ShapeDtypeStruct((T, LANE), F32),
        scratch_shapes=[pltpu.VMEM((tm, D), F32)],
        compiler_params=_cp("parallel"),
        name="moe_router",
    )(x, g.reshape(1, D).astype(F32), w, r_bias.astype(F32).reshape(N_EXPERTS, 1))


def _pattern_rank_table():
    def revolving(n, k):
        if k == 0:
            return [[]]
        if k == n:
            return [list(range(n))]
        return revolving(n - 1, k) + [c + [n - 1] for c in reversed(revolving(n - 1, k - 1))]

    combos = [sum(1 << g for g in c) for c in revolving(N_GROUPS, TOPK_GROUPS)]
    table = []
    for m in range(1 << N_GROUPS):
        ranks = [i for i, c in enumerate(combos) if (c & m) == m]
        table.append(min(ranks) if ranks else len(combos))
    return table


_PATTERN_RANK = _pattern_rank_table()


def _moe_plan(gates, tm):
    T = gates.shape[0]
    nt = T // tm
    per_group = N_EXPERTS // N_GROUPS
    sel = gates[:, :N_EXPERTS] > 0.0
    gsel = jnp.any(sel.reshape(T, N_GROUPS, per_group), axis=-1)
    gmask = jnp.sum(gsel.astype(jnp.int32) << jnp.arange(N_GROUPS, dtype=jnp.int32), axis=-1)
    order = jnp.argsort(jnp.asarray(_PATTERN_RANK, jnp.int32)[gmask]).astype(jnp.int32)
    gates_s = gates[order]
    nr = N_EXPERTS // MOE_EPS
    used = jnp.any((gates_s[:, :N_EXPERTS] > 0.0).reshape(nt, tm, nr, MOE_EPS), axis=(1, 3))
    used = jnp.concatenate([used, jnp.ones((nt, N_PAIRS - nr), jnp.bool_)], axis=1)
    fetch = lax.cummin(jnp.where(used, jnp.arange(N_PAIRS, dtype=jnp.int32), N_PAIRS - 1), axis=1, reverse=True)
    return order, gates_s, used.astype(jnp.int32).reshape(-1), fetch.astype(jnp.int32).reshape(-1)


def _moe_kernel(order_ref, used_ref, fetch_ref, *refs, tm, final_norm, bounds):
    ng = len(bounds) - 1
    x_hbm = refs[:ng]
    g_ref, gate_ref, wgu_ref, wd_ref, gf_ref = refs[ng:ng + 5]
    o_hbm = refs[ng + 5:2 * ng + 5]
    acc_ref, xin_ref, h_ref, sem_ref = refs[2 * ng + 5:]
    i = pl.program_id(0)
    p = pl.program_id(1)
    nt = pl.num_programs(0)
    npair = pl.num_programs(1)

    def per_group(tile, r, make):
        idx = order_ref[tile * tm + r]
        for k in range(ng):
            @pl.when((idx >= bounds[k]) & (idx < bounds[k + 1]))
            def _():
                make(k, idx - bounds[k])

    def row_in(k, row, r):
        return pltpu.make_async_copy(x_hbm[k].at[pl.ds(row, 1)], xin_ref.at[pl.ds(r, 1)], sem_ref.at[0])

    def row_out(k, row, r):
        return pltpu.make_async_copy(acc_ref.at[pl.ds(r, 1)], o_hbm[k].at[pl.ds(row, 1)], sem_ref.at[1])

    def for_rows(fn):
        def body(r, c):
            fn(r)
            return c
        lax.fori_loop(0, tm, body, 0, unroll=8)

    def gather_start(tile):
        for_rows(lambda r: per_group(tile, r, lambda k, row: row_in(k, row, r).start()))

    def gather_wait(tile):
        for_rows(lambda r: per_group(tile, r, lambda k, row: row_in(k, row, r).wait()))

    def scatter_start(tile):
        for_rows(lambda r: per_group(tile, r, lambda k, row: row_out(k, row, r).start()))

    def scatter_wait(tile):
        for_rows(lambda r: per_group(tile, r, lambda k, row: row_out(k, row, r).wait()))

    @pl.when(p == 0)
    def _():
        @pl.when(i == 0)
        def _():
            gather_start(i)
        gather_wait(i)

        def body(r, c):
            sl = pl.ds(pl.multiple_of(r * LANE, LANE), LANE)
            h_ref[sl, :] = _rms_rows(xin_ref[sl, :], g_ref[...]).astype(BF16)
            return c
        lax.fori_loop(0, tm // LANE, body, 0)

        @pl.when(i > 0)
        def _():
            scatter_wait(i - 1)
        acc_ref[...] = xin_ref[...]

        @pl.when(i + 1 < nt)
        def _():
            gather_start(i + 1)

    @pl.when(used_ref[i * npair + p] != 0)
    def _():
        gu = _dot(h_ref[...], wgu_ref[...])
        gates = gate_ref[...]
        lane = lax.broadcasted_iota(jnp.int32, gates.shape, 1)
        F = EXPERT_FF
        hid = []
        for e in range(MOE_EPS):
            ge = jnp.sum(jnp.where(lane == MOE_EPS * p + e, gates, 0.0), axis=1, keepdims=True)
            hid.append(_silu(gu[:, e * F:(e + 1) * F]) * gu[:, (MOE_EPS + e) * F:(MOE_EPS + 1 + e) * F] * ge)
        hid = jnp.concatenate(hid, axis=1).astype(BF16)
        acc_ref[...] += _dot(hid, wd_ref[...])

    @pl.when(p == npair - 1)
    def _():
        if final_norm:
            def body(r, c):
                sl = pl.ds(pl.multiple_of(r * LANE, LANE), LANE)
                acc_ref[sl, :] = _rms_rows(acc_ref[sl, :], gf_ref[...])
                return c
            lax.fori_loop(0, tm // LANE, body, 0)
        scatter_start(i)

        @pl.when(i == nt - 1)
        def _():
            scatter_wait(i)


def moe_experts(xs, g, gates, w_gu, w_d, g_final, *, layer, final_norm, tm=512):
    D = xs[0].shape[1]
    bounds = [0]
    for x in xs:
        bounds.append(bounds[-1] + x.shape[0])
    T = bounds[-1]
    assert T % tm == 0
    order, gates_s, used, fetch = _moe_plan(jnp.concatenate(gates, axis=0), tm)
    wblock = lambda i, p, order, used, fetch: (layer * N_PAIRS + fetch[i * N_PAIRS + p], 0, 0)
    any_spec = pl.BlockSpec(memory_space=pl.ANY)
    return pl.pallas_call(
        functools.partial(_moe_kernel, tm=tm, final_norm=final_norm, bounds=tuple(bounds)),
        grid_spec=pltpu.PrefetchScalarGridSpec(
            num_scalar_prefetch=3,
            grid=(T // tm, N_PAIRS),
            in_specs=[any_spec] * len(xs) + [
                pl.BlockSpec((1, D), lambda i, p, *_: (0, 0)),
                pl.BlockSpec((tm, LANE), lambda i, p, *_: (i, 0)),
                pl.BlockSpec((None, D, 2 * MOE_EPS * EXPERT_FF), wblock),
                pl.BlockSpec((None, MOE_EPS * EXPERT_FF, D), wblock),
                pl.BlockSpec((1, D), lambda i, p, *_: (0, 0)),
            ],
            out_specs=[any_spec] * len(xs),
            scratch_shapes=[pltpu.VMEM((tm, D), F32), pltpu.VMEM((tm, D), F32), pltpu.VMEM((tm, D), BF16),
                            pltpu.SemaphoreType.DMA((2,))],
        ),
        out_shape=[jax.ShapeDtypeStruct(x.shape, F32) for x in xs],
        compiler_params=_cp("arbitrary", "arbitrary"),
        name="moe_experts",
    )(order, used, fetch, *xs, g.reshape(1, D).astype(F32), gates_s, w_gu, w_d, g_final.reshape(1, D).astype(F32))


def _moe_weights(w_gate, w_up, w_down, s_gate, s_up, s_down):
    L, _, D, F = w_gate.shape
    ns = SHARED_FF // F

    def grouped(w, s):
        sh = jnp.transpose(s.reshape(L, D, ns, F), (0, 2, 1, 3))
        return jnp.concatenate([w, sh], axis=1).reshape(L, N_PAIRS, MOE_EPS, D, F)

    gu = jnp.concatenate([grouped(w_gate, s_gate), grouped(w_up, s_up)], axis=2)
    w_gu = jnp.transpose(gu, (0, 1, 3, 2, 4)).reshape(L * N_PAIRS, D, 2 * MOE_EPS * F).astype(BF16)
    w_d = jnp.concatenate([w_down, s_down.reshape(L, ns, F, D)], axis=1).astype(BF16)
    return w_gu, w_d.reshape(L * N_PAIRS, MOE_EPS * F, D)


def kernel(x_prompt, x_sample, mem_prompt, mem_sample, rel_bias, norm_mix, norm_mem, norm_memkv, norm_ffn, norm_final, ev_w_in, ev_sinks, ev_conv, ev_a_log, ev_dt_bias, ev_onorm, ev_w_out, od_w_in, od_lambda, od_subln, od_w_out, mx_wq, mx_wkv, mx_wo, moe_router, moe_bias, moe_w_gate, moe_w_up, moe_w_down, sh_w_gate, sh_w_up, sh_w_down):
    tab = _bias_by_rel(rel_bias)

    w_gu, w_d = _moe_weights(moe_w_gate, moe_w_up, moe_w_down, sh_w_gate, sh_w_up, sh_w_down)
    layers = []
    for l in range(DEPTH):
        i = l // 2
        lw = {}
        if l % 2 == 0:
            w_in = ev_w_in[i]
            lw["w_in"] = w_in[:, :EVEN_MAIN].astype(BF16)
            lw["w_gate_tail"] = jnp.pad(w_in[:, EVEN_MAIN:], ((0, 0), (0, LANE - N_GATE))).astype(BF16)
            lw["w_out_a"] = ev_w_out[i][:A_Q].astype(BF16)
            lw["w_out_b"] = ev_w_out[i][A_Q:].astype(BF16)
        else:
            w_in = od_w_in[i]
            lw["w_in"] = jnp.concatenate([w_in[:, :C_QK] * (C_QK_DIM ** -0.5 * LOG2E), w_in[:, C_QK:]],
                                         axis=1).astype(BF16)
            lw["w_out"] = od_w_out[i].astype(BF16)
            lp = od_lambda[i].astype(F32)
            lam_init = 0.8 - 0.6 * math.exp(-0.3 * l)
            lw["lam_init"] = lam_init
            lw["lam"] = jnp.exp(jnp.sum(lp[0] * lp[1])) - jnp.exp(jnp.sum(lp[2] * lp[3])) + lam_init
        lw["wq"] = mx_wq[l].astype(BF16)
        lw["wkv"] = mx_wkv[l].astype(BF16)
        lw["wo"] = mx_wo[l].astype(BF16)
        layers.append(lw)

    def mix_and_route(x, mem2, B, S, l):
        T, D = x.shape
        lw = layers[l]
        i = l // 2
        if l % 2 == 0:
            z, gl = norm_mm(x, norm_mix[l], lw["w_in"], lw["w_gate_tail"])
            z3 = z.reshape(B, S, EVEN_MAIN)
            out_a = window_attention(z3, ev_sinks[i], tab)
            qkv = conv_qkv(z3, ev_conv[i])
            o_f, o_b = delta_rule(qkv, gl.reshape(B, S, LANE), ev_a_log[i], ev_dt_bias[i])
            out_b = delta_out(o_f.reshape(T, B_V), o_b.reshape(T, B_V), z, ev_onorm[i])
            x = mm_res(x, [out_a.reshape(T, A_Q), out_b], [lw["w_out_a"], lw["w_out_b"]])
        else:
            z = norm_mm(x, norm_mix[l], lw["w_in"])
            o = diff_attention(z.reshape(B, S, 3 * C_QK), lw["lam"], tab, od_subln[i], lw["lam_init"])
            x = mm_res(x, [o.reshape(T, C_HEADS * C_V_DIM)], [lw["w_out"]])
        kv = norm_mm(mem2, norm_memkv[l], lw["wkv"])
        x = memory_attention(x.reshape(B, S, D), kv.reshape(B, MEM_TOKENS, 2 * X_WIDTH),
                             norm_mem[l], lw["wq"], lw["wo"]).reshape(T, D)
        return x, route_tokens(x, norm_ffn[l], moe_router[l], moe_bias[l])

    groups = [(x_prompt, mem_prompt), (x_sample, mem_sample)]
    shapes = [x.shape for x, _ in groups]
    xs = [x.reshape(-1, x.shape[-1]) for x, _ in groups]
    mems = [m.reshape(-1, m.shape[-1]) for _, m in groups]
    for l in range(DEPTH):
        routed = [mix_and_route(x, m, sh[0], sh[1], l) for x, m, sh in zip(xs, mems, shapes)]
        xs = moe_experts([x for x, _ in routed], norm_ffn[l], [gt for _, gt in routed], w_gu, w_d, norm_final,
                         layer=l, final_norm=(l == DEPTH - 1))
    return tuple(x.reshape(sh) for x, sh in zip(xs, shapes))
```

```python
import functools
import math

import jax
import jax.numpy as jnp
from jax import lax
from jax.experimental import pallas as pl
from jax.experimental.pallas import tpu as pltpu

F32 = jnp.float32
BF16 = jnp.bfloat16

D_MODEL = 4096
DEPTH = 2
HEAD_DIM = 128
BLOCK = 128
WINDOW = 128
A_HEADS = 16
A_KV_HEADS = 4
A_GROUP = A_HEADS // A_KV_HEADS
B_HEADS = 16
B_DK = 128
B_DV = 128
CONV_WIDTH = 5
CHUNK = 64
C_HEADS = 16
C_QK_DIM = 128
C_V_DIM = 2 * C_QK_DIM
N_BUCKETS = 32
MAX_DISTANCE = 128
MEM_TOKENS = 256
X_HEADS = 4
X_HEAD_DIM = 128
N_EXPERTS = 64
TOP_K = 8
N_GROUPS = 8
TOPK_GROUPS = 4
EXPERT_FF = 128
SHARED_FF = 512
ROUTED_SCALE = 2.5
EPS = 1e-6

A_Q = A_HEADS * HEAD_DIM
A_KV = A_KV_HEADS * HEAD_DIM
B_QK = B_HEADS * B_DK
B_V = B_HEADS * B_DV
B_CONV = 2 * B_QK + B_V
EVEN_MAIN = A_Q + 2 * A_KV + B_CONV + B_V
N_GATE = 4 * B_HEADS
C_QK = 2 * C_HEADS * C_QK_DIM
X_WIDTH = X_HEADS * X_HEAD_DIM

LANE = 128
VMEM_LIMIT = 56 * 1024 * 1024
HI = lax.Precision.HIGHEST
LOG2E = math.log2(math.e)

DELTA_HB = 4
DELTA_HG = B_HEADS // DELTA_HB
DELTA_CH = 256
DELTA_MXU = BF16
DELTA_PREC = None
MOE_EPS = 4
N_PAIRS = (N_EXPERTS + SHARED_FF // EXPERT_FF) // MOE_EPS


def _cp(*sem):
    return pltpu.CompilerParams(dimension_semantics=sem, vmem_limit_bytes=VMEM_LIMIT)


def _dot(a, b, prec=None):
    return lax.dot_general(a, b, (((1,), (0,)), ((), ())), precision=prec,
                           preferred_element_type=F32)


def _dot_nt(a, b, prec=None):
    return lax.dot_general(a, b, (((1,), (1,)), ((), ())), precision=prec,
                           preferred_element_type=F32)


def _sigmoid(x):
    return 1.0 / (1.0 + jnp.exp(-x))


def _silu(x):
    return x * _sigmoid(x)


def _softplus(x):
    return jnp.maximum(x, 0.0) + jnp.log(1.0 + jnp.exp(-jnp.abs(x)))


def _rms_rows(x, g):
    ms = jnp.mean(x * x, axis=-1, keepdims=True)
    return x * lax.rsqrt(ms + EPS) * g


def _norm_mm_kernel(x_ref, g_ref, w_ref, *rest, tm, has_tail):
    if has_tail:
        wt_ref, o_ref, ot_ref, h_ref = rest
    else:
        o_ref, h_ref = rest

    @pl.when(pl.program_id(1) == 0)
    def _():
        def body(r, c):
            sl = pl.ds(pl.multiple_of(r * LANE, LANE), LANE)
            h_ref[sl, :] = _rms_rows(x_ref[sl, :], g_ref[...]).astype(BF16)
            return c
        lax.fori_loop(0, tm // LANE, body, 0)
        if has_tail:
            ot_ref[...] = _dot(h_ref[...], wt_ref[...])

    o_ref[...] = _dot(h_ref[...], w_ref[...]).astype(o_ref.dtype)


def norm_mm(x, g, w, w_tail=None, *, tm=1024, tn=512, out_dtype=BF16):
    T, D = x.shape
    N = w.shape[1]
    tm = min(tm, T)
    assert T % tm == 0 and N % tn == 0 and tm % LANE == 0
    has_tail = w_tail is not None
    in_specs = [
        pl.BlockSpec((tm, D), lambda i, j: (i, 0), pipeline_mode=pl.Buffered(1)),
        pl.BlockSpec((1, D), lambda i, j: (0, 0)),
        pl.BlockSpec((D, tn), lambda i, j: (0, j)),
    ]
    args = [x, g.reshape(1, D).astype(F32), w]
    out_shape = [jax.ShapeDtypeStruct((T, N), out_dtype)]
    out_specs = [pl.BlockSpec((tm, tn), lambda i, j: (i, j))]
    if has_tail:
        nt = w_tail.shape[1]
        in_specs.append(pl.BlockSpec((D, nt), lambda i, j: (0, 0)))
        args.append(w_tail)
        out_shape.append(jax.ShapeDtypeStruct((T, nt), F32))
        out_specs.append(pl.BlockSpec((tm, nt), lambda i, j: (i, 0)))
    outs = pl.pallas_call(
        functools.partial(_norm_mm_kernel, tm=tm, has_tail=has_tail),
        grid=(T // tm, N // tn),
        in_specs=in_specs, out_specs=out_specs, out_shape=out_shape,
        scratch_shapes=[pltpu.VMEM((tm, D), BF16)],
        compiler_params=_cp("parallel", "arbitrary"),
        name="norm_mm",
    )(*args)
    return outs if has_tail else outs[0]


def _mm_res_kernel(*refs, n_a):
    res_ref = refs[0]
    a_refs = refs[1:1 + n_a]
    w_refs = refs[1 + n_a:1 + 2 * n_a]
    o_ref = refs[-1]
    acc = res_ref[...]
    for a, w in zip(a_refs, w_refs):
        acc = acc + _dot(a[...], w[...])
    o_ref[...] = acc


def mm_res(res, a_list, w_list, *, tm=1024, tn=512):
    T, N = res.shape
    tm = min(tm, T)
    assert T % tm == 0 and N % tn == 0
    n_a = len(a_list)
    in_specs = [pl.BlockSpec((tm, tn), lambda i, j: (i, j))]
    for a in a_list:
        in_specs.append(pl.BlockSpec((tm, a.shape[1]), lambda i, j: (i, 0)))
    for w in w_list:
        in_specs.append(pl.BlockSpec((w.shape[0], tn), lambda i, j: (0, j)))
    return pl.pallas_call(
        functools.partial(_mm_res_kernel, n_a=n_a),
        grid=(T // tm, N // tn),
        in_specs=in_specs,
        out_specs=pl.BlockSpec((tm, tn), lambda i, j: (i, j)),
        out_shape=jax.ShapeDtypeStruct((T, N), F32),
        compiler_params=_cp("parallel", "arbitrary"),
        name="mm_res",
    )(res, *a_list, *w_list)


def _t5_bucket(rel):
    nb = N_BUCKETS // 2
    max_exact = nb // 2
    ret = jnp.where(rel > 0, nb, 0)
    n = jnp.abs(rel)
    n_f = jnp.maximum(n, max_exact).astype(F32)
    large = max_exact + (jnp.log(n_f / max_exact) / math.log(MAX_DISTANCE / max_exact)
                         * (nb - max_exact)).astype(jnp.int32)
    large = jnp.minimum(large, nb - 1)
    return ret + jnp.where(n < max_exact, n, large)


def _bias_by_rel(rel_bias):
    rel = jnp.arange(-MAX_DISTANCE, MAX_DISTANCE + 1, dtype=jnp.int32)
    return rel_bias.astype(F32)[_t5_bucket(rel)]


def _win_attn_kernel(q_ref, kp_ref, kc_ref, kn_ref, vp_ref, vc_ref, vn_ref, bias_ref, sink_ref, o_ref):
    i = pl.program_id(1)
    nb = pl.num_programs(1)
    col = lax.broadcasted_iota(jnp.int32, (1, 3 * BLOCK), 1)
    invalid = ((col < BLOCK) & (i == 0)) | ((col >= 2 * BLOCK) & (i == nb - 1))
    scale = HEAD_DIM ** -0.5
    for hk in range(A_KV_HEADS):
        ks = slice(hk * HEAD_DIM, (hk + 1) * HEAD_DIM)
        qh = jnp.concatenate(
            [q_ref[:, (hk * A_GROUP + g) * HEAD_DIM:(hk * A_GROUP + g + 1) * HEAD_DIM]
             for g in range(A_GROUP)], axis=0)
        kb = jnp.concatenate([kp_ref[:, ks], kc_ref[:, ks], kn_ref[:, ks]], axis=0)
        vb = jnp.concatenate([vp_ref[:, ks], vc_ref[:, ks], vn_ref[:, ks]], axis=0)
        s = _dot_nt(qh, kb) * scale + bias_ref[hk]
        s = jnp.where(invalid, -jnp.inf, s)
        sink = sink_ref[hk]
        m = jnp.maximum(jnp.max(s, axis=-1, keepdims=True), sink)
        p = jnp.exp(s - m)
        den = jnp.sum(p, axis=-1, keepdims=True) + jnp.exp(sink - m)
        p = (p / den).astype(BF16)
        o = _dot(p, vb)
        for g in range(A_GROUP):
            h = hk * A_GROUP + g
            o_ref[:, h * HEAD_DIM:(h + 1) * HEAD_DIM] = o[g * BLOCK:(g + 1) * BLOCK].astype(o_ref.dtype)


def window_attention(z, sinks, tab):
    B, S, _ = z.shape
    nb = S // BLOCK
    q_loc = jnp.arange(BLOCK)[:, None]
    k_loc = jnp.arange(3 * BLOCK)[None, :] - BLOCK
    rel = k_loc - q_loc
    bias = jnp.where((jnp.abs(rel) <= WINDOW)[..., None],
                     tab[jnp.clip(rel, -MAX_DISTANCE, MAX_DISTANCE) + MAX_DISTANCE], -jnp.inf)
    bias = jnp.moveaxis(bias, -1, 0).reshape(A_KV_HEADS, A_GROUP * BLOCK, 3 * BLOCK)
    sink = jnp.broadcast_to(sinks.astype(F32).reshape(A_KV_HEADS, A_GROUP, 1, 1),
                            (A_KV_HEADS, A_GROUP, BLOCK, 1)).reshape(A_KV_HEADS, A_GROUP * BLOCK, 1)
    kcol = A_Q // A_KV
    vcol = (A_Q + A_KV) // A_KV
    prev = lambda b, i: (b, jnp.maximum(i - 1, 0))
    nxt = lambda b, i: (b, jnp.minimum(i + 1, nb - 1))
    return pl.pallas_call(
        _win_attn_kernel,
        grid=(B, nb),
        in_specs=[
            pl.BlockSpec((None, BLOCK, A_Q), lambda b, i: (b, i, 0)),
            pl.BlockSpec((None, BLOCK, A_KV), lambda b, i: (*prev(b, i), kcol)),
            pl.BlockSpec((None, BLOCK, A_KV), lambda b, i: (b, i, kcol)),
            pl.BlockSpec((None, BLOCK, A_KV), lambda b, i: (*nxt(b, i), kcol)),
            pl.BlockSpec((None, BLOCK, A_KV), lambda b, i: (*prev(b, i), vcol)),
            pl.BlockSpec((None, BLOCK, A_KV), lambda b, i: (b, i, vcol)),
            pl.BlockSpec((None, BLOCK, A_KV), lambda b, i: (*nxt(b, i), vcol)),
            pl.BlockSpec((A_KV_HEADS, A_GROUP * BLOCK, 3 * BLOCK), lambda b, i: (0, 0, 0)),
            pl.BlockSpec((A_KV_HEADS, A_GROUP * BLOCK, 1), lambda b, i: (0, 0, 0)),
        ],
        out_specs=pl.BlockSpec((None, BLOCK, A_Q), lambda b, i: (b, i, 0)),
        out_shape=jax.ShapeDtypeStruct((B, S, A_Q), BF16),
        compiler_params=_cp("parallel", "arbitrary"),
        name="window_attn",
    )(z, z, z, z, z, z, z, bias, sink)


CONV_TS = 512
CONV_TC = 512
CONV_HALO = 16


def _conv_kernel(prev_ref, cur_ref, next_ref, w_ref, o_ref, ext_ref):
    i = pl.program_id(1)
    j = pl.program_id(2)
    ns = pl.num_programs(1)
    ts = cur_ref.shape[0]
    pv = jnp.where(i == 0, 0.0, prev_ref[...].astype(F32))
    nx = jnp.where(i == ns - 1, 0.0, next_ref[...].astype(F32))
    ext_ref[0:CONV_HALO, :] = pv
    ext_ref[CONV_HALO:CONV_HALO + ts, :] = cur_ref[...].astype(F32)
    ext_ref[CONV_HALO + ts:, :] = nx
    half = CONV_WIDTH // 2
    acc = None
    for t in range(CONV_WIDTH):
        term = w_ref[t:t + 1, :] * ext_ref[pl.ds(CONV_HALO - half + t, ts), :]
        acc = term if acc is None else acc + term
    y = _silu(acc)
    heads_per_step = CONV_TC // B_DK
    q_steps = B_QK // CONV_TC

    def l2(scale):
        for hh in range(heads_per_step):
            seg = y[:, hh * B_DK:(hh + 1) * B_DK]
            r = lax.rsqrt(jnp.sum(seg * seg, axis=-1, keepdims=True) + EPS)
            o_ref[:, hh * B_DK:(hh + 1) * B_DK] = seg * (r * scale)

    @pl.when(j < q_steps)
    def _():
        l2(B_DK ** -0.5)

    @pl.when((j >= q_steps) & (j < 2 * q_steps))
    def _():
        l2(1.0)

    @pl.when(j >= 2 * q_steps)
    def _():
        o_ref[...] = y


def conv_qkv(z, conv_w):
    B, S, _ = z.shape
    ts = min(CONV_TS, S)
    c0 = (A_Q + 2 * A_KV) // CONV_TC
    hb = ts // CONV_HALO
    nh = S // CONV_HALO
    return pl.pallas_call(
        _conv_kernel,
        grid=(B, S // ts, B_CONV // CONV_TC),
        in_specs=[
            pl.BlockSpec((None, CONV_HALO, CONV_TC), lambda b, i, j: (b, jnp.maximum(i * hb - 1, 0), c0 + j)),
            pl.BlockSpec((None, ts, CONV_TC), lambda b, i, j: (b, i, c0 + j)),
            pl.BlockSpec((None, CONV_HALO, CONV_TC), lambda b, i, j: (b, jnp.minimum((i + 1) * hb, nh - 1), c0 + j)),
            pl.BlockSpec((CONV_WIDTH, CONV_TC), lambda b, i, j: (0, j)),
        ],
        out_specs=pl.BlockSpec((None, ts, CONV_TC), lambda b, i, j: (b, i, j)),
        out_shape=jax.ShapeDtypeStruct((B, S, B_CONV), F32),
        scratch_shapes=[pltpu.VMEM((ts + 2 * CONV_HALO, CONV_TC), F32)],
        compiler_params=_cp("parallel", "parallel", "arbitrary"),
        name="conv_qkv",
    )(z, z, z, conv_w.astype(F32))


def _bdot(a, b):
    return _dot(a.astype(DELTA_MXU), b.astype(DELTA_MXU), DELTA_PREC)


def _delta_chains(ch, eye, lmask_ref):
    n = ch[0]["q"].shape[0]
    for c in ch:
        c["decay"] = jnp.exp(jnp.where(c["causal"], c["gc_col"] - c["gc_row"], -jnp.inf))
        c["kb"] = c["k"] * c["beta"]
        c["k16"] = c["k"].astype(DELTA_MXU)
    for c in ch:
        c["m"] = jnp.where(c["strict"], _dot_nt(c["kb"].astype(DELTA_MXU), c["k16"], DELTA_PREC) * c["decay"], 0.0)
    for c in ch:
        c["inv"] = eye - c["m"] * lmask_ref[0]
    for l in range(1, lmask_ref.shape[0]):
        for c in ch:
            c["i16"] = c["inv"].astype(DELTA_MXU)
            c["p"] = _bdot(c["i16"], c["m"] * lmask_ref[l])
        for c in ch:
            c["inv"] = c["inv"] - _bdot(c["p"], c["i16"])
    for c in ch:
        c["eg"] = jnp.exp(c["gc_col"])
        c["sol"] = _bdot(c["inv"], jnp.concatenate([c["v"] * c["beta"], c["kb"] * c["eg"]], axis=1))
        c["a"] = _dot_nt(c["q"].astype(DELTA_MXU), c["k16"], DELTA_PREC) * c["decay"]
    for c in ch:
        w = c["sol"][:, B_DV:]
        c["ws"] = _bdot(jnp.concatenate([w, c["q"] * c["eg"]], axis=0), c["state"])
    out = []
    for c in ch:
        v_new = c["sol"][:, :B_DV] - c["ws"][:n]
        v16 = v_new.astype(DELTA_MXU)
        o = c["ws"][n:] + _bdot(c["a"], v16)
        k_tail = (c["k"] * jnp.exp(c["g_end"] - c["gc_col"])).astype(DELTA_MXU)
        state = c["state"] * jnp.exp(c["g_end"]) + lax.dot_general(
            k_tail, v16, (((0,), (0,)), ((), ())), precision=DELTA_PREC, preferred_element_type=F32)
        out.append((o, state))
    return out


def _delta_kernel(qf_ref, kf_ref, vf_ref, glf_ref, gtf_ref, qb_ref, kb_ref, vb_ref, glb_ref, gtb_ref,
                  arow_ref, drow_ref, acol_ref, dcol_ref, lmask_ref, of_ref, ob_ref, st_ref):
    c = pl.program_id(2)

    @pl.when(c == 0)
    def _():
        st_ref[...] = jnp.zeros_like(st_ref)

    CHUNK = DELTA_CH
    ri = lax.broadcasted_iota(jnp.int32, (CHUNK, CHUNK), 0)
    ci = lax.broadcasted_iota(jnp.int32, (CHUNK, CHUNK), 1)
    eye = (ri == ci).astype(F32)
    lower = (ri >= ci).astype(F32)
    upper = (ri <= ci).astype(F32)
    hb = DELTA_HB

    chains, sinks = [], []
    for d, (q_ref, k_ref, v_ref, gl_ref, gt_ref, o_ref) in enumerate(
            ((qf_ref, kf_ref, vf_ref, glf_ref, gtf_ref, of_ref),
             (qb_ref, kb_ref, vb_ref, glb_ref, gtb_ref, ob_ref))):
        fwd = d == 0
        gl = gl_ref[...]
        gt = gt_ref[...]
        beta_all = _sigmoid(gl)
        g_all = arow_ref[...] * _softplus(gl + drow_ref[...])
        gT_all = acol_ref[...] * _softplus(gt + dcol_ref[...])
        if fwd:
            gc_all = _dot(lower, g_all, HI)
            gcT_all = _dot(gT_all, upper, HI)
            causal, strict = ri >= ci, ri > ci
        else:
            gc_all = _dot(upper, g_all, HI)
            gcT_all = _dot(gT_all, lower, HI)
            causal, strict = ri <= ci, ri < ci
        end = CHUNK - 1 if fwd else 0
        for hh in range(hb):
            bl = d * hb + hh
            al = (2 + d) * hb + hh
            sl = slice(hh * B_DK, (hh + 1) * B_DK)
            gc_col = gc_all[:, al:al + 1]
            chains.append(dict(
                q=q_ref[:, sl], k=k_ref[:, sl], v=v_ref[:, sl], beta=beta_all[:, bl:bl + 1],
                gc_col=gc_col, gc_row=gcT_all[al:al + 1, :], g_end=gc_col[end:end + 1, :],
                state=st_ref[d * hb + hh], causal=causal, strict=strict))
            sinks.append((o_ref, sl, d * hb + hh))
    for (o, st), (o_ref, sl, si) in zip(_delta_chains(chains, eye, lmask_ref), sinks):
        o_ref[:, sl] = o
        st_ref[si] = st


def delta_rule(qkv, gl, a_log, dt_bias):
    B, S, _ = qkv.shape
    CHUNK = DELTA_CH
    n = S // CHUNK
    hb, hg = DELTA_HB, DELTA_HG
    ri = jnp.arange(CHUNK)[:, None]
    ci = jnp.arange(CHUNK)[None, :]
    lmask = jnp.stack([((ri >> (l + 1)) == (ci >> (l + 1))) & ((ri >> l) != (ci >> l))
                       for l in range(int(math.log2(CHUNK)))]).astype(F32)
    gw = 4 * hb
    g4 = gl[..., :N_GATE].reshape(B, S, 4, hg, hb)
    g4 = jnp.transpose(g4, (0, 3, 1, 2, 4)).reshape(B, hg, S, gw)
    g_rows = jnp.pad(g4, ((0, 0), (0, 0), (0, 0), (0, LANE - gw)))
    g_cols = jnp.transpose(g4.reshape(B, hg, n, CHUNK, gw), (0, 1, 2, 4, 3))
    neg_a = -jnp.exp(a_log.astype(F32)).reshape(2, hg, hb)
    dtb = dt_bias.astype(F32).reshape(2, hg, hb)
    zeros = jnp.zeros((2, hg, hb), F32)
    a4 = jnp.transpose(jnp.concatenate([zeros, neg_a], 0), (1, 0, 2)).reshape(hg, gw)
    d4 = jnp.transpose(jnp.concatenate([zeros, dtb], 0), (1, 0, 2)).reshape(hg, gw)
    arow = jnp.pad(a4, ((0, 0), (0, LANE - gw))).reshape(hg, 1, LANE)
    drow = jnp.pad(d4, ((0, 0), (0, LANE - gw))).reshape(hg, 1, LANE)
    acol = a4.reshape(hg, gw, 1)
    dcol = d4.reshape(hg, gw, 1)
    cw = hb * B_DK
    kq, kk, kv = 0, B_QK // cw, 2 * B_QK // cw

    def seq_specs(cidx):
        return [
            pl.BlockSpec((None, CHUNK, cw), lambda b, g, c: (b, cidx(c), kq + g)),
            pl.BlockSpec((None, CHUNK, cw), lambda b, g, c: (b, cidx(c), kk + g)),
            pl.BlockSpec((None, CHUNK, cw), lambda b, g, c: (b, cidx(c), kv + g)),
            pl.BlockSpec((None, None, CHUNK, LANE), lambda b, g, c: (b, g, cidx(c), 0)),
            pl.BlockSpec((None, None, None, gw, CHUNK), lambda b, g, c: (b, g, cidx(c), 0, 0)),
        ]

    fwd_idx = lambda c: c
    bwd_idx = lambda c: n - 1 - c
    par_specs = [
        pl.BlockSpec((None, 1, LANE), lambda b, g, c: (g, 0, 0)),
        pl.BlockSpec((None, 1, LANE), lambda b, g, c: (g, 0, 0)),
        pl.BlockSpec((None, gw, 1), lambda b, g, c: (g, 0, 0)),
        pl.BlockSpec((None, gw, 1), lambda b, g, c: (g, 0, 0)),
        pl.BlockSpec(lmask.shape, lambda b, g, c: (0, 0, 0)),
    ]
    o_f, o_b = pl.pallas_call(
        _delta_kernel,
        grid=(B, hg, n),
        in_specs=seq_specs(fwd_idx) + seq_specs(bwd_idx) + par_specs,
        out_specs=[
            pl.BlockSpec((None, CHUNK, cw), lambda b, g, c: (b, fwd_idx(c), g)),
            pl.BlockSpec((None, CHUNK, cw), lambda b, g, c: (b, bwd_idx(c), g)),
        ],
        out_shape=[jax.ShapeDtypeStruct((B, S, B_V), F32)] * 2,
        scratch_shapes=[pltpu.VMEM((2 * hb, B_DK, B_DV), F32)],
        compiler_params=_cp("parallel", "parallel", "arbitrary"),
        name="delta_rule",
    )(qkv, qkv, qkv, g_rows, g_cols, qkv, qkv, qkv, g_rows, g_cols, arow, drow, acol, dcol, lmask)
    return o_f, o_b


GATE_TC = 1024


def _delta_out_kernel(of_ref, ob_ref, zb_ref, g_ref, o_ref):
    for hh in range(GATE_TC // B_DV):
        sl = slice(hh * B_DV, (hh + 1) * B_DV)
        ob = of_ref[:, sl] + ob_ref[:, sl]
        y = _rms_rows(ob, g_ref[...])
        o_ref[:, sl] = (y * _silu(zb_ref[:, sl].astype(F32))).astype(o_ref.dtype)


def delta_out(o_f, o_b, z2d, onorm, *, tm=512):
    T = o_f.shape[0]
    tm = min(tm, T)
    c0 = (A_Q + 2 * A_KV + B_CONV) // GATE_TC
    return pl.pallas_call(
        _delta_out_kernel,
        grid=(T // tm, B_V // GATE_TC),
        in_specs=[
            pl.BlockSpec((tm, GATE_TC), lambda i, j: (i, j)),
            pl.BlockSpec((tm, GATE_TC), lambda i, j: (i, j)),
            pl.BlockSpec((tm, GATE_TC), lambda i, j: (i, c0 + j)),
            pl.BlockSpec((1, B_DV), lambda i, j: (0, 0)),
        ],
        out_specs=pl.BlockSpec((tm, GATE_TC), lambda i, j: (i, j)),
        out_shape=jax.ShapeDtypeStruct((T, B_V), BF16),
        compiler_params=_cp("parallel", "arbitrary"),
        name="delta_out",
    )(o_f, o_b, z2d, onorm.reshape(1, B_DV).astype(F32))


DIFF_TQ = 1024
DIFF_TK = 512


def _diff_attn_kernel(lam_ref, far_ref, q_ref, k_ref, v_ref, b_ref, g_ref, o_ref, m_ref, l_ref, acc_ref, *, out_scale, tk):
    h = pl.program_id(1)
    qi = pl.program_id(2)
    nk = k_ref.shape[0] // tk
    r = q_ref.shape[0] // tk
    base = qi * r

    m_ref[...] = jnp.full_like(m_ref, -jnp.inf)
    l_ref[...] = jnp.zeros_like(l_ref)
    acc_ref[...] = jnp.zeros_like(acc_ref)

    def step(kj, near_dj, far_const):
        rows = pl.ds(pl.multiple_of(kj * tk, tk), tk)
        v = v_ref[rows, :]
        units = [(c, t) for c in range(r) for t in range(2)]

        def scores(c, t):
            sl = slice(t * C_QK_DIM, (t + 1) * C_QK_DIM)
            return _dot_nt(q_ref[c * tk:(c + 1) * tk, sl], k_ref[rows, sl])

        s_next = scores(*units[0])
        for i, (c, t) in enumerate(units):
            s = s_next
            if i + 1 < len(units):
                s_next = scores(*units[i + 1])
            qr = slice(c * tk, (c + 1) * tk)
            off = None if near_dj is None else near_dj - c
            if off is None:
                const = far_const
            elif abs(off) <= 1:
                const = None
                s = s + b_ref[off + 1]
            else:
                const = far_ref[0 if off < 0 else 1, h]
            m_loc = jnp.max(s, axis=-1, keepdims=True)
            if const is not None:
                m_loc = m_loc + const
            m_old = m_ref[t, qr, :]
            m_new = jnp.maximum(m_old, m_loc)
            alpha = jnp.exp2(m_old - m_new)
            shift = m_new if const is None else m_new - const
            p = jnp.exp2(s - jnp.concatenate([shift] * (tk // LANE), axis=1))
            psum = p[:, :LANE]
            for cc in range(1, tk // LANE):
                psum = psum + p[:, cc * LANE:(cc + 1) * LANE]
            l_ref[t, qr, :] = alpha * l_ref[t, qr, :] + psum
            acc_ref[t, qr, :] = (jnp.concatenate([alpha] * (C_V_DIM // LANE), axis=1) * acc_ref[t, qr, :]
                                 + _dot(p.astype(BF16), v))
            m_ref[t, qr, :] = m_new

    def far_left(kj, carry):
        step(kj, None, far_ref[0, h])
        return carry

    def far_right(kj, carry):
        step(kj, None, far_ref[1, h])
        return carry

    lax.fori_loop(0, jnp.maximum(base - 1, 0), far_left, 0)
    for dj in range(-1, r + 1):
        @pl.when((base + dj >= 0) & (base + dj < nk))
        def _():
            step(base + dj, dj, None)
    lax.fori_loop(jnp.minimum(base + r + 1, nk), nk, far_right, 0)

    lam = lam_ref[0, 0]
    r0 = 1.0 / jnp.sum(l_ref[0], axis=-1, keepdims=True)
    r1 = 1.0 / jnp.sum(l_ref[1], axis=-1, keepdims=True)
    o = acc_ref[0] * r0 - acc_ref[1] * (lam * r1)
    o_ref[...] = (_rms_rows(o, g_ref[...]) * out_scale).astype(o_ref.dtype)


def diff_attention(z, lam, tab, subln, lam_init):
    B, S, _ = z.shape
    tq, tk = min(DIFF_TQ, S), min(DIFF_TK, S)
    assert tk > MAX_DISTANCE and tq % tk == 0
    nq = S // tq
    r = tq // tk
    tab2 = tab.T * LOG2E
    nd = 3
    period = 2 * tk + 1
    u = jnp.arange(period)[None, :]
    d = jnp.arange(-1, 2)[:, None]
    idx = jnp.clip(d * tk + u - tk, -MAX_DISTANCE, MAX_DISTANCE) + MAX_DISTANCE
    sig = tab2[:, idx]
    skew = jnp.tile(sig, (1, 1, tk))[..., :tk * (period - 1)].reshape(C_HEADS, nd, tk, period - 1)
    btile = skew[..., tk:2 * tk]
    far = jnp.stack([tab2[:, 0], tab2[:, -1]])
    hw = 2 * C_QK_DIM
    return pl.pallas_call(
        functools.partial(_diff_attn_kernel, out_scale=1.0 - lam_init, tk=tk),
        grid=(B, C_HEADS, nq),
        in_specs=[
            pl.BlockSpec(memory_space=pltpu.SMEM),
            pl.BlockSpec(memory_space=pltpu.SMEM),
            pl.BlockSpec((None, tq, hw), lambda b, h, i: (b, i, h)),
            pl.BlockSpec((None, S, hw), lambda b, h, i: (b, 0, C_HEADS + h)),
            pl.BlockSpec((None, S, hw), lambda b, h, i: (b, 0, 2 * C_HEADS + h)),
            pl.BlockSpec((None, nd, tk, tk), lambda b, h, i: (h, 0, 0, 0)),
            pl.BlockSpec((1, C_V_DIM), lambda b, h, i: (0, 0)),
        ],
        out_specs=pl.BlockSpec((None, tq, C_V_DIM), lambda b, h, i: (b, i, h)),
        out_shape=jax.ShapeDtypeStruct((B, S, C_HEADS * C_V_DIM), BF16),
        scratch_shapes=[pltpu.VMEM((2, tq, LANE), F32), pltpu.VMEM((2, tq, LANE), F32),
                        pltpu.VMEM((2, tq, C_V_DIM), F32)],
        compiler_params=_cp("parallel", "parallel", "arbitrary"),
        name="diff_attn",
    )(lam.reshape(1, 1).astype(F32), far, z, z, z, btile, subln.reshape(1, C_V_DIM).astype(F32))


def _mem_attn_kernel(x_ref, g_ref, wq_ref, kv_ref, wo_ref, o_ref):
    x = x_ref[...]
    h = _rms_rows(x, g_ref[...]).astype(BF16)
    q = _dot(h, wq_ref[...]).astype(BF16)
    outs = []
    for hd in range(X_HEADS):
        sl = slice(hd * X_HEAD_DIM, (hd + 1) * X_HEAD_DIM)
        s = _dot_nt(q[:, sl], kv_ref[:, sl]) * (X_HEAD_DIM ** -0.5)
        m = jnp.max(s, axis=-1, keepdims=True)
        e = jnp.exp(s - m)
        p = (e / jnp.sum(e, axis=-1, keepdims=True)).astype(BF16)
        outs.append(_dot(p, kv_ref[:, X_WIDTH + hd * X_HEAD_DIM:X_WIDTH + (hd + 1) * X_HEAD_DIM]))
    o = jnp.concatenate(outs, axis=1).astype(BF16)
    o_ref[...] = x + _dot(o, wo_ref[...])


def memory_attention(x, kv, g, wq, wo, *, tm=256):
    B, S, D = x.shape
    tm = min(tm, S)
    return pl.pallas_call(
        _mem_attn_kernel,
        grid=(B, S // tm),
        in_specs=[
            pl.BlockSpec((None, tm, D), lambda b, i: (b, i, 0)),
            pl.BlockSpec((1, D), lambda b, i: (0, 0)),
            pl.BlockSpec((D, X_WIDTH), lambda b, i: (0, 0)),
            pl.BlockSpec((None, MEM_TOKENS, 2 * X_WIDTH), lambda b, i: (b, 0, 0)),
            pl.BlockSpec((X_WIDTH, D), lambda b, i: (0, 0)),
        ],
        out_specs=pl.BlockSpec((None, tm, D), lambda b, i: (b, i, 0)),
        out_shape=jax.ShapeDtypeStruct((B, S, D), F32),
        compiler_params=_cp("parallel", "arbitrary"),
        name="mem_attn",
    )(x, g.reshape(1, D).astype(F32), wq, kv, wo)


def _first_max_onehot(vals, rows):
    m = jnp.max(vals, axis=0, keepdims=True)
    idx = jnp.min(jnp.where(vals == m, rows, vals.shape[0]), axis=0, keepdims=True)
    return rows == idx


def _router_kernel(x_ref, g_ref, w_ref, b_ref, o_ref, h_ref):
    tm = x_ref.shape[0]

    def body(r, c):
        sl = pl.ds(pl.multiple_of(r * LANE, LANE), LANE)
        h_ref[sl, :] = _rms_rows(x_ref[sl, :], g_ref[...])
        return c
    lax.fori_loop(0, tm // LANE, body, 0)
    logits = _dot(h_ref[...], w_ref[...], HI)
    lt = jnp.transpose(logits)[:N_EXPERTS, :]
    scores = _sigmoid(lt)
    biased = scores + b_ref[...]
    per_group = N_EXPERTS // N_GROUPS
    rows8 = lax.broadcasted_iota(jnp.int32, (per_group, tm), 0)
    gscore = []
    for gi in range(N_GROUPS):
        blk = biased[gi * per_group:(gi + 1) * per_group]
        first = _first_max_onehot(blk, rows8)
        m1 = jnp.max(blk, axis=0, keepdims=True)
        m2 = jnp.max(jnp.where(first, -jnp.inf, blk), axis=0, keepdims=True)
        gscore.append(m1 + m2)
    gscore = jnp.concatenate(gscore, axis=0)
    growi = lax.broadcasted_iota(jnp.int32, (N_GROUPS, tm), 0)
    gsel = jnp.zeros((N_GROUPS, tm), jnp.bool_)
    work = gscore
    for _ in range(TOPK_GROUPS):
        oh = _first_max_onehot(work, growi)
        gsel = gsel | oh
        work = jnp.where(oh, -jnp.inf, work)
    gself = gsel.astype(F32)
    masked = jnp.concatenate(
        [jnp.where(gself[gi:gi + 1] > 0.0, biased[gi * per_group:(gi + 1) * per_group], -jnp.inf)
         for gi in range(N_GROUPS)], axis=0)
    erow = lax.broadcasted_iota(jnp.int32, (N_EXPERTS, tm), 0)
    esel = jnp.zeros((N_EXPERTS, tm), jnp.bool_)
    work = masked
    for _ in range(TOP_K):
        oh = _first_max_onehot(work, erow)
        esel = esel | oh
        work = jnp.where(oh, -jnp.inf, work)
    w = jnp.where(esel, scores, 0.0)
    gates = w / jnp.sum(w, axis=0, keepdims=True) * ROUTED_SCALE
    n_shared = SHARED_FF // EXPERT_FF
    full = jnp.concatenate([gates, jnp.ones((8, tm), F32), jnp.zeros((LANE - N_EXPERTS - 8, tm), F32)], axis=0)
    frow = lax.broadcasted_iota(jnp.int32, (LANE, tm), 0)
    full = jnp.where(frow < N_EXPERTS + n_shared, full, 0.0)
    o_ref[...] = jnp.transpose(full)


def route_tokens(x, g, w_router, r_bias, *, tm=512):
    T, D = x.shape
    tm = min(tm, T)
    w = jnp.pad(w_router.astype(F32), ((0, 0), (0, LANE - N_EXPERTS)))
    return pl.pallas_call(
        _router_kernel,
        grid=(T // tm,),
        in_specs=[
            pl.BlockSpec((tm, D), lambda i: (i, 0)),
            pl.BlockSpec((1, D), lambda i: (0, 0)),
            pl.BlockSpec((D, LANE), lambda i: (0, 0)),
            pl.BlockSpec((N_EXPERTS, 1), lambda i: (0, 0)),
        ],
        out_specs=pl.BlockSpec((tm, LANE), lambda i: (i, 0)),
        out_shape=jax.ShapeDtypeStruct((T, LANE), F32),
        scratch_shapes=[pltpu.VMEM((tm, D), F32)],
        compiler_params=_cp("parallel"),
        name="moe_router",
    )(x, g.reshape(1, D).astype(F32), w, r_bias.astype(F32).reshape(N_EXPERTS, 1))


def _pattern_rank_table():
    def revolving(n, k):
        if k == 0:
            return [[]]
        if k == n:
            return [list(range(n))]
        return revolving(n - 1, k) + [c + [n - 1] for c in reversed(revolving(n - 1, k - 1))]

    combos = [sum(1 << g for g in c) for c in revolving(N_GROUPS, TOPK_GROUPS)]
    table = []
    for m in range(1 << N_GROUPS):
        ranks = [i for i, c in enumerate(combos) if (c & m) == m]
        table.append(min(ranks) if ranks else len(combos))
    return table


_PATTERN_RANK = _pattern_rank_table()


def _moe_plan(gates, tm):
    T = gates.shape[0]
    nt = T // tm
    per_group = N_EXPERTS // N_GROUPS
    sel = gates[:, :N_EXPERTS] > 0.0
    gsel = jnp.any(sel.reshape(T, N_GROUPS, per_group), axis=-1)
    gmask = jnp.sum(gsel.astype(jnp.int32) << jnp.arange(N_GROUPS, dtype=jnp.int32), axis=-1)
    order = jnp.argsort(jnp.asarray(_PATTERN_RANK, jnp.int32)[gmask]).astype(jnp.int32)
    gates_s = gates[order]
    nr = N_EXPERTS // MOE_EPS
    used = jnp.any((gates_s[:, :N_EXPERTS] > 0.0).reshape(nt, tm, nr, MOE_EPS), axis=(1, 3))
    used = jnp.concatenate([used, jnp.ones((nt, N_PAIRS - nr), jnp.bool_)], axis=1)
    fetch = lax.cummin(jnp.where(used, jnp.arange(N_PAIRS, dtype=jnp.int32), N_PAIRS - 1), axis=1, reverse=True)
    return order, gates_s, used.astype(jnp.int32).reshape(-1), fetch.astype(jnp.int32).reshape(-1)


def _moe_kernel(order_ref, used_ref, fetch_ref, *refs, tm, final_norm, bounds):
    ng = len(bounds) - 1
    x_hbm = refs[:ng]
    g_ref, gate_ref, wgu_ref, wd_ref, gf_ref = refs[ng:ng + 5]
    o_hbm = refs[ng + 5:2 * ng + 5]
    acc_ref, xin_ref, h_ref, sem_ref = refs[2 * ng + 5:]
    i = pl.program_id(0)
    p = pl.program_id(1)
    nt = pl.num_programs(0)
    npair = pl.num_programs(1)

    def per_group(tile, r, make):
        idx = order_ref[tile * tm + r]
        if ng == 1:
            make(0, idx)
            return
        for k in range(ng):
            @pl.when((idx >= bounds[k]) & (idx < bounds[k + 1]))
            def _():
                make(k, idx - bounds[k])

    def row_in(k, row, r):
        return pltpu.make_async_copy(x_hbm[k].at[pl.ds(row, 1)], xin_ref.at[pl.ds(r, 1)], sem_ref.at[0])

    def row_out(k, row, r):
        return pltpu.make_async_copy(acc_ref.at[pl.ds(r, 1)], o_hbm[k].at[pl.ds(row, 1)], sem_ref.at[1])

    def for_rows(fn):
        def body(r, c):
            fn(r)
            return c
        lax.fori_loop(0, tm, body, 0, unroll=8)

    def gather_start(tile):
        for_rows(lambda r: per_group(tile, r, lambda k, row: row_in(k, row, r).start()))

    def gather_wait(tile):
        if ng == 1:
            pltpu.make_async_copy(x_hbm[0].at[pl.ds(0, tm)], xin_ref, sem_ref.at[0]).wait()
            return
        for_rows(lambda r: per_group(tile, r, lambda k, row: row_in(k, row, r).wait()))

    def scatter_start(tile):
        for_rows(lambda r: per_group(tile, r, lambda k, row: row_out(k, row, r).start()))

    def scatter_wait(tile):
        if ng == 1:
            pltpu.make_async_copy(acc_ref, o_hbm[0].at[pl.ds(0, tm)], sem_ref.at[1]).wait()
            return
        for_rows(lambda r: per_group(tile, r, lambda k, row: row_out(k, row, r).wait()))

    @pl.when(p == 0)
    def _():
        @pl.when(i == 0)
        def _():
            gather_start(i)
        gather_wait(i)

        def body(r, c):
            sl = pl.ds(pl.multiple_of(r * LANE, LANE), LANE)
            h_ref[sl, :] = _rms_rows(xin_ref[sl, :], g_ref[...]).astype(BF16)
            return c
        lax.fori_loop(0, tm // LANE, body, 0)

        @pl.when(i > 0)
        def _():
            scatter_wait(i - 1)
        acc_ref[...] = xin_ref[...]

        @pl.when(i + 1 < nt)
        def _():
            gather_start(i + 1)

    @pl.when(used_ref[i * npair + p] != 0)
    def _():
        gu = _dot(h_ref[...], wgu_ref[...])
        gates = gate_ref[...]
        lane = lax.broadcasted_iota(jnp.int32, gates.shape, 1)
        F = EXPERT_FF
        hid = []
        for e in range(MOE_EPS):
            ge = jnp.sum(jnp.where(lane == MOE_EPS * p + e, gates, 0.0), axis=1, keepdims=True)
            hid.append(_silu(gu[:, e * F:(e + 1) * F]) * gu[:, (MOE_EPS + e) * F:(MOE_EPS + 1 + e) * F] * ge)
        hid = jnp.concatenate(hid, axis=1).astype(BF16)
        acc_ref[...] += _dot(hid, wd_ref[...])

    @pl.when(p == npair - 1)
    def _():
        if final_norm:
            def body(r, c):
                sl = pl.ds(pl.multiple_of(r * LANE, LANE), LANE)
                acc_ref[sl, :] = _rms_rows(acc_ref[sl, :], gf_ref[...])
                return c
            lax.fori_loop(0, tm // LANE, body, 0)
        scatter_start(i)

        @pl.when(i == nt - 1)
        def _():
            scatter_wait(i)


def moe_experts(xs, g, gates, w_gu, w_d, g_final, *, layer, final_norm, tm=512):
    D = xs[0].shape[1]
    bounds = [0]
    for x in xs:
        bounds.append(bounds[-1] + x.shape[0])
    T = bounds[-1]
    assert T % tm == 0
    order, gates_s, used, fetch = _moe_plan(jnp.concatenate(gates, axis=0), tm)
    wblock = lambda i, p, order, used, fetch: (layer * N_PAIRS + fetch[i * N_PAIRS + p], 0, 0)
    any_spec = pl.BlockSpec(memory_space=pl.ANY)
    return pl.pallas_call(
        functools.partial(_moe_kernel, tm=tm, final_norm=final_norm, bounds=tuple(bounds)),
        grid_spec=pltpu.PrefetchScalarGridSpec(
            num_scalar_prefetch=3,
            grid=(T // tm, N_PAIRS),
            in_specs=[any_spec] * len(xs) + [
                pl.BlockSpec((1, D), lambda i, p, *_: (0, 0)),
                pl.BlockSpec((tm, LANE), lambda i, p, *_: (i, 0)),
                pl.BlockSpec((None, D, 2 * MOE_EPS * EXPERT_FF), wblock),
                pl.BlockSpec((None, MOE_EPS * EXPERT_FF, D), wblock),
                pl.BlockSpec((1, D), lambda i, p, *_: (0, 0)),
            ],
            out_specs=[any_spec] * len(xs),
            scratch_shapes=[pltpu.VMEM((tm, D), F32), pltpu.VMEM((tm, D), F32), pltpu.VMEM((tm, D), BF16),
                            pltpu.SemaphoreType.DMA((2,))],
        ),
        out_shape=[jax.ShapeDtypeStruct(x.shape, F32) for x in xs],
        compiler_params=_cp("arbitrary", "arbitrary"),
        name="moe_experts",
    )(order, used, fetch, *xs, g.reshape(1, D).astype(F32), gates_s, w_gu, w_d, g_final.reshape(1, D).astype(F32))


def _moe_weights(w_gate, w_up, w_down, s_gate, s_up, s_down):
    L, _, D, F = w_gate.shape
    ns = SHARED_FF // F

    def grouped(w, s):
        sh = jnp.transpose(s.reshape(L, D, ns, F), (0, 2, 1, 3))
        return jnp.concatenate([w, sh], axis=1).reshape(L, N_PAIRS, MOE_EPS, D, F)

    gu = jnp.concatenate([grouped(w_gate, s_gate), grouped(w_up, s_up)], axis=2)
    w_gu = jnp.transpose(gu, (0, 1, 3, 2, 4)).reshape(L * N_PAIRS, D, 2 * MOE_EPS * F).astype(BF16)
    w_d = jnp.concatenate([w_down, s_down.reshape(L, ns, F, D)], axis=1).astype(BF16)
    return w_gu, w_d.reshape(L * N_PAIRS, MOE_EPS * F, D)


def kernel(x_prompt, x_sample, mem_prompt, mem_sample, rel_bias, norm_mix, norm_mem, norm_memkv, norm_ffn, norm_final, ev_w_in, ev_sinks, ev_conv, ev_a_log, ev_dt_bias, ev_onorm, ev_w_out, od_w_in, od_lambda, od_subln, od_w_out, mx_wq, mx_wkv, mx_wo, moe_router, moe_bias, moe_w_gate, moe_w_up, moe_w_down, sh_w_gate, sh_w_up, sh_w_down):
    tab = _bias_by_rel(rel_bias)

    w_gu, w_d = _moe_weights(moe_w_gate, moe_w_up, moe_w_down, sh_w_gate, sh_w_up, sh_w_down)
    layers = []
    for l in range(DEPTH):
        i = l // 2
        lw = {}
        if l % 2 == 0:
            w_in = ev_w_in[i]
            lw["w_in"] = w_in[:, :EVEN_MAIN].astype(BF16)
            lw["w_gate_tail"] = jnp.pad(w_in[:, EVEN_MAIN:], ((0, 0), (0, LANE - N_GATE))).astype(BF16)
            lw["w_out_a"] = ev_w_out[i][:A_Q].astype(BF16)
            lw["w_out_b"] = ev_w_out[i][A_Q:].astype(BF16)
        else:
            w_in = od_w_in[i]
            lw["w_in"] = jnp.concatenate([w_in[:, :C_QK] * (C_QK_DIM ** -0.5 * LOG2E), w_in[:, C_QK:]],
                                         axis=1).astype(BF16)
            lw["w_out"] = od_w_out[i].astype(BF16)
            lp = od_lambda[i].astype(F32)
            lam_init = 0.8 - 0.6 * math.exp(-0.3 * l)
            lw["lam_init"] = lam_init
            lw["lam"] = jnp.exp(jnp.sum(lp[0] * lp[1])) - jnp.exp(jnp.sum(lp[2] * lp[3])) + lam_init
        lw["wq"] = mx_wq[l].astype(BF16)
        lw["wkv"] = mx_wkv[l].astype(BF16)
        lw["wo"] = mx_wo[l].astype(BF16)
        layers.append(lw)

    def mix_and_route(x, mem2, B, S, l):
        T, D = x.shape
        lw = layers[l]
        i = l // 2
        if l % 2 == 0:
            z, gl = norm_mm(x, norm_mix[l], lw["w_in"], lw["w_gate_tail"])
            z3 = z.reshape(B, S, EVEN_MAIN)
            out_a = window_attention(z3, ev_sinks[i], tab)
            qkv = conv_qkv(z3, ev_conv[i])
            o_f, o_b = delta_rule(qkv, gl.reshape(B, S, LANE), ev_a_log[i], ev_dt_bias[i])
            out_b = delta_out(o_f.reshape(T, B_V), o_b.reshape(T, B_V), z, ev_onorm[i])
            x = mm_res(x, [out_a.reshape(T, A_Q), out_b], [lw["w_out_a"], lw["w_out_b"]])
        else:
            z = norm_mm(x, norm_mix[l], lw["w_in"])
            o = diff_attention(z.reshape(B, S, 3 * C_QK), lw["lam"], tab, od_subln[i], lw["lam_init"])
            x = mm_res(x, [o.reshape(T, C_HEADS * C_V_DIM)], [lw["w_out"]])
        kv = norm_mm(mem2, norm_memkv[l], lw["wkv"])
        x = memory_attention(x.reshape(B, S, D), kv.reshape(B, MEM_TOKENS, 2 * X_WIDTH),
                             norm_mem[l], lw["wq"], lw["wo"]).reshape(T, D)
        return x, route_tokens(x, norm_ffn[l], moe_router[l], moe_bias[l])

    groups = [(x_prompt, mem_prompt), (x_sample, mem_sample)]
    shapes = [x.shape for x, _ in groups]
    xs = [x.reshape(-1, x.shape[-1]) for x, _ in groups]
    mems = [m.reshape(-1, m.shape[-1]) for _, m in groups]
    for l in range(DEPTH):
        routed = [mix_and_route(x, m, sh[0], sh[1], l) for x, m, sh in zip(xs, mems, shapes)]
        xs = [moe_experts([x], norm_ffn[l], [gt], w_gu, w_d, norm_final,
                          layer=l, final_norm=(l == DEPTH - 1))[0] for x, gt in routed]
    return tuple(x.reshape(sh) for x, sh in zip(xs, shapes))
```

```python
import functools
import math

import jax
import jax.numpy as jnp
from jax import lax
from jax.experimental import pallas as pl
from jax.experimental.pallas import tpu as pltpu

F32 = jnp.float32
BF16 = jnp.bfloat16

D_MODEL = 4096
DEPTH = 2
HEAD_DIM = 128
BLOCK = 128
WINDOW = 128
A_HEADS = 16
A_KV_HEADS = 4
A_GROUP = A_HEADS // A_KV_HEADS
B_HEADS = 16
B_DK = 128
B_DV = 128
CONV_WIDTH = 5
CHUNK = 64
C_HEADS = 16
C_QK_DIM = 128
C_V_DIM = 2 * C_QK_DIM
N_BUCKETS = 32
MAX_DISTANCE = 128
MEM_TOKENS = 256
X_HEADS = 4
X_HEAD_DIM = 128
N_EXPERTS = 64
TOP_K = 8
N_GROUPS = 8
TOPK_GROUPS = 4
EXPERT_FF = 128
SHARED_FF = 512
ROUTED_SCALE = 2.5
EPS = 1e-6

A_Q = A_HEADS * HEAD_DIM
A_KV = A_KV_HEADS * HEAD_DIM
B_QK = B_HEADS * B_DK
B_V = B_HEADS * B_DV
B_CONV = 2 * B_QK + B_V
EVEN_MAIN = A_Q + 2 * A_KV + B_CONV + B_V
N_GATE = 4 * B_HEADS
C_QK = 2 * C_HEADS * C_QK_DIM
X_WIDTH = X_HEADS * X_HEAD_DIM

LANE = 128
VMEM_LIMIT = 56 * 1024 * 1024
HI = lax.Precision.HIGHEST
LOG2E = math.log2(math.e)

DELTA_HB = 4
DELTA_HG = B_HEADS // DELTA_HB
DELTA_CH = 256
DELTA_MXU = BF16
DELTA_PREC = None
MOE_EPS = 4
N_PAIRS = (N_EXPERTS + SHARED_FF // EXPERT_FF) // MOE_EPS


def _cp(*sem):
    return pltpu.CompilerParams(dimension_semantics=sem, vmem_limit_bytes=VMEM_LIMIT)


def _dot(a, b, prec=None):
    return lax.dot_general(a, b, (((1,), (0,)), ((), ())), precision=prec,
                           preferred_element_type=F32)


def _dot_nt(a, b, prec=None):
    return lax.dot_general(a, b, (((1,), (1,)), ((), ())), precision=prec,
                           preferred_element_type=F32)


def _sigmoid(x):
    return 1.0 / (1.0 + jnp.exp(-x))


def _silu(x):
    return x * _sigmoid(x)


def _softplus(x):
    return jnp.maximum(x, 0.0) + jnp.log(1.0 + jnp.exp(-jnp.abs(x)))


def _rms_rows(x, g):
    ms = jnp.mean(x * x, axis=-1, keepdims=True)
    return x * lax.rsqrt(ms + EPS) * g


def _norm_mm_kernel(x_ref, g_ref, w_ref, *rest, tm, has_tail):
    if has_tail:
        wt_ref, o_ref, ot_ref, h_ref = rest
    else:
        o_ref, h_ref = rest

    @pl.when(pl.program_id(1) == 0)
    def _():
        def body(r, c):
            sl = pl.ds(pl.multiple_of(r * LANE, LANE), LANE)
            h_ref[sl, :] = _rms_rows(x_ref[sl, :], g_ref[...]).astype(BF16)
            return c
        lax.fori_loop(0, tm // LANE, body, 0)
        if has_tail:
            ot_ref[...] = _dot(h_ref[...], wt_ref[...])

    o_ref[...] = _dot(h_ref[...], w_ref[...]).astype(o_ref.dtype)


def norm_mm(x, g, w, w_tail=None, *, tm=1024, tn=512, out_dtype=BF16):
    T, D = x.shape
    N = w.shape[1]
    tm = min(tm, T)
    assert T % tm == 0 and N % tn == 0 and tm % LANE == 0
    has_tail = w_tail is not None
    in_specs = [
        pl.BlockSpec((tm, D), lambda i, j: (i, 0), pipeline_mode=pl.Buffered(1)),
        pl.BlockSpec((1, D), lambda i, j: (0, 0)),
        pl.BlockSpec((D, tn), lambda i, j: (0, j)),
    ]
    args = [x, g.reshape(1, D).astype(F32), w]
    out_shape = [jax.ShapeDtypeStruct((T, N), out_dtype)]
    out_specs = [pl.BlockSpec((tm, tn), lambda i, j: (i, j))]
    if has_tail:
        nt = w_tail.shape[1]
        in_specs.append(pl.BlockSpec((D, nt), lambda i, j: (0, 0)))
        args.append(w_tail)
        out_shape.append(jax.ShapeDtypeStruct((T, nt), F32))
        out_specs.append(pl.BlockSpec((tm, nt), lambda i, j: (i, 0)))
    outs = pl.pallas_call(
        functools.partial(_norm_mm_kernel, tm=tm, has_tail=has_tail),
        grid=(T // tm, N // tn),
        in_specs=in_specs, out_specs=out_specs, out_shape=out_shape,
        scratch_shapes=[pltpu.VMEM((tm, D), BF16)],
        compiler_params=_cp("parallel", "arbitrary"),
        name="norm_mm",
    )(*args)
    return outs if has_tail else outs[0]


def _mm_res_kernel(*refs, n_a):
    res_ref = refs[0]
    a_refs = refs[1:1 + n_a]
    w_refs = refs[1 + n_a:1 + 2 * n_a]
    o_ref = refs[-1]
    acc = res_ref[...]
    for a, w in zip(a_refs, w_refs):
        acc = acc + _dot(a[...], w[...])
    o_ref[...] = acc


def mm_res(res, a_list, w_list, *, tm=1024, tn=512):
    T, N = res.shape
    tm = min(tm, T)
    assert T % tm == 0 and N % tn == 0
    n_a = len(a_list)
    in_specs = [pl.BlockSpec((tm, tn), lambda i, j: (i, j))]
    for a in a_list:
        in_specs.append(pl.BlockSpec((tm, a.shape[1]), lambda i, j: (i, 0)))
    for w in w_list:
        in_specs.append(pl.BlockSpec((w.shape[0], tn), lambda i, j: (0, j)))
    return pl.pallas_call(
        functools.partial(_mm_res_kernel, n_a=n_a),
        grid=(T // tm, N // tn),
        in_specs=in_specs,
        out_specs=pl.BlockSpec((tm, tn), lambda i, j: (i, j)),
        out_shape=jax.ShapeDtypeStruct((T, N), F32),
        compiler_params=_cp("parallel", "arbitrary"),
        name="mm_res",
    )(res, *a_list, *w_list)


def _t5_bucket(rel):
    nb = N_BUCKETS // 2
    max_exact = nb // 2
    ret = jnp.where(rel > 0, nb, 0)
    n = jnp.abs(rel)
    n_f = jnp.maximum(n, max_exact).astype(F32)
    large = max_exact + (jnp.log(n_f / max_exact) / math.log(MAX_DISTANCE / max_exact)
                         * (nb - max_exact)).astype(jnp.int32)
    large = jnp.minimum(large, nb - 1)
    return ret + jnp.where(n < max_exact, n, large)


def _bias_by_rel(rel_bias):
    rel = jnp.arange(-MAX_DISTANCE, MAX_DISTANCE + 1, dtype=jnp.int32)
    return rel_bias.astype(F32)[_t5_bucket(rel)]


def _win_attn_kernel(q_ref, kp_ref, kc_ref, kn_ref, vp_ref, vc_ref, vn_ref, bias_ref, sink_ref, o_ref):
    i = pl.program_id(1)
    nb = pl.num_programs(1)
    col = lax.broadcasted_iota(jnp.int32, (1, 3 * BLOCK), 1)
    invalid = ((col < BLOCK) & (i == 0)) | ((col >= 2 * BLOCK) & (i == nb - 1))
    scale = HEAD_DIM ** -0.5
    for hk in range(A_KV_HEADS):
        ks = slice(hk * HEAD_DIM, (hk + 1) * HEAD_DIM)
        qh = jnp.concatenate(
            [q_ref[:, (hk * A_GROUP + g) * HEAD_DIM:(hk * A_GROUP + g + 1) * HEAD_DIM]
             for g in range(A_GROUP)], axis=0)
        kb = jnp.concatenate([kp_ref[:, ks], kc_ref[:, ks], kn_ref[:, ks]], axis=0)
        vb = jnp.concatenate([vp_ref[:, ks], vc_ref[:, ks], vn_ref[:, ks]], axis=0)
        s = _dot_nt(qh, kb) * scale + bias_ref[hk]
        s = jnp.where(invalid, -jnp.inf, s)
        sink = sink_ref[hk]
        m = jnp.maximum(jnp.max(s, axis=-1, keepdims=True), sink)
        p = jnp.exp(s - m)
        den = jnp.sum(p, axis=-1, keepdims=True) + jnp.exp(sink - m)
        p = (p / den).astype(BF16)
        o = _dot(p, vb)
        for g in range(A_GROUP):
            h = hk * A_GROUP + g
            o_ref[:, h * HEAD_DIM:(h + 1) * HEAD_DIM] = o[g * BLOCK:(g + 1) * BLOCK].astype(o_ref.dtype)


def window_attention(z, sinks, tab):
    B, S, _ = z.shape
    nb = S // BLOCK
    q_loc = jnp.arange(BLOCK)[:, None]
    k_loc = jnp.arange(3 * BLOCK)[None, :] - BLOCK
    rel = k_loc - q_loc
    bias = jnp.where((jnp.abs(rel) <= WINDOW)[..., None],
                     tab[jnp.clip(rel, -MAX_DISTANCE, MAX_DISTANCE) + MAX_DISTANCE], -jnp.inf)
    bias = jnp.moveaxis(bias, -1, 0).reshape(A_KV_HEADS, A_GROUP * BLOCK, 3 * BLOCK)
    sink = jnp.broadcast_to(sinks.astype(F32).reshape(A_KV_HEADS, A_GROUP, 1, 1),
                            (A_KV_HEADS, A_GROUP, BLOCK, 1)).reshape(A_KV_HEADS, A_GROUP * BLOCK, 1)
    kcol = A_Q // A_KV
    vcol = (A_Q + A_KV) // A_KV
    prev = lambda b, i: (b, jnp.maximum(i - 1, 0))
    nxt = lambda b, i: (b, jnp.minimum(i + 1, nb - 1))
    return pl.pallas_call(
        _win_attn_kernel,
        grid=(B, nb),
        in_specs=[
            pl.BlockSpec((None, BLOCK, A_Q), lambda b, i: (b, i, 0)),
            pl.BlockSpec((None, BLOCK, A_KV), lambda b, i: (*prev(b, i), kcol)),
            pl.BlockSpec((None, BLOCK, A_KV), lambda b, i: (b, i, kcol)),
            pl.BlockSpec((None, BLOCK, A_KV), lambda b, i: (*nxt(b, i), kcol)),
            pl.BlockSpec((None, BLOCK, A_KV), lambda b, i: (*prev(b, i), vcol)),
            pl.BlockSpec((None, BLOCK, A_KV), lambda b, i: (b, i, vcol)),
            pl.BlockSpec((None, BLOCK, A_KV), lambda b, i: (*nxt(b, i), vcol)),
            pl.BlockSpec((A_KV_HEADS, A_GROUP * BLOCK, 3 * BLOCK), lambda b, i: (0, 0, 0)),
            pl.BlockSpec((A_KV_HEADS, A_GROUP * BLOCK, 1), lambda b, i: (0, 0, 0)),
        ],
        out_specs=pl.BlockSpec((None, BLOCK, A_Q), lambda b, i: (b, i, 0)),
        out_shape=jax.ShapeDtypeStruct((B, S, A_Q), BF16),
        compiler_params=_cp("parallel", "arbitrary"),
        name="window_attn",
    )(z, z, z, z, z, z, z, bias, sink)


CONV_TS = 512
CONV_TC = 512
CONV_HALO = 16


def _conv_kernel(prev_ref, cur_ref, next_ref, w_ref, o_ref, ext_ref):
    i = pl.program_id(1)
    j = pl.program_id(2)
    ns = pl.num_programs(1)
    ts = cur_ref.shape[0]
    pv = jnp.where(i == 0, 0.0, prev_ref[...].astype(F32))
    nx = jnp.where(i == ns - 1, 0.0, next_ref[...].astype(F32))
    ext_ref[0:CONV_HALO, :] = pv
    ext_ref[CONV_HALO:CONV_HALO + ts, :] = cur_ref[...].astype(F32)
    ext_ref[CONV_HALO + ts:, :] = nx
    half = CONV_WIDTH // 2
    acc = None
    for t in range(CONV_WIDTH):
        term = w_ref[t:t + 1, :] * ext_ref[pl.ds(CONV_HALO - half + t, ts), :]
        acc = term if acc is None else acc + term
    y = _silu(acc)
    heads_per_step = CONV_TC // B_DK
    q_steps = B_QK // CONV_TC

    def l2(scale):
        for hh in range(heads_per_step):
            seg = y[:, hh * B_DK:(hh + 1) * B_DK]
            r = lax.rsqrt(jnp.sum(seg * seg, axis=-1, keepdims=True) + EPS)
            o_ref[:, hh * B_DK:(hh + 1) * B_DK] = seg * (r * scale)

    @pl.when(j < q_steps)
    def _():
        l2(B_DK ** -0.5)

    @pl.when((j >= q_steps) & (j < 2 * q_steps))
    def _():
        l2(1.0)

    @pl.when(j >= 2 * q_steps)
    def _():
        o_ref[...] = y


def conv_qkv(z, conv_w):
    B, S, _ = z.shape
    ts = min(CONV_TS, S)
    c0 = (A_Q + 2 * A_KV) // CONV_TC
    hb = ts // CONV_HALO
    nh = S // CONV_HALO
    return pl.pallas_call(
        _conv_kernel,
        grid=(B, S // ts, B_CONV // CONV_TC),
        in_specs=[
            pl.BlockSpec((None, CONV_HALO, CONV_TC), lambda b, i, j: (b, jnp.maximum(i * hb - 1, 0), c0 + j)),
            pl.BlockSpec((None, ts, CONV_TC), lambda b, i, j: (b, i, c0 + j)),
            pl.BlockSpec((None, CONV_HALO, CONV_TC), lambda b, i, j: (b, jnp.minimum((i + 1) * hb, nh - 1), c0 + j)),
            pl.BlockSpec((CONV_WIDTH, CONV_TC), lambda b, i, j: (0, j)),
        ],
        out_specs=pl.BlockSpec((None, ts, CONV_TC), lambda b, i, j: (b, i, j)),
        out_shape=jax.ShapeDtypeStruct((B, S, B_CONV), F32),
        scratch_shapes=[pltpu.VMEM((ts + 2 * CONV_HALO, CONV_TC), F32)],
        compiler_params=_cp("parallel", "parallel", "arbitrary"),
        name="conv_qkv",
    )(z, z, z, conv_w.astype(F32))


def _bdot(a, b):
    return _dot(a.astype(DELTA_MXU), b.astype(DELTA_MXU), DELTA_PREC)


def _delta_chains(ch, eye, lmask_ref):
    n = ch[0]["q"].shape[0]
    for c in ch:
        c["decay"] = jnp.exp(jnp.where(c["causal"], c["gc_col"] - c["gc_row"], -jnp.inf))
        c["kb"] = c["k"] * c["beta"]
        c["k16"] = c["k"].astype(DELTA_MXU)
    for c in ch:
        c["m"] = jnp.where(c["strict"], _dot_nt(c["kb"].astype(DELTA_MXU), c["k16"], DELTA_PREC) * c["decay"], 0.0)
    for c in ch:
        c["inv"] = eye - c["m"] * lmask_ref[0]
    for l in range(1, lmask_ref.shape[0]):
        for c in ch:
            c["i16"] = c["inv"].astype(DELTA_MXU)
            c["p"] = _bdot(c["i16"], c["m"] * lmask_ref[l])
        for c in ch:
            c["inv"] = c["inv"] - _bdot(c["p"], c["i16"])
    for c in ch:
        c["eg"] = jnp.exp(c["gc_col"])
        c["sol"] = _bdot(c["inv"], jnp.concatenate([c["v"] * c["beta"], c["kb"] * c["eg"]], axis=1))
        c["a"] = _dot_nt(c["q"].astype(DELTA_MXU), c["k16"], DELTA_PREC) * c["decay"]
    for c in ch:
        w = c["sol"][:, B_DV:]
        c["ws"] = _bdot(jnp.concatenate([w, c["q"] * c["eg"]], axis=0), c["state"])
    out = []
    for c in ch:
        v_new = c["sol"][:, :B_DV] - c["ws"][:n]
        v16 = v_new.astype(DELTA_MXU)
        o = c["ws"][n:] + _bdot(c["a"], v16)
        k_tail = (c["k"] * jnp.exp(c["g_end"] - c["gc_col"])).astype(DELTA_MXU)
        state = c["state"] * jnp.exp(c["g_end"]) + lax.dot_general(
            k_tail, v16, (((0,), (0,)), ((), ())), precision=DELTA_PREC, preferred_element_type=F32)
        out.append((o, state))
    return out


def _delta_kernel(qf_ref, kf_ref, vf_ref, glf_ref, gtf_ref, qb_ref, kb_ref, vb_ref, glb_ref, gtb_ref,
                  arow_ref, drow_ref, acol_ref, dcol_ref, lmask_ref, of_ref, ob_ref, st_ref):
    c = pl.program_id(2)

    @pl.when(c == 0)
    def _():
        st_ref[...] = jnp.zeros_like(st_ref)

    CHUNK = DELTA_CH
    ri = lax.broadcasted_iota(jnp.int32, (CHUNK, CHUNK), 0)
    ci = lax.broadcasted_iota(jnp.int32, (CHUNK, CHUNK), 1)
    eye = (ri == ci).astype(F32)
    lower = (ri >= ci).astype(F32)
    upper = (ri <= ci).astype(F32)
    hb = DELTA_HB

    chains, sinks = [], []
    for d, (q_ref, k_ref, v_ref, gl_ref, gt_ref, o_ref) in enumerate(
            ((qf_ref, kf_ref, vf_ref, glf_ref, gtf_ref, of_ref),
             (qb_ref, kb_ref, vb_ref, glb_ref, gtb_ref, ob_ref))):
        fwd = d == 0
        gl = gl_ref[...]
        gt = gt_ref[...]
        beta_all = _sigmoid(gl)
        g_all = arow_ref[...] * _softplus(gl + drow_ref[...])
        gT_all = acol_ref[...] * _softplus(gt + dcol_ref[...])
        if fwd:
            gc_all = _dot(lower, g_all, HI)
            gcT_all = _dot(gT_all, upper, HI)
            causal, strict = ri >= ci, ri > ci
        else:
            gc_all = _dot(upper, g_all, HI)
            gcT_all = _dot(gT_all, lower, HI)
            causal, strict = ri <= ci, ri < ci
        end = CHUNK - 1 if fwd else 0
        for hh in range(hb):
            bl = d * hb + hh
            al = (2 + d) * hb + hh
            sl = slice(hh * B_DK, (hh + 1) * B_DK)
            gc_col = gc_all[:, al:al + 1]
            chains.append(dict(
                q=q_ref[:, sl], k=k_ref[:, sl], v=v_ref[:, sl], beta=beta_all[:, bl:bl + 1],
                gc_col=gc_col, gc_row=gcT_all[al:al + 1, :], g_end=gc_col[end:end + 1, :],
                state=st_ref[d * hb + hh], causal=causal, strict=strict))
            sinks.append((o_ref, sl, d * hb + hh))
    for (o, st), (o_ref, sl, si) in zip(_delta_chains(chains, eye, lmask_ref), sinks):
        o_ref[:, sl] = o
        st_ref[si] = st


def delta_rule(qkv, gl, a_log, dt_bias):
    B, S, _ = qkv.shape
    CHUNK = DELTA_CH
    n = S // CHUNK
    hb, hg = DELTA_HB, DELTA_HG
    ri = jnp.arange(CHUNK)[:, None]
    ci = jnp.arange(CHUNK)[None, :]
    lmask = jnp.stack([((ri >> (l + 1)) == (ci >> (l + 1))) & ((ri >> l) != (ci >> l))
                       for l in range(int(math.log2(CHUNK)))]).astype(F32)
    gw = 4 * hb
    g4 = gl[..., :N_GATE].reshape(B, S, 4, hg, hb)
    g4 = jnp.transpose(g4, (0, 3, 1, 2, 4)).reshape(B, hg, S, gw)
    g_rows = jnp.pad(g4, ((0, 0), (0, 0), (0, 0), (0, LANE - gw)))
    g_cols = jnp.transpose(g4.reshape(B, hg, n, CHUNK, gw), (0, 1, 2, 4, 3))
    neg_a = -jnp.exp(a_log.astype(F32)).reshape(2, hg, hb)
    dtb = dt_bias.astype(F32).reshape(2, hg, hb)
    zeros = jnp.zeros((2, hg, hb), F32)
    a4 = jnp.transpose(jnp.concatenate([zeros, neg_a], 0), (1, 0, 2)).reshape(hg, gw)
    d4 = jnp.transpose(jnp.concatenate([zeros, dtb], 0), (1, 0, 2)).reshape(hg, gw)
    arow = jnp.pad(a4, ((0, 0), (0, LANE - gw))).reshape(hg, 1, LANE)
    drow = jnp.pad(d4, ((0, 0), (0, LANE - gw))).reshape(hg, 1, LANE)
    acol = a4.reshape(hg, gw, 1)
    dcol = d4.reshape(hg, gw, 1)
    cw = hb * B_DK
    kq, kk, kv = 0, B_QK // cw, 2 * B_QK // cw

    def seq_specs(cidx):
        return [
            pl.BlockSpec((None, CHUNK, cw), lambda b, g, c: (b, cidx(c), kq + g)),
            pl.BlockSpec((None, CHUNK, cw), lambda b, g, c: (b, cidx(c), kk + g)),
            pl.BlockSpec((None, CHUNK, cw), lambda b, g, c: (b, cidx(c), kv + g)),
            pl.BlockSpec((None, None, CHUNK, LANE), lambda b, g, c: (b, g, cidx(c), 0)),
            pl.BlockSpec((None, None, None, gw, CHUNK), lambda b, g, c: (b, g, cidx(c), 0, 0)),
        ]

    fwd_idx = lambda c: c
    bwd_idx = lambda c: n - 1 - c
    par_specs = [
        pl.BlockSpec((None, 1, LANE), lambda b, g, c: (g, 0, 0)),
        pl.BlockSpec((None, 1, LANE), lambda b, g, c: (g, 0, 0)),
        pl.BlockSpec((None, gw, 1), lambda b, g, c: (g, 0, 0)),
        pl.BlockSpec((None, gw, 1), lambda b, g, c: (g, 0, 0)),
        pl.BlockSpec(lmask.shape, lambda b, g, c: (0, 0, 0)),
    ]
    o_f, o_b = pl.pallas_call(
        _delta_kernel,
        grid=(B, hg, n),
        in_specs=seq_specs(fwd_idx) + seq_specs(bwd_idx) + par_specs,
        out_specs=[
            pl.BlockSpec((None, CHUNK, cw), lambda b, g, c: (b, fwd_idx(c), g)),
            pl.BlockSpec((None, CHUNK, cw), lambda b, g, c: (b, bwd_idx(c), g)),
        ],
        out_shape=[jax.ShapeDtypeStruct((B, S, B_V), F32)] * 2,
        scratch_shapes=[pltpu.VMEM((2 * hb, B_DK, B_DV), F32)],
        compiler_params=_cp("parallel", "parallel", "arbitrary"),
        name="delta_rule",
    )(qkv, qkv, qkv, g_rows, g_cols, qkv, qkv, qkv, g_rows, g_cols, arow, drow, acol, dcol, lmask)
    return o_f, o_b


GATE_TC = 1024


def _delta_out_kernel(of_ref, ob_ref, zb_ref, g_ref, o_ref):
    for hh in range(GATE_TC // B_DV):
        sl = slice(hh * B_DV, (hh + 1) * B_DV)
        ob = of_ref[:, sl] + ob_ref[:, sl]
        y = _rms_rows(ob, g_ref[...])
        o_ref[:, sl] = (y * _silu(zb_ref[:, sl].astype(F32))).astype(o_ref.dtype)


def delta_out(o_f, o_b, z2d, onorm, *, tm=512):
    T = o_f.shape[0]
    tm = min(tm, T)
    c0 = (A_Q + 2 * A_KV + B_CONV) // GATE_TC
    return pl.pallas_call(
        _delta_out_kernel,
        grid=(T // tm, B_V // GATE_TC),
        in_specs=[
            pl.BlockSpec((tm, GATE_TC), lambda i, j: (i, j)),
            pl.BlockSpec((tm, GATE_TC), lambda i, j: (i, j)),
            pl.BlockSpec((tm, GATE_TC), lambda i, j: (i, c0 + j)),
            pl.BlockSpec((1, B_DV), lambda i, j: (0, 0)),
        ],
        out_specs=pl.BlockSpec((tm, GATE_TC), lambda i, j: (i, j)),
        out_shape=jax.ShapeDtypeStruct((T, B_V), BF16),
        compiler_params=_cp("parallel", "arbitrary"),
        name="delta_out",
    )(o_f, o_b, z2d, onorm.reshape(1, B_DV).astype(F32))


DIFF_TQ = 1024
DIFF_TK = 512
DIFF_UNIT_ROWS = 256


def _diff_attn_kernel(lam_ref, far_ref, q_ref, k_ref, v_ref, b_ref, g_ref, o_ref, m_ref, l_ref, acc_ref, *, out_scale, tk):
    h = pl.program_id(1)
    qi = pl.program_id(2)
    nk = k_ref.shape[0] // tk
    r = q_ref.shape[0] // tk
    base = qi * r

    m_ref[...] = jnp.full_like(m_ref, -jnp.inf)
    l_ref[...] = jnp.zeros_like(l_ref)
    acc_ref[...] = jnp.zeros_like(acc_ref)

    def step(kj, near_dj, far_const):
        rows = pl.ds(pl.multiple_of(kj * tk, tk), tk)
        v = v_ref[rows, :]
        ru = DIFF_UNIT_ROWS
        units = [(u, t) for u in range(q_ref.shape[0] // ru) for t in range(2)]

        def scores(u, t):
            sl = slice(t * C_QK_DIM, (t + 1) * C_QK_DIM)
            return _dot_nt(q_ref[u * ru:(u + 1) * ru, sl], k_ref[rows, sl])

        s_next = scores(*units[0])
        for i, (u, t) in enumerate(units):
            s = s_next
            if i + 1 < len(units):
                s_next = scores(*units[i + 1])
            qr = slice(u * ru, (u + 1) * ru)
            c = (u * ru) // tk
            off = None if near_dj is None else near_dj - c
            if off is None:
                const = far_const
            elif abs(off) <= 1:
                const = None
                s = s + b_ref[off + 1, (u * ru) % tk:(u * ru) % tk + ru, :]
            else:
                const = far_ref[0 if off < 0 else 1, h]
            m_loc = jnp.max(s, axis=-1, keepdims=True)
            if const is not None:
                m_loc = m_loc + const
            m_old = m_ref[t, qr, :]
            m_new = jnp.maximum(m_old, m_loc)
            alpha = jnp.exp2(m_old - m_new)
            shift = m_new if const is None else m_new - const
            p = jnp.exp2(s - jnp.concatenate([shift] * (tk // LANE), axis=1))
            psum = p[:, :LANE]
            for cc in range(1, tk // LANE):
                psum = psum + p[:, cc * LANE:(cc + 1) * LANE]
            l_ref[t, qr, :] = alpha * l_ref[t, qr, :] + psum
            acc_ref[t, qr, :] = (jnp.concatenate([alpha] * (C_V_DIM // LANE), axis=1) * acc_ref[t, qr, :]
                                 + _dot(p.astype(BF16), v))
            m_ref[t, qr, :] = m_new

    def far_left(kj, carry):
        step(kj, None, far_ref[0, h])
        return carry

    def far_right(kj, carry):
        step(kj, None, far_ref[1, h])
        return carry

    lax.fori_loop(0, jnp.maximum(base - 1, 0), far_left, 0)
    for dj in range(-1, r + 1):
        @pl.when((base + dj >= 0) & (base + dj < nk))
        def _():
            step(base + dj, dj, None)
    lax.fori_loop(jnp.minimum(base + r + 1, nk), nk, far_right, 0)

    lam = lam_ref[0, 0]
    r0 = 1.0 / jnp.sum(l_ref[0], axis=-1, keepdims=True)
    r1 = 1.0 / jnp.sum(l_ref[1], axis=-1, keepdims=True)
    o = acc_ref[0] * r0 - acc_ref[1] * (lam * r1)
    o_ref[...] = (_rms_rows(o, g_ref[...]) * out_scale).astype(o_ref.dtype)


def diff_attention(z, lam, tab, subln, lam_init):
    B, S, _ = z.shape
    tq, tk = min(DIFF_TQ, S), min(DIFF_TK, S)
    assert tk > MAX_DISTANCE and tq % tk == 0
    nq = S // tq
    r = tq // tk
    tab2 = tab.T * LOG2E
    nd = 3
    period = 2 * tk + 1
    u = jnp.arange(period)[None, :]
    d = jnp.arange(-1, 2)[:, None]
    idx = jnp.clip(d * tk + u - tk, -MAX_DISTANCE, MAX_DISTANCE) + MAX_DISTANCE
    sig = tab2[:, idx]
    skew = jnp.tile(sig, (1, 1, tk))[..., :tk * (period - 1)].reshape(C_HEADS, nd, tk, period - 1)
    btile = skew[..., tk:2 * tk]
    far = jnp.stack([tab2[:, 0], tab2[:, -1]])
    hw = 2 * C_QK_DIM
    return pl.pallas_call(
        functools.partial(_diff_attn_kernel, out_scale=1.0 - lam_init, tk=tk),
        grid=(B, C_HEADS, nq),
        in_specs=[
            pl.BlockSpec(memory_space=pltpu.SMEM),
            pl.BlockSpec(memory_space=pltpu.SMEM),
            pl.BlockSpec((None, tq, hw), lambda b, h, i: (b, i, h)),
            pl.BlockSpec((None, S, hw), lambda b, h, i: (b, 0, C_HEADS + h)),
            pl.BlockSpec((None, S, hw), lambda b, h, i: (b, 0, 2 * C_HEADS + h)),
            pl.BlockSpec((None, nd, tk, tk), lambda b, h, i: (h, 0, 0, 0)),
            pl.BlockSpec((1, C_V_DIM), lambda b, h, i: (0, 0)),
        ],
        out_specs=pl.BlockSpec((None, tq, C_V_DIM), lambda b, h, i: (b, i, h)),
        out_shape=jax.ShapeDtypeStruct((B, S, C_HEADS * C_V_DIM), BF16),
        scratch_shapes=[pltpu.VMEM((2, tq, LANE), F32), pltpu.VMEM((2, tq, LANE), F32),
                        pltpu.VMEM((2, tq, C_V_DIM), F32)],
        compiler_params=_cp("parallel", "parallel", "arbitrary"),
        name="diff_attn",
    )(lam.reshape(1, 1).astype(F32), far, z, z, z, btile, subln.reshape(1, C_V_DIM).astype(F32))


def _mem_attn_kernel(x_ref, g_ref, wq_ref, kv_ref, wo_ref, o_ref):
    x = x_ref[...]
    h = _rms_rows(x, g_ref[...]).astype(BF16)
    q = _dot(h, wq_ref[...]).astype(BF16)
    outs = []
    for hd in range(X_HEADS):
        sl = slice(hd * X_HEAD_DIM, (hd + 1) * X_HEAD_DIM)
        s = _dot_nt(q[:, sl], kv_ref[:, sl]) * (X_HEAD_DIM ** -0.5)
        m = jnp.max(s, axis=-1, keepdims=True)
        e = jnp.exp(s - m)
        p = (e / jnp.sum(e, axis=-1, keepdims=True)).astype(BF16)
        outs.append(_dot(p, kv_ref[:, X_WIDTH + hd * X_HEAD_DIM:X_WIDTH + (hd + 1) * X_HEAD_DIM]))
    o = jnp.concatenate(outs, axis=1).astype(BF16)
    o_ref[...] = x + _dot(o, wo_ref[...])


def memory_attention(x, kv, g, wq, wo, *, tm=256):
    B, S, D = x.shape
    tm = min(tm, S)
    return pl.pallas_call(
        _mem_attn_kernel,
        grid=(B, S // tm),
        in_specs=[
            pl.BlockSpec((None, tm, D), lambda b, i: (b, i, 0)),
            pl.BlockSpec((1, D), lambda b, i: (0, 0)),
            pl.BlockSpec((D, X_WIDTH), lambda b, i: (0, 0)),
            pl.BlockSpec((None, MEM_TOKENS, 2 * X_WIDTH), lambda b, i: (b, 0, 0)),
            pl.BlockSpec((X_WIDTH, D), lambda b, i: (0, 0)),
        ],
        out_specs=pl.BlockSpec((None, tm, D), lambda b, i: (b, i, 0)),
        out_shape=jax.ShapeDtypeStruct((B, S, D), F32),
        compiler_params=_cp("parallel", "arbitrary"),
        name="mem_attn",
    )(x, g.reshape(1, D).astype(F32), wq, kv, wo)


def _first_max_onehot(vals, rows):
    m = jnp.max(vals, axis=0, keepdims=True)
    idx = jnp.min(jnp.where(vals == m, rows, vals.shape[0]), axis=0, keepdims=True)
    return rows == idx


def _router_kernel(x_ref, g_ref, w_ref, b_ref, o_ref, h_ref):
    tm = x_ref.shape[0]

    def body(r, c):
        sl = pl.ds(pl.multiple_of(r * LANE, LANE), LANE)
        h_ref[sl, :] = _rms_rows(x_ref[sl, :], g_ref[...])
        return c
    lax.fori_loop(0, tm // LANE, body, 0)
    logits = _dot(h_ref[...], w_ref[...], HI)
    lt = jnp.transpose(logits)[:N_EXPERTS, :]
    scores = _sigmoid(lt)
    biased = scores + b_ref[...]
    per_group = N_EXPERTS // N_GROUPS
    rows8 = lax.broadcasted_iota(jnp.int32, (per_group, tm), 0)
    gscore = []
    for gi in range(N_GROUPS):
        blk = biased[gi * per_group:(gi + 1) * per_group]
        first = _first_max_onehot(blk, rows8)
        m1 = jnp.max(blk, axis=0, keepdims=True)
        m2 = jnp.max(jnp.where(first, -jnp.inf, blk), axis=0, keepdims=True)
        gscore.append(m1 + m2)
    gscore = jnp.concatenate(gscore, axis=0)
    growi = lax.broadcasted_iota(jnp.int32, (N_GROUPS, tm), 0)
    gsel = jnp.zeros((N_GROUPS, tm), jnp.bool_)
    work = gscore
    for _ in range(TOPK_GROUPS):
        oh = _first_max_onehot(work, growi)
        gsel = gsel | oh
        work = jnp.where(oh, -jnp.inf, work)
    gself = gsel.astype(F32)
    masked = jnp.concatenate(
        [jnp.where(gself[gi:gi + 1] > 0.0, biased[gi * per_group:(gi + 1) * per_group], -jnp.inf)
         for gi in range(N_GROUPS)], axis=0)
    erow = lax.broadcasted_iota(jnp.int32, (N_EXPERTS, tm), 0)
    esel = jnp.zeros((N_EXPERTS, tm), jnp.bool_)
    work = masked
    for _ in range(TOP_K):
        oh = _first_max_onehot(work, erow)
        esel = esel | oh
        work = jnp.where(oh, -jnp.inf, work)
    w = jnp.where(esel, scores, 0.0)
    gates = w / jnp.sum(w, axis=0, keepdims=True) * ROUTED_SCALE
    n_shared = SHARED_FF // EXPERT_FF
    full = jnp.concatenate([gates, jnp.ones((8, tm), F32), jnp.zeros((LANE - N_EXPERTS - 8, tm), F32)], axis=0)
    frow = lax.broadcasted_iota(jnp.int32, (LANE, tm), 0)
    full = jnp.where(frow < N_EXPERTS + n_shared, full, 0.0)
    o_ref[...] = jnp.transpose(full)


def route_tokens(x, g, w_router, r_bias, *, tm=512):
    T, D = x.shape
    tm = min(tm, T)
    w = jnp.pad(w_router.astype(F32), ((0, 0), (0, LANE - N_EXPERTS)))
    return pl.pallas_call(
        _router_kernel,
        grid=(T // tm,),
        in_specs=[
            pl.BlockSpec((tm, D), lambda i: (i, 0)),
            pl.BlockSpec((1, D), lambda i: (0, 0)),
            pl.BlockSpec((D, LANE), lambda i: (0, 0)),
            pl.BlockSpec((N_EXPERTS, 1), lambda i: (0, 0)),
        ],
        out_specs=pl.BlockSpec((tm, LANE), lambda i: (i, 0)),
        out_shape=jax.ShapeDtypeStruct((T, LANE), F32),
        scratch_shapes=[pltpu.VMEM((tm, D), F32)],
        compiler_params=_cp("parallel"),
        name="moe_router",
    )(x, g.reshape(1, D).astype(F32), w, r_bias.astype(F32).reshape(N_EXPERTS, 1))


def _pattern_rank_table():
    def revolving(n, k):
        if k == 0:
            return [[]]
        if k == n:
            return [list(range(n))]
        return revolving(n - 1, k) + [c + [n - 1] for c in reversed(revolving(n - 1, k - 1))]

    combos = [sum(1 << g for g in c) for c in revolving(N_GROUPS, TOPK_GROUPS)]
    table = []
    for m in range(1 << N_GROUPS):
        ranks = [i for i, c in enumerate(combos) if (c & m) == m]
        table.append(min(ranks) if ranks else len(combos))
    return table


_PATTERN_RANK = _pattern_rank_table()


def _moe_plan(gates, tm):
    T = gates.shape[0]
    nt = T // tm
    per_group = N_EXPERTS // N_GROUPS
    sel = gates[:, :N_EXPERTS] > 0.0
    gsel = jnp.any(sel.reshape(T, N_GROUPS, per_group), axis=-1)
    gmask = jnp.sum(gsel.astype(jnp.int32) << jnp.arange(N_GROUPS, dtype=jnp.int32), axis=-1)
    order = jnp.argsort(jnp.asarray(_PATTERN_RANK, jnp.int32)[gmask]).astype(jnp.int32)
    gates_s = gates[order]
    nr = N_EXPERTS // MOE_EPS
    used = jnp.any((gates_s[:, :N_EXPERTS] > 0.0).reshape(nt, tm, nr, MOE_EPS), axis=(1, 3))
    used = jnp.concatenate([used, jnp.ones((nt, N_PAIRS - nr), jnp.bool_)], axis=1)
    fetch = lax.cummin(jnp.where(used, jnp.arange(N_PAIRS, dtype=jnp.int32), N_PAIRS - 1), axis=1, reverse=True)
    return order, gates_s, used.astype(jnp.int32).reshape(-1), fetch.astype(jnp.int32).reshape(-1)


def _moe_kernel(order_ref, used_ref, fetch_ref, x_hbm, g_ref, gate_ref, wgu_ref, wd_ref, gf_ref, o_hbm,
                acc_ref, xin_ref, h_ref, sem_ref, *, tm, final_norm):
    i = pl.program_id(0)
    p = pl.program_id(1)
    nt = pl.num_programs(0)
    npair = pl.num_programs(1)

    def for_rows(fn):
        def body(r, c):
            fn(r)
            return c
        lax.fori_loop(0, tm, body, 0, unroll=8)

    def gather_start(tile):
        for_rows(lambda r: pltpu.make_async_copy(
            x_hbm.at[pl.ds(order_ref[tile * tm + r], 1)], xin_ref.at[pl.ds(r, 1)], sem_ref.at[0]).start())

    def gather_wait(tile):
        pltpu.make_async_copy(x_hbm.at[pl.ds(0, tm)], xin_ref, sem_ref.at[0]).wait()

    def scatter_start(tile):
        for_rows(lambda r: pltpu.make_async_copy(
            acc_ref.at[pl.ds(r, 1)], o_hbm.at[pl.ds(order_ref[tile * tm + r], 1)], sem_ref.at[1]).start())

    def scatter_wait(tile):
        pltpu.make_async_copy(acc_ref, o_hbm.at[pl.ds(0, tm)], sem_ref.at[1]).wait()

    @pl.when(p == 0)
    def _():
        @pl.when(i == 0)
        def _():
            gather_start(i)
        gather_wait(i)

        def body(r, c):
            sl = pl.ds(pl.multiple_of(r * LANE, LANE), LANE)
            h_ref[sl, :] = _rms_rows(xin_ref[sl, :], g_ref[...]).astype(BF16)
            return c
        lax.fori_loop(0, tm // LANE, body, 0)

        @pl.when(i > 0)
        def _():
            scatter_wait(i - 1)
        acc_ref[...] = xin_ref[...]

        @pl.when(i + 1 < nt)
        def _():
            gather_start(i + 1)

    @pl.when(used_ref[i * npair + p] != 0)
    def _():
        gu = _dot(h_ref[...], wgu_ref[...])
        gates = gate_ref[...]
        lane = lax.broadcasted_iota(jnp.int32, gates.shape, 1)
        F = EXPERT_FF
        hid = []
        for e in range(MOE_EPS):
            ge = jnp.sum(jnp.where(lane == MOE_EPS * p + e, gates, 0.0), axis=1, keepdims=True)
            hid.append(_silu(gu[:, e * F:(e + 1) * F]) * gu[:, (MOE_EPS + e) * F:(MOE_EPS + 1 + e) * F] * ge)
        hid = jnp.concatenate(hid, axis=1).astype(BF16)
        acc_ref[...] += _dot(hid, wd_ref[...])

    @pl.when(p == npair - 1)
    def _():
        if final_norm:
            def body(r, c):
                sl = pl.ds(pl.multiple_of(r * LANE, LANE), LANE)
                acc_ref[sl, :] = _rms_rows(acc_ref[sl, :], gf_ref[...])
                return c
            lax.fori_loop(0, tm // LANE, body, 0)
        scatter_start(i)

        @pl.when(i == nt - 1)
        def _():
            scatter_wait(i)


def moe_experts(x, g, gates, w_gu, w_d, g_final, *, layer, final_norm, tm=512):
    T, D = x.shape
    assert T % tm == 0
    order, gates_s, used, fetch = _moe_plan(gates, tm)
    wblock = lambda i, p, order, used, fetch: (layer * N_PAIRS + fetch[i * N_PAIRS + p], 0, 0)
    any_spec = pl.BlockSpec(memory_space=pl.ANY)
    return pl.pallas_call(
        functools.partial(_moe_kernel, tm=tm, final_norm=final_norm),
        grid_spec=pltpu.PrefetchScalarGridSpec(
            num_scalar_prefetch=3,
            grid=(T // tm, N_PAIRS),
            in_specs=[
                any_spec,
                pl.BlockSpec((1, D), lambda i, p, *_: (0, 0)),
                pl.BlockSpec((tm, LANE), lambda i, p, *_: (i, 0)),
                pl.BlockSpec((None, D, 2 * MOE_EPS * EXPERT_FF), wblock),
                pl.BlockSpec((None, MOE_EPS * EXPERT_FF, D), wblock),
                pl.BlockSpec((1, D), lambda i, p, *_: (0, 0)),
            ],
            out_specs=any_spec,
            scratch_shapes=[pltpu.VMEM((tm, D), F32), pltpu.VMEM((tm, D), F32), pltpu.VMEM((tm, D), BF16),
                            pltpu.SemaphoreType.DMA((2,))],
        ),
        out_shape=jax.ShapeDtypeStruct((T, D), F32),
        compiler_params=_cp("arbitrary", "arbitrary"),
        name="moe_experts",
    )(order, used, fetch, x, g.reshape(1, D).astype(F32), gates_s, w_gu, w_d, g_final.reshape(1, D).astype(F32))


def _moe_weights(w_gate, w_up, w_down, s_gate, s_up, s_down):
    L, _, D, F = w_gate.shape
    ns = SHARED_FF // F

    def grouped(w, s):
        sh = jnp.transpose(s.reshape(L, D, ns, F), (0, 2, 1, 3))
        return jnp.concatenate([w, sh], axis=1).reshape(L, N_PAIRS, MOE_EPS, D, F)

    gu = jnp.concatenate([grouped(w_gate, s_gate), grouped(w_up, s_up)], axis=2)
    w_gu = jnp.transpose(gu, (0, 1, 3, 2, 4)).reshape(L * N_PAIRS, D, 2 * MOE_EPS * F).astype(BF16)
    w_d = jnp.concatenate([w_down, s_down.reshape(L, ns, F, D)], axis=1).astype(BF16)
    return w_gu, w_d.reshape(L * N_PAIRS, MOE_EPS * F, D)


def kernel(x_prompt, x_sample, mem_prompt, mem_sample, rel_bias, norm_mix, norm_mem, norm_memkv, norm_ffn, norm_final, ev_w_in, ev_sinks, ev_conv, ev_a_log, ev_dt_bias, ev_onorm, ev_w_out, od_w_in, od_lambda, od_subln, od_w_out, mx_wq, mx_wkv, mx_wo, moe_router, moe_bias, moe_w_gate, moe_w_up, moe_w_down, sh_w_gate, sh_w_up, sh_w_down):
    tab = _bias_by_rel(rel_bias)

    w_gu, w_d = _moe_weights(moe_w_gate, moe_w_up, moe_w_down, sh_w_gate, sh_w_up, sh_w_down)
    layers = []
    for l in range(DEPTH):
        i = l // 2
        lw = {}
        if l % 2 == 0:
            w_in = ev_w_in[i]
            lw["w_in"] = w_in[:, :EVEN_MAIN].astype(BF16)
            lw["w_gate_tail"] = jnp.pad(w_in[:, EVEN_MAIN:], ((0, 0), (0, LANE - N_GATE))).astype(BF16)
            lw["w_out_a"] = ev_w_out[i][:A_Q].astype(BF16)
            lw["w_out_b"] = ev_w_out[i][A_Q:].astype(BF16)
        else:
            w_in = od_w_in[i]
            lw["w_in"] = jnp.concatenate([w_in[:, :C_QK] * (C_QK_DIM ** -0.5 * LOG2E), w_in[:, C_QK:]],
                                         axis=1).astype(BF16)
            lw["w_out"] = od_w_out[i].astype(BF16)
            lp = od_lambda[i].astype(F32)
            lam_init = 0.8 - 0.6 * math.exp(-0.3 * l)
            lw["lam_init"] = lam_init
            lw["lam"] = jnp.exp(jnp.sum(lp[0] * lp[1])) - jnp.exp(jnp.sum(lp[2] * lp[3])) + lam_init
        lw["wq"] = mx_wq[l].astype(BF16)
        lw["wkv"] = mx_wkv[l].astype(BF16)
        lw["wo"] = mx_wo[l].astype(BF16)
        layers.append(lw)

    def mix_and_route(x, mem2, B, S, l):
        T, D = x.shape
        lw = layers[l]
        i = l // 2
        if l % 2 == 0:
            z, gl = norm_mm(x, norm_mix[l], lw["w_in"], lw["w_gate_tail"])
            z3 = z.reshape(B, S, EVEN_MAIN)
            out_a = window_attention(z3, ev_sinks[i], tab)
            qkv = conv_qkv(z3, ev_conv[i])
            o_f, o_b = delta_rule(qkv, gl.reshape(B, S, LANE), ev_a_log[i], ev_dt_bias[i])
            out_b = delta_out(o_f.reshape(T, B_V), o_b.reshape(T, B_V), z, ev_onorm[i])
            x = mm_res(x, [out_a.reshape(T, A_Q), out_b], [lw["w_out_a"], lw["w_out_b"]])
        else:
            z = norm_mm(x, norm_mix[l], lw["w_in"])
            o = diff_attention(z.reshape(B, S, 3 * C_QK), lw["lam"], tab, od_subln[i], lw["lam_init"])
            x = mm_res(x, [o.reshape(T, C_HEADS * C_V_DIM)], [lw["w_out"]])
        kv = norm_mm(mem2, norm_memkv[l], lw["wkv"])
        x = memory_attention(x.reshape(B, S, D), kv.reshape(B, MEM_TOKENS, 2 * X_WIDTH),
                             norm_mem[l], lw["wq"], lw["wo"]).reshape(T, D)
        return x, route_tokens(x, norm_ffn[l], moe_router[l], moe_bias[l])

    groups = [(x_prompt, mem_prompt), (x_sample, mem_sample)]
    shapes = [x.shape for x, _ in groups]
    xs = [x.reshape(-1, x.shape[-1]) for x, _ in groups]
    mems = [m.reshape(-1, m.shape[-1]) for _, m in groups]
    for l in range(DEPTH):
        routed = [mix_and_route(x, m, sh[0], sh[1], l) for x, m, sh in zip(xs, mems, shapes)]
        xs = [moe_experts(x, norm_ffn[l], gt, w_gu, w_d, norm_final, layer=l, final_norm=(l == DEPTH - 1))
              for x, gt in routed]
    return tuple(x.reshape(sh) for x, sh in zip(xs, shapes))
```
